```python
import jax, jax.numpy as jnp
from jax import lax
import numpy as np

D_MODEL = 1024
BATCH = 2
SEQ = 16384
DEPTH = 4

ATT_HEADS = 8
ATT_HEAD_DIM = 64
ATT_WIDTH = ATT_HEADS * ATT_HEAD_DIM
MLSTM_HEADS = 4
MLSTM_HEAD_DIM = 128
MLSTM_WIDTH = MLSTM_HEADS * MLSTM_HEAD_DIM
CONV_WIDTH = 4
GMLP_GROUPS = 4
GMLP_GROUP_DIM = 128
GMLP_WIDTH = GMLP_GROUPS * GMLP_GROUP_DIM
CHUNK = 128
D_FF = 2816
FFN_RES = 0.5
N_BRANCH = 3
EPS = 1e-6

SPLITS = (3 * ATT_WIDTH, ATT_HEADS,
          2 * MLSTM_WIDTH, MLSTM_WIDTH, MLSTM_HEADS, MLSTM_HEADS, MLSTM_WIDTH,
          2 * GMLP_WIDTH, N_BRANCH * D_MODEL)
D_IN = 3 * ATT_WIDTH + ATT_HEADS + 3 * MLSTM_WIDTH + 2 * MLSTM_HEADS + MLSTM_WIDTH + 2 * GMLP_WIDTH + N_BRANCH * D_MODEL

kernel_name = 'hybrid_fox_mlstm_gmlp_macaron'


def rmsnorm(x, g):
    xf = x.astype(jnp.float32)
    y = xf * lax.rsqrt(jnp.mean(xf * xf, axis=-1, keepdims=True) + EPS)
    return (y * g.astype(jnp.float32)).astype(x.dtype)


def swiglu(h, w_gate, w_up, w_down):
    return (jax.nn.silu(h @ w_gate) * (h @ w_up)) @ w_down


def causal_depthwise_conv(x, w):
    return lax.conv_general_dilated(
        x, w[:, None, :].astype(x.dtype), window_strides=(1,),
        padding=[(CONV_WIDTH - 1, 0)], dimension_numbers=('NWC', 'WIO', 'NWC'),
        feature_group_count=x.shape[-1])


def forgetting_attention(q, k, v, f_pre):
    B, S, H, Dh = q.shape
    nblk = S // CHUNK
    logf = jax.nn.log_sigmoid(f_pre.astype(jnp.float32))
    c = jnp.transpose(jnp.cumsum(logf, axis=1), (0, 2, 1))
    q_blocks = q.reshape(B, nblk, CHUNK, H, Dh).transpose(1, 0, 3, 2, 4)
    c_blocks = c.reshape(B, H, nblk, CHUNK).transpose(2, 0, 1, 3)
    kpos = jnp.arange(S)
    scale = Dh ** -0.5

    def one_block(args):
        blk, qb, cb = args
        s = jnp.einsum('bhqd,bkhd->bhqk', qb, k, preferred_element_type=jnp.float32) * scale
        s = s + cb[..., :, None] - c[..., None, :]
        qpos = blk * CHUNK + jnp.arange(CHUNK)
        s = jnp.where(kpos[None, :] <= qpos[:, None], s, -jnp.inf)
        p = jax.nn.softmax(s, axis=-1)
        return jnp.einsum('bhqk,bkhd->bqhd', p.astype(v.dtype), v)

    out = lax.map(one_block, (jnp.arange(nblk), q_blocks, c_blocks))
    return out.transpose(1, 0, 2, 3, 4).reshape(B, S, H * Dh)


def mlstm_chunkwise(q, k, v, i_pre, f_pre):
    B, S, H, Dh = q.shape
    nc = S // CHUNK
    f32 = jnp.float32

    def to_chunks(a):
        a = a.astype(f32).reshape((B, nc, CHUNK, H) + a.shape[3:])
        return jnp.moveaxis(a, (1, 3), (0, 2))

    qc = to_chunks(q) * (Dh ** -0.5)
    kc = to_chunks(k)
    vc = to_chunks(v)
    lfc = to_chunks(jax.nn.log_sigmoid(f_pre.astype(f32)))
    lic = to_chunks(i_pre)
    causal = jnp.tril(jnp.ones((CHUNK, CHUNK), dtype=bool))

    def step(carry, inp):
        C, n, m = carry
        qb, kb, vb, lf, li = inp
        b = jnp.cumsum(lf, axis=-1)
        a = b + m[..., None]
        dmat = jnp.where(causal, b[..., :, None] - b[..., None, :] + li[..., None, :], -jnp.inf)
        m_t = jnp.maximum(a, jnp.max(dmat, axis=-1))
        w_intra = jnp.exp(dmat - m_t[..., None])
        w_inter = jnp.exp(a - m_t)
        qk = jnp.einsum('bhtd,bhsd->bhts', qb, kb) * w_intra
        num = jnp.einsum('bhts,bhsv->bhtv', qk, vb) + w_inter[..., None] * jnp.einsum('bhvd,bhtd->bhtv', C, qb)
        den = jnp.sum(qk, axis=-1) + w_inter * jnp.einsum('bhd,bhtd->bht', n, qb)
        h = num / jnp.maximum(jnp.abs(den), jnp.exp(-m_t))[..., None]
        bL = b[..., -1]
        g = bL[..., None] - b + li
        m_new = jnp.maximum(bL + m, jnp.max(g, axis=-1))
        w_s = jnp.exp(g - m_new[..., None])
        decay = jnp.exp(bL + m - m_new)
        C_new = decay[..., None, None] * C + jnp.einsum('bhs,bhsv,bhsd->bhvd', w_s, vb, kb)
        n_new = decay[..., None] * n + jnp.einsum('bhs,bhsd->bhd', w_s, kb)
        return (C_new, n_new, m_new), h

    init = (jnp.zeros((B, H, Dh, Dh), f32), jnp.zeros((B, H, Dh), f32), jnp.zeros((B, H), f32))
    _, hs = lax.scan(step, init, (qc, kc, vc, lfc, lic))
    return jnp.moveaxis(hs, (0, 2), (1, 3)).reshape(B, S, H, Dh)


def chunked_spatial_gating(u, v, w_s, b_s):
    B, S, _ = v.shape
    nc = S // CHUNK
    vc = v.reshape(B, nc, CHUNK, GMLP_GROUPS, GMLP_GROUP_DIM)
    w = jnp.where(jnp.tril(jnp.ones((CHUNK, CHUNK), dtype=bool)), w_s, 0.0).astype(v.dtype)
    mixed = jnp.einsum('gts,bcsgd->bctgd', w, vc) + b_s.T.astype(v.dtype)[None, None, :, :, None]
    return u * mixed.reshape(B, S, GMLP_WIDTH)


def token_mixer(h, w_in, b_f_att, b_i_ml, b_f_ml, conv_ml, norm_ml_head, norm_gmlp,
                w_spatial, b_spatial, w_br_att, w_br_ml, w_br_gmlp, w_out):
    B, S, _ = h.shape
    split_points = [int(p) for p in np.cumsum(SPLITS)[:-1]]
    proj = h @ w_in
    att_qkv, att_f, ml_qk, ml_v, ml_i, ml_f, ml_o, gm_uv, gates = jnp.split(proj, split_points, axis=-1)

    q_a, k_a, v_a = [t.reshape(B, S, ATT_HEADS, ATT_HEAD_DIM) for t in jnp.split(att_qkv, 3, axis=-1)]
    y_att = forgetting_attention(q_a, k_a, v_a, att_f + b_f_att)

    qk_m = jax.nn.silu(causal_depthwise_conv(ml_qk, conv_ml))
    q_m, k_m = [t.reshape(B, S, MLSTM_HEADS, MLSTM_HEAD_DIM) for t in jnp.split(qk_m, 2, axis=-1)]
    v_m = ml_v.reshape(B, S, MLSTM_HEADS, MLSTM_HEAD_DIM)
    h_m = mlstm_chunkwise(q_m, k_m, v_m, ml_i + b_i_ml, ml_f + b_f_ml)
    h_m = rmsnorm(h_m, norm_ml_head.reshape(MLSTM_HEADS, MLSTM_HEAD_DIM)).reshape(B, S, MLSTM_WIDTH)
    y_ml = jax.nn.sigmoid(ml_o) * h_m.astype(h.dtype)

    u_c, v_c = jnp.split(jax.nn.gelu(gm_uv), 2, axis=-1)
    y_gm = chunked_spatial_gating(u_c, rmsnorm(v_c, norm_gmlp), w_spatial, b_spatial)

    g = jax.nn.sigmoid(gates).reshape(B, S, N_BRANCH, D_MODEL)
    merged = (g[:, :, 0] * (y_att @ w_br_att)
              + g[:, :, 1] * (y_ml @ w_br_ml)
              + g[:, :, 2] * (y_gm @ w_br_gmlp))
    return merged @ w_out


def setup_inputs(seed: int = 0) -> dict:
    key = jax.random.key(seed)
    ks = jax.random.split(key, 25)
    f32 = jnp.float32

    def nrm(k, shape, scale):
        return jax.random.normal(k, shape, f32) * scale

    def gain(k, shape):
        return 1.0 + 0.02 * jax.random.normal(k, shape, f32)

    out_scale = (2 * DEPTH) ** -0.5
    return {
        'x': nrm(ks[0], (BATCH, SEQ, D_MODEL), 1.0),
        'norm_ffn1': gain(ks[1], (DEPTH, D_MODEL)),
        'ffn1_gate': nrm(ks[2], (DEPTH, D_MODEL, D_FF), D_MODEL ** -0.5),
        'ffn1_up': nrm(ks[3], (DEPTH, D_MODEL, D_FF), D_MODEL ** -0.5),
        'ffn1_down': nrm(ks[4], (DEPTH, D_FF, D_MODEL), D_FF ** -0.5 * out_scale),
        'norm_mix': gain(ks[5], (DEPTH, D_MODEL)),
        'w_in': nrm(ks[6], (DEPTH, D_MODEL, D_IN), D_MODEL ** -0.5),
        'b_f_att': jnp.linspace(1.0, 5.0, ATT_HEADS, dtype=f32)[None, :] + nrm(ks[7], (DEPTH, ATT_HEADS), 0.1),
        'b_i_ml': nrm(ks[8], (DEPTH, MLSTM_HEADS), 0.1),
        'b_f_ml': jnp.linspace(3.0, 6.0, MLSTM_HEADS, dtype=f32)[None, :] + nrm(ks[9], (DEPTH, MLSTM_HEADS), 0.1),
        'conv_ml': nrm(ks[10], (DEPTH, CONV_WIDTH, 2 * MLSTM_WIDTH), CONV_WIDTH ** -0.5),
        'norm_ml_head': gain(ks[11], (DEPTH, MLSTM_WIDTH)),
        'norm_gmlp': gain(ks[12], (DEPTH, GMLP_WIDTH)),
        'w_spatial': nrm(ks[13], (DEPTH, GMLP_GROUPS, CHUNK, CHUNK), CHUNK ** -0.5),
        'b_spatial': 1.0 + nrm(ks[14], (DEPTH, GMLP_GROUPS, CHUNK), 0.1),
        'w_br_att': nrm(ks[15], (DEPTH, ATT_WIDTH, D_MODEL), ATT_WIDTH ** -0.5),
        'w_br_ml': nrm(ks[16], (DEPTH, MLSTM_WIDTH, D_MODEL), MLSTM_WIDTH ** -0.5),
        'w_br_gmlp': nrm(ks[17], (DEPTH, GMLP_WIDTH, D_MODEL), GMLP_WIDTH ** -0.5),
        'w_out': nrm(ks[18], (DEPTH, D_MODEL, D_MODEL), D_MODEL ** -0.5 * out_scale),
        'norm_ffn2': gain(ks[19], (DEPTH, D_MODEL)),
        'ffn2_gate': nrm(ks[20], (DEPTH, D_MODEL, D_FF), D_MODEL ** -0.5),
        'ffn2_up': nrm(ks[21], (DEPTH, D_MODEL, D_FF), D_MODEL ** -0.5),
        'ffn2_down': nrm(ks[22], (DEPTH, D_FF, D_MODEL), D_FF ** -0.5 * out_scale),
        'norm_final': gain(ks[23], (D_MODEL,)),
    }


def reference(x, norm_ffn1, ffn1_gate, ffn1_up, ffn1_down, norm_mix, w_in, b_f_att, b_i_ml, b_f_ml,
              conv_ml, norm_ml_head, norm_gmlp, w_spatial, b_spatial, w_br_att, w_br_ml, w_br_gmlp,
              w_out, norm_ffn2, ffn2_gate, ffn2_up, ffn2_down, norm_final):
    for l in range(DEPTH):
        x = x + FFN_RES * swiglu(rmsnorm(x, norm_ffn1[l]), ffn1_gate[l], ffn1_up[l], ffn1_down[l])
        x = x + token_mixer(rmsnorm(x, norm_mix[l]), w_in[l], b_f_att[l], b_i_ml[l], b_f_ml[l],
                            conv_ml[l], norm_ml_head[l], norm_gmlp[l], w_spatial[l], b_spatial[l],
                            w_br_att[l], w_br_ml[l], w_br_gmlp[l], w_out[l])
        x = x + FFN_RES * swiglu(rmsnorm(x, norm_ffn2[l]), ffn2_gate[l], ffn2_up[l], ffn2_down[l])
    return rmsnorm(x, norm_final)
```

```python
import functools

import jax
import jax.numpy as jnp
from jax import lax
from jax.experimental import pallas as pl
from jax.experimental.pallas import tpu as pltpu

D_MODEL = 1024
ATT_HEADS = 8
ATT_HEAD_DIM = 64
ATT_WIDTH = ATT_HEADS * ATT_HEAD_DIM
ML_HEADS = 4
ML_HEAD_DIM = 128
ML_WIDTH = ML_HEADS * ML_HEAD_DIM
CONV_WIDTH = 4
GM_GROUPS = 4
GM_GROUP_DIM = 128
GM_WIDTH = GM_GROUPS * GM_GROUP_DIM
CHUNK = 128
D_FF = 2816
FFN_RES = 0.5
N_BRANCH = 3
EPS = 1e-6

LANES = 128
SUBLANES = 8
VMEM_LIMIT = 56 * 1024 * 1024
NEG_BIG = -1e30

GATE_ATT_F = 0
GATE_ML_I = ATT_HEADS
GATE_ML_F = ATT_HEADS + ML_HEADS
GATE_ROWS = 16

F32 = jnp.float32
BF16 = jnp.bfloat16


def _cparams(sem):
    return pltpu.CompilerParams(dimension_semantics=sem, vmem_limit_bytes=VMEM_LIMIT)


def _resident(shape):
    nd = len(shape)
    return pl.BlockSpec(shape, lambda *_: (0,) * nd, pipeline_mode=pl.Buffered(1))


def _rms(x, g):
    ms = jnp.mean(x * x, axis=-1, keepdims=True)
    return x * lax.rsqrt(ms + EPS) * g


def _log_sigmoid(z):
    return jnp.minimum(z, 0.0) - jnp.log1p(jnp.exp(-jnp.abs(z)))


def _dot(a, b):
    return jnp.dot(a, b, preferred_element_type=F32)


def _dot_nt(a, b):
    return lax.dot_general(a, b, (((1,), (1,)), ((), ())), preferred_element_type=F32)


def _ffn_kernel(x_ref, g_ref, wg_ref, wu_ref, wd_ref, o_ref):
    x = x_ref[...]
    hn = _rms(x, g_ref[...]).astype(BF16)
    gate = _dot(hn, wg_ref[...])
    up = _dot(hn, wu_ref[...])
    act = (gate * jax.nn.sigmoid(gate) * up).astype(BF16)
    o_ref[...] = x + FFN_RES * _dot(act, wd_ref[...])


def _ffn(x2, g, wg, wu, wd, tm):
    n = x2.shape[0]
    return pl.pallas_call(
        _ffn_kernel,
        out_shape=jax.ShapeDtypeStruct((n, D_MODEL), F32),
        grid=(n // tm,),
        in_specs=[
            pl.BlockSpec((tm, D_MODEL), lambda i: (i, 0)),
            _resident((1, D_MODEL)),
            _resident((D_MODEL, D_FF)),
            _resident((D_MODEL, D_FF)),
            _resident((D_FF, D_MODEL)),
        ],
        out_specs=pl.BlockSpec((tm, D_MODEL), lambda i: (i, 0)),
        compiler_params=_cparams(("parallel",)),
        name="ffn",
    )(x2, g, wg, wu, wd)


def _inproj_kernel(x_ref, g_ref, watt_ref, wsm_ref, wqk_ref, wv_ref,
                   q_ref, k_ref, v_ref, sm_ref, mqk_ref, mv_ref):
    hn = _rms(x_ref[...], g_ref[...]).astype(BF16)
    qkv = _dot(hn, watt_ref[...])
    q_ref[...] = (qkv[:, :ATT_WIDTH] * (ATT_HEAD_DIM ** -0.5)).astype(BF16)
    k_ref[...] = qkv[:, ATT_WIDTH:2 * ATT_WIDTH].astype(BF16)
    v_ref[...] = qkv[:, 2 * ATT_WIDTH:].astype(BF16)
    sm_ref[...] = _dot(hn, wsm_ref[...])
    mqk_ref[...] = _dot(hn, wqk_ref[...])
    mv_ref[...] = _dot(hn, wv_ref[...]).astype(BF16)


def _inproj(x2, g, watt, wsm, wqk, wv, tm):
    n = x2.shape[0]
    row = lambda w: pl.BlockSpec((tm, w), lambda i: (i, 0))
    return pl.pallas_call(
        _inproj_kernel,
        out_shape=(
            jax.ShapeDtypeStruct((n, ATT_WIDTH), BF16),
            jax.ShapeDtypeStruct((n, ATT_WIDTH), BF16),
            jax.ShapeDtypeStruct((n, ATT_WIDTH), BF16),
            jax.ShapeDtypeStruct((n, LANES), F32),
            jax.ShapeDtypeStruct((n, 2 * ML_WIDTH), F32),
            jax.ShapeDtypeStruct((n, ML_WIDTH), BF16),
        ),
        grid=(n // tm,),
        in_specs=[
            row(D_MODEL),
            _resident((1, D_MODEL)),
            _resident((D_MODEL, 3 * ATT_WIDTH)),
            _resident((D_MODEL, LANES)),
            _resident((D_MODEL, 2 * ML_WIDTH)),
            _resident((D_MODEL, ML_WIDTH)),
        ],
        out_specs=(row(ATT_WIDTH), row(ATT_WIDTH), row(ATT_WIDTH), row(LANES),
                   row(2 * ML_WIDTH), row(ML_WIDTH)),
        compiler_params=_cparams(("parallel",)),
        name="inproj",
    )(x2, g, watt, wsm, wqk, wv)


def _split3(x):
    hi = x.astype(BF16)
    r1 = x - hi.astype(F32)
    mid = r1.astype(BF16)
    lo = (r1 - mid.astype(F32)).astype(BF16)
    return hi, mid, lo


def _gates_kernel(sm_ref, bias_ref, col_ref, row_ref, carry_ref, *, ts):
    @pl.when(pl.program_id(1) == 0)
    def _():
        carry_ref[...] = jnp.zeros_like(carry_ref)

    z = sm_ref[0] + bias_ref[...]
    lane = lax.broadcasted_iota(jnp.int32, z.shape, 1)
    is_att = lane < GATE_ML_I
    is_mlf = (lane >= GATE_ML_F) & (lane < GATE_ML_F + ML_HEADS)
    vals = jnp.where(is_att | is_mlf, _log_sigmoid(z), z)

    r = lax.broadcasted_iota(jnp.int32, (ts, ts), 0)
    c = lax.broadcasted_iota(jnp.int32, (ts, ts), 1)
    tri = c <= r
    tri_full = jnp.where(tri, 1.0, 0.0).astype(BF16)
    tri_chunk = jnp.where(tri & ((r // CHUNK) == (c // CHUNK)), 1.0, 0.0).astype(BF16)
    hi, mid, lo = _split3(vals)
    cs_full = _dot(tri_full, hi) + _dot(tri_full, mid) + _dot(tri_full, lo) + carry_ref[...]
    cs_chunk = _dot(tri_chunk, hi) + _dot(tri_chunk, mid) + _dot(tri_chunk, lo)
    carry_ref[...] = cs_full[ts - 1:ts, :]

    out = jnp.where(is_att, cs_full, jnp.where(is_mlf, cs_chunk, vals))
    col_ref[0] = out
    row_ref[0] = out.T[:GATE_ROWS, :]


def _gates(small, bias, ts):
    b, s, _ = small.shape
    return pl.pallas_call(
        functools.partial(_gates_kernel, ts=ts),
        out_shape=(jax.ShapeDtypeStruct((b, s, LANES), F32),
                   jax.ShapeDtypeStruct((b, GATE_ROWS, s), F32)),
        grid=(b, s // ts),
        in_specs=[pl.BlockSpec((1, ts, LANES), lambda bi, si: (bi, si, 0)),
                  _resident((1, LANES))],
        out_specs=(pl.BlockSpec((1, ts, LANES), lambda bi, si: (bi, si, 0)),
                   pl.BlockSpec((1, GATE_ROWS, ts), lambda bi, si: (bi, 0, si))),
        scratch_shapes=[pltpu.VMEM((1, LANES), F32)],
        compiler_params=_cparams(("parallel", "arbitrary")),
        name="gates",
    )(small, bias)


def _att_kernel(q_ref, k_ref, v_ref, gcol_ref, grow_ref, o_ref, *, t):
    hp = pl.program_id(1)
    i = pl.program_id(2)
    lane = lax.broadcasted_iota(jnp.int32, (1, LANES), 1)
    in_head = [lane < ATT_HEAD_DIM, lane >= ATT_HEAD_DIM]
    q = q_ref[0]
    gcol = gcol_ref[0]
    zero = jnp.zeros((), BF16)

    qm, ct, base = [], [], []
    for hh in range(2):
        h = hp * 2 + hh
        qm.append(jnp.where(in_head[hh], q, zero))
        c_t = jnp.sum(jnp.where(lane == h, gcol, 0.0), axis=-1, keepdims=True)
        c_0 = c_t[0:1, :]
        base.append(c_0)
        ct.append(c_t - c_0)

    def step(j, carry, masked):
        m0, l0, m1, l1, acc = carry
        ms, ls = [m0, m1], [l0, l1]
        start = pl.multiple_of(j * t, t)
        kj = k_ref[0, pl.ds(start, t), :]
        vj = v_ref[0, pl.ds(start, t), :]
        new_m, new_l, alphas, pvs = [], [], [], []
        for hh in range(2):
            h = hp * 2 + hh
            cs = grow_ref[0, pl.ds(h, 1), pl.ds(start, t)] - base[hh]
            s = _dot_nt(qm[hh], kj) + ct[hh] - cs
            if masked:
                r = lax.broadcasted_iota(jnp.int32, (t, t), 0)
                c = lax.broadcasted_iota(jnp.int32, (t, t), 1)
                s = jnp.where(c <= r, s, NEG_BIG)
            m_new = jnp.maximum(ms[hh], jnp.max(s, axis=-1, keepdims=True))
            alpha = jnp.exp(ms[hh] - m_new)
            p = jnp.exp(s - m_new)
            new_l.append(alpha * ls[hh] + jnp.sum(p, axis=-1, keepdims=True))
            new_m.append(m_new)
            alphas.append(alpha)
            pvs.append(_dot(p.astype(BF16), jnp.where(in_head[hh], vj, zero)))
        acc = acc * jnp.where(in_head[0], alphas[0], alphas[1]) + pvs[0] + pvs[1]
        return new_m[0], new_l[0], new_m[1], new_l[1], acc

    init = (jnp.full((t, 1), NEG_BIG, F32), jnp.zeros((t, 1), F32),
            jnp.full((t, 1), NEG_BIG, F32), jnp.zeros((t, 1), F32),
            jnp.zeros((t, LANES), F32))
    carry = lax.fori_loop(0, i, functools.partial(step, masked=False), init)
    _, l0, _, l1, acc = step(i, carry, masked=True)
    o_ref[0] = (acc / jnp.where(in_head[0], l0, l1)).astype(o_ref.dtype)


def _attention(q, k, v, gcol, grow, t):
    b, s, _ = q.shape
    npair = ATT_HEADS // 2
    return pl.pallas_call(
        functools.partial(_att_kernel, t=t),
        out_shape=jax.ShapeDtypeStruct((b, s, ATT_WIDTH), BF16),
        grid=(b, npair, s // t),
        in_specs=[
            pl.BlockSpec((1, t, LANES), lambda bi, hp, i: (bi, i, hp)),
            pl.BlockSpec((1, s, LANES), lambda bi, hp, i: (bi, 0, hp)),
            pl.BlockSpec((1, s, LANES), lambda bi, hp, i: (bi, 0, hp)),
            pl.BlockSpec((1, t, LANES), lambda bi, hp, i: (bi, i, 0)),
            pl.BlockSpec((1, GATE_ROWS, s), lambda bi, hp, i: (bi, 0, 0)),
        ],
        out_specs=pl.BlockSpec((1, t, LANES), lambda bi, hp, i: (bi, i, hp)),
        compiler_params=_cparams(("parallel", "parallel", "arbitrary")),
        name="fox_attention",
    )(q, k, v, gcol, grow)


def _mlstm_kernel(qk_ref, v_ref, gcol_ref, grow_ref, cw_ref, nw_ref, o_ref,
                  prev_ref, ct_ref, n_ref, m_ref):
    @pl.when(pl.program_id(1) == 0)
    def _():
        prev_ref[...] = jnp.zeros_like(prev_ref)
        ct_ref[...] = jnp.zeros_like(ct_ref)
        n_ref[...] = jnp.zeros_like(n_ref)
        m_ref[...] = jnp.zeros_like(m_ref)

    L = CHUNK
    x = qk_ref[0]
    xcat = jnp.concatenate([prev_ref[...], x], axis=0)
    cw = cw_ref[...]
    conv = x * cw[CONV_WIDTH - 1:CONV_WIDTH, :]
    for j in range(CONV_WIDTH - 1):
        off = SUBLANES - (CONV_WIDTH - 1) + j
        conv = conv + xcat[off:off + L, :] * cw[j:j + 1, :]
    prev_ref[...] = x[L - SUBLANES:, :]
    qk = conv * jax.nn.sigmoid(conv)

    gcol = gcol_ref[0]
    grow = grow_ref[0]
    vall = v_ref[0]
    r = lax.broadcasted_iota(jnp.int32, (L, L), 0)
    c = lax.broadcasted_iota(jnp.int32, (L, L), 1)
    causal = c <= r
    nw = nw_ref[...]

    for h in range(ML_HEADS):
        sl = slice(h * ML_HEAD_DIM, (h + 1) * ML_HEAD_DIM)
        q = qk[:, sl] * (ML_HEAD_DIM ** -0.5)
        k = qk[:, ML_WIDTH + h * ML_HEAD_DIM:ML_WIDTH + (h + 1) * ML_HEAD_DIM]
        v = vall[:, sl]
        li_col = gcol[:, GATE_ML_I + h:GATE_ML_I + h + 1]
        b_col = gcol[:, GATE_ML_F + h:GATE_ML_F + h + 1]
        li_row = grow[GATE_ML_I + h:GATE_ML_I + h + 1, :]
        b_row = grow[GATE_ML_F + h:GATE_ML_F + h + 1, :]
        m = m_ref[h]
        ct = ct_ref[h]
        n = n_ref[h]

        a = b_col + m
        dmat = jnp.where(causal, b_col - b_row + li_row, NEG_BIG)
        m_t = jnp.maximum(a, jnp.max(dmat, axis=-1, keepdims=True))
        w_intra = jnp.exp(dmat - m_t)
        w_inter = jnp.exp(a - m_t)
        qb = q.astype(BF16)
        qkw = _dot_nt(qb, k.astype(BF16)) * w_intra
        num = _dot(qkw.astype(BF16), v) + w_inter * _dot(qb, ct.astype(BF16))
        den = (jnp.sum(qkw, axis=-1, keepdims=True)
               + w_inter * jnp.sum(q * n, axis=-1, keepdims=True))
        hh = num / jnp.maximum(jnp.abs(den), jnp.exp(-m_t))
        o_ref[0, :, sl] = _rms(hh, nw[:, sl])

        b_last = b_row[:, L - 1:L]
        g_row = b_last - b_row + li_row
        m_new = jnp.maximum(b_last + m, jnp.max(g_row, axis=-1, keepdims=True))
        w_s = jnp.exp(b_last - b_col + li_col - m_new)
        decay = jnp.exp(b_last + m - m_new)
        wv = (w_s * v.astype(F32)).astype(BF16)
        ct_ref[h] = decay * ct + _dot(k.T.astype(BF16), wv)
        n_ref[h] = decay * n + jnp.sum(w_s * k, axis=0, keepdims=True)
        m_ref[h] = m_new


def _mlstm(mqk, mv, gcol, grow, conv_w, norm_w):
    b, s, _ = mqk.shape
    nc = s // CHUNK
    return pl.pallas_call(
        _mlstm_kernel,
        out_shape=jax.ShapeDtypeStruct((b, s, ML_WIDTH), F32),
        grid=(b, nc),
        in_specs=[
            pl.BlockSpec((1, CHUNK, 2 * ML_WIDTH), lambda bi, ci: (bi, ci, 0)),
            pl.BlockSpec((1, CHUNK, ML_WIDTH), lambda bi, ci: (bi, ci, 0)),
            pl.BlockSpec((1, CHUNK, LANES), lambda bi, ci: (bi, ci, 0)),
            pl.BlockSpec((1, GATE_ROWS, CHUNK), lambda bi, ci: (bi, 0, ci)),
            _resident((CONV_WIDTH, 2 * ML_WIDTH)),
            _resident((1, ML_WIDTH)),
        ],
        out_specs=pl.BlockSpec((1, CHUNK, ML_WIDTH), lambda bi, ci: (bi, ci, 0)),
        scratch_shapes=[
            pltpu.VMEM((SUBLANES, 2 * ML_WIDTH), F32),
            pltpu.VMEM((ML_HEADS, ML_HEAD_DIM, ML_HEAD_DIM), F32),
            pltpu.VMEM((ML_HEADS, 1, ML_HEAD_DIM), F32),
            pltpu.VMEM((ML_HEADS, 1, 1), F32),
        ],
        compiler_params=_cparams(("parallel", "arbitrary")),
        name="mlstm",
    )(mqk, mv, gcol, grow, conv_w, norm_w)


def _gelu_tanh(x):
    return 0.5 * x * (1.0 + jnp.tanh(0.7978845608028654 * (x + 0.044715 * (x * x * x))))


def _merge_kernel(x_ref, yatt_ref, hm_ref, g_ref, wo_ref, wuv_ref, wgt_ref, ng_ref,
                  wsp_ref, bsp_ref, wba_ref, wbm_ref, wbg_ref, wout_ref, o_ref, *, tm):
    x = x_ref[...]
    hn = _rms(x, g_ref[...]).astype(BF16)

    y_ml = jax.nn.sigmoid(_dot(hn, wo_ref[...])) * hm_ref[...]

    uv = _gelu_tanh(_dot(hn, wuv_ref[...]))
    u = uv[:, :GM_WIDTH]
    vn = _rms(uv[:, GM_WIDTH:], ng_ref[...]).astype(BF16)
    r = lax.broadcasted_iota(jnp.int32, (tm, tm), 0)
    c = lax.broadcasted_iota(jnp.int32, (tm, tm), 1)
    keep = (c <= r) & ((r // CHUNK) == (c // CHUNK))
    mixed = []
    for gi in range(GM_GROUPS):
        w = jnp.where(keep, wsp_ref[gi], 0.0).astype(BF16)
        mixed.append(_dot(w, vn[:, gi * GM_GROUP_DIM:(gi + 1) * GM_GROUP_DIM]))
    y_gm = u * (jnp.concatenate(mixed, axis=-1) + bsp_ref[...])

    gates = jax.nn.sigmoid(_dot(hn, wgt_ref[...]))
    merged = (gates[:, :D_MODEL] * _dot(yatt_ref[...], wba_ref[...])
              + gates[:, D_MODEL:2 * D_MODEL] * _dot(y_ml.astype(BF16), wbm_ref[...])
              + gates[:, 2 * D_MODEL:] * _dot(y_gm.astype(BF16), wbg_ref[...]))
    o_ref[...] = x + _dot(merged.astype(BF16), wout_ref[...])


def _merge(x2, yatt, hm, g, wo, wuv, wgt, ng, wsp, bsp, wba, wbm, wbg, wout, tm):
    n = x2.shape[0]
    row = lambda w: pl.BlockSpec((tm, w), lambda i: (i, 0))
    return pl.pallas_call(
        functools.partial(_merge_kernel, tm=tm),
        out_shape=jax.ShapeDtypeStruct((n, D_MODEL), F32),
        grid=(n // tm,),
        in_specs=[
            row(D_MODEL), row(ATT_WIDTH), row(ML_WIDTH),
            _resident((1, D_MODEL)),
            _resident((D_MODEL, ML_WIDTH)),
            _resident((D_MODEL, 2 * GM_WIDTH)),
            _resident((D_MODEL, N_BRANCH * D_MODEL)),
            _resident((1, GM_WIDTH)),
            _resident((GM_GROUPS, tm, tm)),
            _resident((tm, GM_WIDTH)),
            _resident((ATT_WIDTH, D_MODEL)),
            _resident((ML_WIDTH, D_MODEL)),
            _resident((GM_WIDTH, D_MODEL)),
            _resident((D_MODEL, D_MODEL)),
        ],
        out_specs=row(D_MODEL),
        compiler_params=_cparams(("parallel",)),
        name="merge",
    )(x2, yatt, hm, g, wo, wuv, wgt, ng, wsp, bsp, wba, wbm, wbg, wout)


def _final_kernel(x_ref, g_ref, o_ref):
    o_ref[...] = _rms(x_ref[...], g_ref[...])


def _final_norm(x2, g, tm):
    n = x2.shape[0]
    return pl.pallas_call(
        _final_kernel,
        out_shape=jax.ShapeDtypeStruct((n, D_MODEL), F32),
        grid=(n // tm,),
        in_specs=[pl.BlockSpec((tm, D_MODEL), lambda i: (i, 0)), _resident((1, D_MODEL))],
        out_specs=pl.BlockSpec((tm, D_MODEL), lambda i: (i, 0)),
        compiler_params=_cparams(("parallel",)),
        name="final_norm",
    )(x2, g)


def _tile(n, pref):
    t = min(n, pref)
    assert n % t == 0, (n, t)
    return t


def _prepare_layer_params(p, tm_merge):
    depth = p["w_in"].shape[0]
    o_att = 3 * ATT_WIDTH
    o_mqk = o_att + ATT_HEADS
    o_mv = o_mqk + 2 * ML_WIDTH
    o_mi = o_mv + ML_WIDTH
    o_mf = o_mi + ML_HEADS
    o_mo = o_mf + ML_HEADS
    o_uv = o_mo + ML_WIDTH
    o_gt = o_uv + 2 * GM_WIDTH
    w_in = p["w_in"]
    w_small = jnp.concatenate(
        [w_in[:, :, o_att:o_mqk], w_in[:, :, o_mi:o_mf], w_in[:, :, o_mf:o_mo],
         jnp.zeros((depth, D_MODEL, LANES - GATE_ROWS), F32)], axis=-1)
    b_small = jnp.concatenate(
        [p["b_f_att"], p["b_i_ml"], p["b_f_ml"], jnp.zeros((depth, LANES - GATE_ROWS), F32)],
        axis=-1)[:, None, :]
    reps = tm_merge // CHUNK
    bf = lambda a: a.astype(BF16)
    row = lambda a: a[:, None, :]
    return dict(
        norm_ffn1=row(p["norm_ffn1"]), ffn1_gate=bf(p["ffn1_gate"]), ffn1_up=bf(p["ffn1_up"]),
        ffn1_down=bf(p["ffn1_down"]),
        norm_mix=row(p["norm_mix"]),
        w_att=bf(w_in[:, :, :o_att]), w_small=bf(w_small), b_small=b_small,
        w_mqk=bf(w_in[:, :, o_mqk:o_mv]), w_mv=bf(w_in[:, :, o_mv:o_mi]),
        w_mo=bf(w_in[:, :, o_mo:o_uv]), w_uv=bf(w_in[:, :, o_uv:o_gt]), w_gt=bf(w_in[:, :, o_gt:]),
        conv_ml=p["conv_ml"], norm_ml_head=row(p["norm_ml_head"]), norm_gmlp=row(p["norm_gmlp"]),
        w_spatial=jnp.tile(p["w_spatial"], (1, 1, reps, reps)),
        b_spatial=jnp.tile(jnp.repeat(jnp.swapaxes(p["b_spatial"], 1, 2), GM_GROUP_DIM, axis=2),
                           (1, reps, 1)),
        w_br_att=bf(p["w_br_att"]), w_br_ml=bf(p["w_br_ml"]), w_br_gmlp=bf(p["w_br_gmlp"]),
        w_out=bf(p["w_out"]),
        norm_ffn2=row(p["norm_ffn2"]), ffn2_gate=bf(p["ffn2_gate"]), ffn2_up=bf(p["ffn2_up"]),
        ffn2_down=bf(p["ffn2_down"]),
    )


def _layer(x2, lp, b, s, tiles):
    n = b * s
    x2 = _ffn(x2, lp["norm_ffn1"], lp["ffn1_gate"], lp["ffn1_up"], lp["ffn1_down"], tiles["ffn"])
    q, k, v, small, mqk, mv = _inproj(x2, lp["norm_mix"], lp["w_att"], lp["w_small"],
                                      lp["w_mqk"], lp["w_mv"], tiles["inproj"])
    r3 = lambda a: a.reshape(b, s, a.shape[-1])
    gcol, grow = _gates(r3(small), lp["b_small"], tiles["gates"])
    yatt = _attention(r3(q), r3(k), r3(v), gcol, grow, tiles["att"])
    hm = _mlstm(r3(mqk), r3(mv), gcol, grow, lp["conv_ml"], lp["norm_ml_head"])
    x2 = _merge(x2, yatt.reshape(n, ATT_WIDTH), hm.reshape(n, ML_WIDTH), lp["norm_mix"],
                lp["w_mo"], lp["w_uv"], lp["w_gt"], lp["norm_gmlp"], lp["w_spatial"],
                lp["b_spatial"], lp["w_br_att"], lp["w_br_ml"], lp["w_br_gmlp"], lp["w_out"],
                tiles["merge"])
    return _ffn(x2, lp["norm_ffn2"], lp["ffn2_gate"], lp["ffn2_up"], lp["ffn2_down"], tiles["ffn"])


def _tiles_for(n, s):
    return dict(ffn=_tile(n, 512), inproj=_tile(n, 512), gates=_tile(s, 512), att=_tile(s, 256),
                merge=_tile(s, 256), final=_tile(n, 1024))


def _trunk(x, params, norm_final):
    b, s, _ = x.shape
    n = b * s
    tiles = _tiles_for(n, s)
    stacked = _prepare_layer_params(params, tiles["merge"])

    def body(x2, lp):
        return _layer(x2, lp, b, s, tiles), None

    x2, _ = lax.scan(body, x.reshape(n, D_MODEL), stacked)
    return _final_norm(x2, norm_final[None, :], tiles["final"]).reshape(b, s, D_MODEL)


def kernel(x, norm_ffn1, ffn1_gate, ffn1_up, ffn1_down, norm_mix, w_in, b_f_att, b_i_ml, b_f_ml, conv_ml, norm_ml_head, norm_gmlp, w_spatial, b_spatial, w_br_att, w_br_ml, w_br_gmlp, w_out, norm_ffn2, ffn2_gate, ffn2_up, ffn2_down, norm_final):
    params = dict(norm_ffn1=norm_ffn1, ffn1_gate=ffn1_gate, ffn1_up=ffn1_up, ffn1_down=ffn1_down,
                  norm_mix=norm_mix, w_in=w_in, b_f_att=b_f_att, b_i_ml=b_i_ml, b_f_ml=b_f_ml,
                  conv_ml=conv_ml, norm_ml_head=norm_ml_head, norm_gmlp=norm_gmlp,
                  w_spatial=w_spatial, b_spatial=b_spatial, w_br_att=w_br_att, w_br_ml=w_br_ml,
                  w_br_gmlp=w_br_gmlp, w_out=w_out, norm_ffn2=norm_ffn2, ffn2_gate=ffn2_gate,
                  ffn2_up=ffn2_up, ffn2_down=ffn2_down)
    return _trunk(x, params, norm_final)
```

```python
import functools

import jax
import jax.numpy as jnp
from jax import lax
from jax.experimental import pallas as pl
from jax.experimental.pallas import tpu as pltpu

D_MODEL = 1024
ATT_HEADS = 8
ATT_HEAD_DIM = 64
ATT_WIDTH = ATT_HEADS * ATT_HEAD_DIM
ML_HEADS = 4
ML_HEAD_DIM = 128
ML_WIDTH = ML_HEADS * ML_HEAD_DIM
CONV_WIDTH = 4
GM_GROUPS = 4
GM_GROUP_DIM = 128
GM_WIDTH = GM_GROUPS * GM_GROUP_DIM
CHUNK = 128
D_FF = 2816
FFN_RES = 0.5
N_BRANCH = 3
EPS = 1e-6

LANES = 128
SUBLANES = 8
ATT_SLOTS = ATT_HEADS * LANES
ATT_ROW_BLOCK = 32
VMEM_LIMIT = 56 * 1024 * 1024
NEG_BIG = -1e30

GATE_ATT_F = 0
GATE_ML_I = ATT_HEADS
GATE_ML_F = ATT_HEADS + ML_HEADS
GATE_ROWS = 16

F32 = jnp.float32
BF16 = jnp.bfloat16


def _cparams(sem):
    return pltpu.CompilerParams(dimension_semantics=sem, vmem_limit_bytes=VMEM_LIMIT)


def _resident(shape):
    nd = len(shape)
    return pl.BlockSpec(shape, lambda *_: (0,) * nd, pipeline_mode=pl.Buffered(1))


def _rms(x, g):
    ms = jnp.mean(x * x, axis=-1, keepdims=True)
    return x * lax.rsqrt(ms + EPS) * g


def _log_sigmoid(z):
    return jnp.minimum(z, 0.0) - jnp.log1p(jnp.exp(-jnp.abs(z)))


def _dot(a, b):
    return jnp.dot(a, b, preferred_element_type=F32)


def _dot_nt(a, b):
    return lax.dot_general(a, b, (((1,), (1,)), ((), ())), preferred_element_type=F32)


def _ffn_kernel(x_ref, g_ref, wg_ref, wu_ref, wd_ref, o_ref):
    x = x_ref[...]
    hn = _rms(x, g_ref[...]).astype(BF16)
    gate = _dot(hn, wg_ref[...])
    up = _dot(hn, wu_ref[...])
    act = (gate * jax.nn.sigmoid(gate) * up).astype(BF16)
    o_ref[...] = x + FFN_RES * _dot(act, wd_ref[...])


def _ffn(x2, g, wg, wu, wd, tm):
    n = x2.shape[0]
    return pl.pallas_call(
        _ffn_kernel,
        out_shape=jax.ShapeDtypeStruct((n, D_MODEL), F32),
        grid=(n // tm,),
        in_specs=[
            pl.BlockSpec((tm, D_MODEL), lambda i: (i, 0)),
            _resident((1, D_MODEL)),
            _resident((D_MODEL, D_FF)),
            _resident((D_MODEL, D_FF)),
            _resident((D_FF, D_MODEL)),
        ],
        out_specs=pl.BlockSpec((tm, D_MODEL), lambda i: (i, 0)),
        compiler_params=_cparams(("parallel",)),
        name="ffn",
    )(x2, g, wg, wu, wd)


def _inproj_kernel(x_ref, g_ref, wqt_ref, wk_ref, wvt_ref, wsm_ref, wqk_ref, wv_ref,
                   qt_ref, k_ref, vt_ref, sm_ref, mqk_ref, mv_ref, *, tm, tq, tk):
    hn = _rms(x_ref[0], g_ref[...])
    hb = hn.astype(BF16)
    hnt = hn.T.astype(BF16)
    slot_row = lax.broadcasted_iota(jnp.int32, (ATT_SLOTS, tm), 0) % LANES
    qt = _dot(wqt_ref[...], hnt)
    qt = jnp.where((slot_row >= ATT_HEAD_DIM) & (slot_row < ATT_HEAD_DIM + 3), 1.0, qt)
    vt = _dot(wvt_ref[...], hnt)
    vt = jnp.where(slot_row == ATT_HEAD_DIM, 1.0, vt)
    qt = qt.astype(BF16)
    vt = vt.astype(BF16)
    for c in range(tm // tq):
        qt_ref[0, :, c] = qt[:, c * tq:(c + 1) * tq].reshape(ATT_HEADS, LANES, tq)
    for c in range(tm // tk):
        vt_ref[0, :, c] = vt[:, c * tk:(c + 1) * tk].reshape(ATT_HEADS, LANES, tk)
    k_ref[0] = _dot(hb, wk_ref[...]).astype(BF16)
    sm_ref[0] = _dot(hb, wsm_ref[...])
    mqk_ref[0] = _dot(hb, wqk_ref[...])
    mv_ref[0] = _dot(hb, wv_ref[...]).astype(BF16)


def _inproj(x3, g, wqt, wk, wvt, wsm, wqk, wv, tm, tq, tk):
    b, s, _ = x3.shape
    row = lambda w: pl.BlockSpec((1, tm, w), lambda bi, si: (bi, si, 0))
    tr = lambda t: pl.BlockSpec((1, ATT_HEADS, tm // t, LANES, t),
                                lambda bi, si: (bi, 0, si, 0, 0))
    tr_shape = lambda t: jax.ShapeDtypeStruct((b, ATT_HEADS, s // t, LANES, t), BF16)
    return pl.pallas_call(
        functools.partial(_inproj_kernel, tm=tm, tq=tq, tk=tk),
        out_shape=(
            tr_shape(tq),
            jax.ShapeDtypeStruct((b, s, ATT_SLOTS), BF16),
            tr_shape(tk),
            jax.ShapeDtypeStruct((b, s, LANES), F32),
            jax.ShapeDtypeStruct((b, s, 2 * ML_WIDTH), F32),
            jax.ShapeDtypeStruct((b, s, ML_WIDTH), BF16),
        ),
        grid=(b, s // tm),
        in_specs=[
            row(D_MODEL),
            _resident((1, D_MODEL)),
            _resident((ATT_SLOTS, D_MODEL)),
            _resident((D_MODEL, ATT_SLOTS)),
            _resident((ATT_SLOTS, D_MODEL)),
            _resident((D_MODEL, LANES)),
            _resident((D_MODEL, 2 * ML_WIDTH)),
            _resident((D_MODEL, ML_WIDTH)),
        ],
        out_specs=(tr(tq), row(ATT_SLOTS), tr(tk), row(LANES), row(2 * ML_WIDTH), row(ML_WIDTH)),
        compiler_params=_cparams(("parallel", "parallel")),
        name="inproj",
    )(x3, g, wqt, wk, wvt, wsm, wqk, wv)


def _split3(x):
    hi = x.astype(BF16)
    r1 = x - hi.astype(F32)
    mid = r1.astype(BF16)
    lo = (r1 - mid.astype(F32)).astype(BF16)
    return hi, mid, lo


def _gates_kernel(sm_ref, bias_ref, k_ref, col_ref, row_ref, ka_ref, carry_ref, *, ts):
    @pl.when(pl.program_id(1) == 0)
    def _():
        carry_ref[...] = jnp.zeros_like(carry_ref)

    z = sm_ref[0] + bias_ref[...]
    lane = lax.broadcasted_iota(jnp.int32, z.shape, 1)
    is_att = lane < GATE_ML_I
    is_mlf = (lane >= GATE_ML_F) & (lane < GATE_ML_F + ML_HEADS)
    vals = jnp.where(is_att | is_mlf, _log_sigmoid(z), z)

    r = lax.broadcasted_iota(jnp.int32, (ts, ts), 0)
    c = lax.broadcasted_iota(jnp.int32, (ts, ts), 1)
    tri = c <= r
    tri_full = jnp.where(tri, 1.0, 0.0).astype(BF16)
    tri_chunk = jnp.where(tri & ((r // CHUNK) == (c // CHUNK)), 1.0, 0.0).astype(BF16)
    hi, mid, lo = _split3(vals)
    cs_full = _dot(tri_full, hi) + _dot(tri_full, mid) + _dot(tri_full, lo) + carry_ref[...]
    cs_chunk = _dot(tri_chunk, hi) + _dot(tri_chunk, mid) + _dot(tri_chunk, lo)
    carry_ref[...] = cs_full[ts - 1:ts, :]

    out = jnp.where(is_att, cs_full, jnp.where(is_mlf, cs_chunk, vals))
    col_ref[0] = out
    row_ref[0] = out.T[:GATE_ROWS, :]

    src = lax.broadcasted_iota(jnp.int32, (LANES, ATT_SLOTS), 0)
    dst = lax.broadcasted_iota(jnp.int32, (LANES, ATT_SLOTS), 1)
    decay_cols = jnp.zeros((ts, ATT_SLOTS), F32)
    for j, piece in enumerate(_split3(-cs_full)):
        place = jnp.where((src < ATT_HEADS) & (dst == src * LANES + ATT_HEAD_DIM + j), 1.0, 0.0)
        decay_cols = decay_cols + _dot(piece, place.astype(BF16))
    ka_ref[0] = (k_ref[0].astype(F32) + decay_cols).astype(BF16)


def _gates(small, bias, k, ts):
    b, s, _ = small.shape
    row = lambda w: pl.BlockSpec((1, ts, w), lambda bi, si: (bi, si, 0))
    return pl.pallas_call(
        functools.partial(_gates_kernel, ts=ts),
        out_shape=(jax.ShapeDtypeStruct((b, s, LANES), F32),
                   jax.ShapeDtypeStruct((b, GATE_ROWS, s), F32),
                   jax.ShapeDtypeStruct((b, s, ATT_SLOTS), BF16)),
        grid=(b, s // ts),
        in_specs=[row(LANES), _resident((1, LANES)), row(ATT_SLOTS)],
        out_specs=(row(LANES),
                   pl.BlockSpec((1, GATE_ROWS, ts), lambda bi, si: (bi, 0, si)),
                   row(ATT_SLOTS)),
        scratch_shapes=[pltpu.VMEM((1, LANES), F32)],
        compiler_params=_cparams(("parallel", "arbitrary")),
        name="gates",
    )(small, bias, k)


def _att_kernel(qt_ref, k_ref, vt_ref, o_ref, s0_ref, s1_ref, p0_ref, p1_ref, *, tq, tk, hb, rb):
    i = pl.program_id(2)
    s_refs = (s0_ref, s1_ref)
    p_refs = (p0_ref, p1_ref)

    def scores(hh, chunk):
        kj = k_ref[0, pl.ds(pl.multiple_of(chunk * tk, tk), tk), hh * LANES:(hh + 1) * LANES]
        return _dot(kj, qt_ref[0, hh, 0])

    def weigh(hh, chunk, par, alpha, acc):
        return acc * alpha + _dot(vt_ref[0, hh, chunk], p_refs[par][hh])

    def visit(par, carry, pv_chunk, next_chunk, next_mask=None):
        oth = 1 - par
        out = []
        for hh in range(hb):
            m, cmax, alpha_prev, acc = carry[hh]
            acc = weigh(hh, pv_chunk, oth, alpha_prev, acc)
            m_new = jnp.maximum(m, cmax)
            alpha = jnp.exp(m - m_new)
            for r in range(0, tk, rb):
                p_refs[par][hh, r:r + rb, :] = jnp.exp(
                    s_refs[par][hh, r:r + rb, :] - m_new).astype(BF16)
            s_next = scores(hh, next_chunk)
            if next_mask is not None:
                s_next = jnp.where(next_mask, s_next, NEG_BIG)
            s_refs[oth][hh] = s_next
            out.append((m_new, jnp.max(s_next, axis=0, keepdims=True), alpha, acc))
        return tuple(out)

    key = lax.broadcasted_iota(jnp.int32, (tk, tq), 0)
    qry = lax.broadcasted_iota(jnp.int32, (tk, tq), 1)
    init = []
    for hh in range(hb):
        s_first = jnp.where(key <= qry, scores(hh, 2 * i), NEG_BIG)
        s0_ref[hh] = s_first
        p1_ref[hh] = jnp.zeros((tk, tq), BF16)
        init.append((jnp.full((1, tq), NEG_BIG, F32), jnp.max(s_first, axis=0, keepdims=True),
                     jnp.ones((1, tq), F32), jnp.zeros((LANES, tq), F32)))
    carry = visit(0, tuple(init), 0, 2 * i + 1, key + tk <= qry)
    carry = visit(1, carry, 2 * i, 0)

    def pair(w, carry):
        carry = visit(0, carry, jnp.where(w == 1, 2 * i + 1, 2 * w - 3), 2 * w - 1)
        return visit(1, carry, 2 * w - 2, 2 * w)

    carry = lax.fori_loop(1, i + 1, pair, carry)
    last_chunk = jnp.where(i == 0, 1, 2 * i - 1)
    lane = lax.broadcasted_iota(jnp.int32, (1, LANES), 1)
    for pr in range(hb // 2):
        halves = []
        for hh in (2 * pr, 2 * pr + 1):
            _, _, alpha, acc = carry[hh]
            acc_t = weigh(hh, last_chunk, 1, alpha, acc).T
            halves.append(acc_t / acc_t[:, ATT_HEAD_DIM:ATT_HEAD_DIM + 1])
        both = jnp.where(lane < ATT_HEAD_DIM, halves[0],
                         pltpu.roll(halves[1], ATT_HEAD_DIM, axis=1))
        o_ref[0, :, pr * LANES:(pr + 1) * LANES] = both.astype(o_ref.dtype)


def _attention(qt, k, vt, tq, tk, hb):
    b, s, _ = k.shape
    assert tq == 2 * tk
    return pl.pallas_call(
        functools.partial(_att_kernel, tq=tq, tk=tk, hb=hb, rb=ATT_ROW_BLOCK),
        out_shape=jax.ShapeDtypeStruct((b, s, ATT_WIDTH), BF16),
        grid=(b, ATT_HEADS // hb, s // tq),
        in_specs=[
            pl.BlockSpec((1, hb, 1, LANES, tq), lambda bi, hg, i: (bi, hg, i, 0, 0)),
            pl.BlockSpec((1, s, hb * LANES), lambda bi, hg, i: (bi, 0, hg),
                         pipeline_mode=pl.Buffered(1)),
            pl.BlockSpec((1, hb, s // tk, LANES, tk), lambda bi, hg, i: (bi, hg, 0, 0, 0),
                         pipeline_mode=pl.Buffered(1)),
        ],
        out_specs=pl.BlockSpec((1, tq, hb * ATT_HEAD_DIM), lambda bi, hg, i: (bi, i, hg)),
        scratch_shapes=[pltpu.VMEM((hb, tk, tq), F32), pltpu.VMEM((hb, tk, tq), F32),
                        pltpu.VMEM((hb, tk, tq), BF16), pltpu.VMEM((hb, tk, tq), BF16)],
        compiler_params=_cparams(("parallel", "parallel", "arbitrary")),
        name="fox_attention",
    )(qt, k, vt)


def _mlstm_kernel(qk_ref, v_ref, gcol_ref, grow_ref, cw_ref, nw_ref, o_ref,
                  prev_ref, ct_ref, n_ref, m_ref):
    @pl.when(pl.program_id(1) == 0)
    def _():
        prev_ref[...] = jnp.zeros_like(prev_ref)
        ct_ref[...] = jnp.zeros_like(ct_ref)
        n_ref[...] = jnp.zeros_like(n_ref)
        m_ref[...] = jnp.zeros_like(m_ref)

    L = CHUNK
    x = qk_ref[0]
    xcat = jnp.concatenate([prev_ref[...], x], axis=0)
    cw = cw_ref[...]
    conv = x * cw[CONV_WIDTH - 1:CONV_WIDTH, :]
    for j in range(CONV_WIDTH - 1):
        off = SUBLANES - (CONV_WIDTH - 1) + j
        conv = conv + xcat[off:off + L, :] * cw[j:j + 1, :]
    prev_ref[...] = x[L - SUBLANES:, :]
    qk = conv * jax.nn.sigmoid(conv)

    gcol = gcol_ref[0]
    grow = grow_ref[0]
    vall = v_ref[0]
    r = lax.broadcasted_iota(jnp.int32, (L, L), 0)
    c = lax.broadcasted_iota(jnp.int32, (L, L), 1)
    causal = c <= r
    nw = nw_ref[...]

    for h in range(ML_HEADS):
        sl = slice(h * ML_HEAD_DIM, (h + 1) * ML_HEAD_DIM)
        q = qk[:, sl] * (ML_HEAD_DIM ** -0.5)
        k = qk[:, ML_WIDTH + h * ML_HEAD_DIM:ML_WIDTH + (h + 1) * ML_HEAD_DIM]
        v = vall[:, sl]
        li_col = gcol[:, GATE_ML_I + h:GATE_ML_I + h + 1]
        b_col = gcol[:, GATE_ML_F + h:GATE_ML_F + h + 1]
        li_row = grow[GATE_ML_I + h:GATE_ML_I + h + 1, :]
        b_row = grow[GATE_ML_F + h:GATE_ML_F + h + 1, :]
        m = m_ref[h]
        ct = ct_ref[h]
        n = n_ref[h]

        a = b_col + m
        dmat = jnp.where(causal, b_col - b_row + li_row, NEG_BIG)
        m_t = jnp.maximum(a, jnp.max(dmat, axis=-1, keepdims=True))
        w_intra = jnp.exp(dmat - m_t)
        w_inter = jnp.exp(a - m_t)
        qb = q.astype(BF16)
        qkw = _dot_nt(qb, k.astype(BF16)) * w_intra
        num = _dot(qkw.astype(BF16), v) + w_inter * _dot(qb, ct.astype(BF16))
        den = (jnp.sum(qkw, axis=-1, keepdims=True)
               + w_inter * jnp.sum(q * n, axis=-1, keepdims=True))
        hh = num / jnp.maximum(jnp.abs(den), jnp.exp(-m_t))
        o_ref[0, :, sl] = _rms(hh, nw[:, sl])

        b_last = b_row[:, L - 1:L]
        g_row = b_last - b_row + li_row
        m_new = jnp.maximum(b_last + m, jnp.max(g_row, axis=-1, keepdims=True))
        w_s = jnp.exp(b_last - b_col + li_col - m_new)
        decay = jnp.exp(b_last + m - m_new)
        wv = (w_s * v.astype(F32)).astype(BF16)
        ct_ref[h] = decay * ct + _dot(k.T.astype(BF16), wv)
        n_ref[h] = decay * n + jnp.sum(w_s * k, axis=0, keepdims=True)
        m_ref[h] = m_new


def _mlstm(mqk, mv, gcol, grow, conv_w, norm_w):
    b, s, _ = mqk.shape
    nc = s // CHUNK
    return pl.pallas_call(
        _mlstm_kernel,
        out_shape=jax.ShapeDtypeStruct((b, s, ML_WIDTH), F32),
        grid=(b, nc),
        in_specs=[
            pl.BlockSpec((1, CHUNK, 2 * ML_WIDTH), lambda bi, ci: (bi, ci, 0)),
            pl.BlockSpec((1, CHUNK, ML_WIDTH), lambda bi, ci: (bi, ci, 0)),
            pl.BlockSpec((1, CHUNK, LANES), lambda bi, ci: (bi, ci, 0)),
            pl.BlockSpec((1, GATE_ROWS, CHUNK), lambda bi, ci: (bi, 0, ci)),
            _resident((CONV_WIDTH, 2 * ML_WIDTH)),
            _resident((1, ML_WIDTH)),
        ],
        out_specs=pl.BlockSpec((1, CHUNK, ML_WIDTH), lambda bi, ci: (bi, ci, 0)),
        scratch_shapes=[
            pltpu.VMEM((SUBLANES, 2 * ML_WIDTH), F32),
            pltpu.VMEM((ML_HEADS, ML_HEAD_DIM, ML_HEAD_DIM), F32),
            pltpu.VMEM((ML_HEADS, 1, ML_HEAD_DIM), F32),
            pltpu.VMEM((ML_HEADS, 1, 1), F32),
        ],
        compiler_params=_cparams(("parallel", "arbitrary")),
        name="mlstm",
    )(mqk, mv, gcol, grow, conv_w, norm_w)


def _gelu_tanh(x):
    return 0.5 * x * (1.0 + jnp.tanh(0.7978845608028654 * (x + 0.044715 * (x * x * x))))


def _merge_kernel(x_ref, yatt_ref, hm_ref, g_ref, wo_ref, wuv_ref, wgt_ref, ng_ref,
                  wsp_ref, bsp_ref, wba_ref, wbm_ref, wbg_ref, wout_ref, o_ref, *, tm):
    x = x_ref[...]
    hn = _rms(x, g_ref[...]).astype(BF16)

    y_ml = jax.nn.sigmoid(_dot(hn, wo_ref[...])) * hm_ref[...]

    uv = _gelu_tanh(_dot(hn, wuv_ref[...]))
    u = uv[:, :GM_WIDTH]
    vn = _rms(uv[:, GM_WIDTH:], ng_ref[...]).astype(BF16)
    r = lax.broadcasted_iota(jnp.int32, (tm, tm), 0)
    c = lax.broadcasted_iota(jnp.int32, (tm, tm), 1)
    keep = (c <= r) & ((r // CHUNK) == (c // CHUNK))
    mixed = []
    for gi in range(GM_GROUPS):
        w = jnp.where(keep, wsp_ref[gi], 0.0).astype(BF16)
        mixed.append(_dot(w, vn[:, gi * GM_GROUP_DIM:(gi + 1) * GM_GROUP_DIM]))
    y_gm = u * (jnp.concatenate(mixed, axis=-1) + bsp_ref[...])

    gates = jax.nn.sigmoid(_dot(hn, wgt_ref[...]))
    merged = (gates[:, :D_MODEL] * _dot(yatt_ref[...], wba_ref[...])
              + gates[:, D_MODEL:2 * D_MODEL] * _dot(y_ml.astype(BF16), wbm_ref[...])
              + gates[:, 2 * D_MODEL:] * _dot(y_gm.astype(BF16), wbg_ref[...]))
    o_ref[...] = x + _dot(merged.astype(BF16), wout_ref[...])


def _merge(x2, yatt, hm, g, wo, wuv, wgt, ng, wsp, bsp, wba, wbm, wbg, wout, tm):
    n = x2.shape[0]
    row = lambda w: pl.BlockSpec((tm, w), lambda i: (i, 0))
    return pl.pallas_call(
        functools.partial(_merge_kernel, tm=tm),
        out_shape=jax.ShapeDtypeStruct((n, D_MODEL), F32),
        grid=(n // tm,),
        in_specs=[
            row(D_MODEL), row(ATT_WIDTH), row(ML_WIDTH),
            _resident((1, D_MODEL)),
            _resident((D_MODEL, ML_WIDTH)),
            _resident((D_MODEL, 2 * GM_WIDTH)),
            _resident((D_MODEL, N_BRANCH * D_MODEL)),
            _resident((1, GM_WIDTH)),
            _resident((GM_GROUPS, tm, tm)),
            _resident((tm, GM_WIDTH)),
            _resident((ATT_WIDTH, D_MODEL)),
            _resident((ML_WIDTH, D_MODEL)),
            _resident((GM_WIDTH, D_MODEL)),
            _resident((D_MODEL, D_MODEL)),
        ],
        out_specs=row(D_MODEL),
        compiler_params=_cparams(("parallel",)),
        name="merge",
    )(x2, yatt, hm, g, wo, wuv, wgt, ng, wsp, bsp, wba, wbm, wbg, wout)


def _final_kernel(x_ref, g_ref, o_ref):
    o_ref[...] = _rms(x_ref[...], g_ref[...])


def _final_norm(x2, g, tm):
    n = x2.shape[0]
    return pl.pallas_call(
        _final_kernel,
        out_shape=jax.ShapeDtypeStruct((n, D_MODEL), F32),
        grid=(n // tm,),
        in_specs=[pl.BlockSpec((tm, D_MODEL), lambda i: (i, 0)), _resident((1, D_MODEL))],
        out_specs=pl.BlockSpec((tm, D_MODEL), lambda i: (i, 0)),
        compiler_params=_cparams(("parallel",)),
        name="final_norm",
    )(x2, g)


def _tile(n, pref):
    t = min(n, pref)
    assert n % t == 0, (n, t)
    return t


def _prepare_layer_params(p, tm_merge):
    depth = p["w_in"].shape[0]
    o_att = 3 * ATT_WIDTH
    o_mqk = o_att + ATT_HEADS
    o_mv = o_mqk + 2 * ML_WIDTH
    o_mi = o_mv + ML_WIDTH
    o_mf = o_mi + ML_HEADS
    o_mo = o_mf + ML_HEADS
    o_uv = o_mo + ML_WIDTH
    o_gt = o_uv + 2 * GM_WIDTH
    w_in = p["w_in"]
    w_small = jnp.concatenate(
        [w_in[:, :, o_att:o_mqk], w_in[:, :, o_mi:o_mf], w_in[:, :, o_mf:o_mo],
         jnp.zeros((depth, D_MODEL, LANES - GATE_ROWS), F32)], axis=-1)
    b_small = jnp.concatenate(
        [p["b_f_att"], p["b_i_ml"], p["b_f_ml"], jnp.zeros((depth, LANES - GATE_ROWS), F32)],
        axis=-1)[:, None, :]
    reps = tm_merge // CHUNK
    bf = lambda a: a.astype(BF16)
    row = lambda a: a[:, None, :]

    def slots(w):
        w = w.reshape(depth, D_MODEL, ATT_HEADS, ATT_HEAD_DIM)
        w = jnp.pad(w, ((0, 0), (0, 0), (0, 0), (0, LANES - ATT_HEAD_DIM)))
        return w.reshape(depth, D_MODEL, ATT_SLOTS)

    w_q = slots(w_in[:, :, :ATT_WIDTH]) * (ATT_HEAD_DIM ** -0.5)
    w_k = slots(w_in[:, :, ATT_WIDTH:2 * ATT_WIDTH])
    w_v = slots(w_in[:, :, 2 * ATT_WIDTH:o_att])
    return dict(
        norm_ffn1=row(p["norm_ffn1"]), ffn1_gate=bf(p["ffn1_gate"]), ffn1_up=bf(p["ffn1_up"]),
        ffn1_down=bf(p["ffn1_down"]),
        norm_mix=row(p["norm_mix"]),
        w_qt=bf(jnp.swapaxes(w_q, 1, 2)), w_k=bf(w_k), w_vt=bf(jnp.swapaxes(w_v, 1, 2)),
        w_small=bf(w_small), b_small=b_small,
        w_mqk=bf(w_in[:, :, o_mqk:o_mv]), w_mv=bf(w_in[:, :, o_mv:o_mi]),
        w_mo=bf(w_in[:, :, o_mo:o_uv]), w_uv=bf(w_in[:, :, o_uv:o_gt]), w_gt=bf(w_in[:, :, o_gt:]),
        conv_ml=p["conv_ml"], norm_ml_head=row(p["norm_ml_head"]), norm_gmlp=row(p["norm_gmlp"]),
        w_spatial=jnp.tile(p["w_spatial"], (1, 1, reps, reps)),
        b_spatial=jnp.tile(jnp.repeat(jnp.swapaxes(p["b_spatial"], 1, 2), GM_GROUP_DIM, axis=2),
                           (1, reps, 1)),
        w_br_att=bf(p["w_br_att"]), w_br_ml=bf(p["w_br_ml"]), w_br_gmlp=bf(p["w_br_gmlp"]),
        w_out=bf(p["w_out"]),
        norm_ffn2=row(p["norm_ffn2"]), ffn2_gate=bf(p["ffn2_gate"]), ffn2_up=bf(p["ffn2_up"]),
        ffn2_down=bf(p["ffn2_down"]),
    )


def _layer(x2, lp, b, s, tiles):
    n = b * s
    x2 = _ffn(x2, lp["norm_ffn1"], lp["ffn1_gate"], lp["ffn1_up"], lp["ffn1_down"], tiles["ffn"])
    qt, k, vt, small, mqk, mv = _inproj(
        x2.reshape(b, s, D_MODEL), lp["norm_mix"], lp["w_qt"], lp["w_k"], lp["w_vt"],
        lp["w_small"], lp["w_mqk"], lp["w_mv"], tiles["inproj"], tiles["att_q"], tiles["att_k"])
    gcol, grow, k = _gates(small, lp["b_small"], k, tiles["gates"])
    yatt = _attention(qt, k, vt, tiles["att_q"], tiles["att_k"], tiles["att_heads"])
    hm = _mlstm(mqk, mv, gcol, grow, lp["conv_ml"], lp["norm_ml_head"])
    x2 = _merge(x2, yatt.reshape(n, ATT_WIDTH), hm.reshape(n, ML_WIDTH), lp["norm_mix"],
                lp["w_mo"], lp["w_uv"], lp["w_gt"], lp["norm_gmlp"], lp["w_spatial"],
                lp["b_spatial"], lp["w_br_att"], lp["w_br_ml"], lp["w_br_gmlp"], lp["w_out"],
                tiles["merge"])
    return _ffn(x2, lp["norm_ffn2"], lp["ffn2_gate"], lp["ffn2_up"], lp["ffn2_down"], tiles["ffn"])


def _tiles_for(n, s):
    att_q = _tile(s, 512)
    return dict(ffn=_tile(n, 512), inproj=_tile(s, 512), gates=_tile(s, 512), att_q=att_q,
                att_k=att_q // 2, att_heads=2, merge=_tile(s, 256), final=_tile(n, 1024))


def _trunk(x, params, norm_final):
    b, s, _ = x.shape
    n = b * s
    tiles = _tiles_for(n, s)
    stacked = _prepare_layer_params(params, tiles["merge"])

    def body(x2, lp):
        return _layer(x2, lp, b, s, tiles), None

    x2, _ = lax.scan(body, x.reshape(n, D_MODEL), stacked)
    return _final_norm(x2, norm_final[None, :], tiles["final"]).reshape(b, s, D_MODEL)


def kernel(x, norm_ffn1, ffn1_gate, ffn1_up, ffn1_down, norm_mix, w_in, b_f_att, b_i_ml, b_f_ml, conv_ml, norm_ml_head, norm_gmlp, w_spatial, b_spatial, w_br_att, w_br_ml, w_br_gmlp, w_out, norm_ffn2, ffn2_gate, ffn2_up, ffn2_down, norm_final):
    params = dict(norm_ffn1=norm_ffn1, ffn1_gate=ffn1_gate, ffn1_up=ffn1_up, ffn1_down=ffn1_down,
                  norm_mix=norm_mix, w_in=w_in, b_f_att=b_f_att, b_i_ml=b_i_ml, b_f_ml=b_f_ml,
                  conv_ml=conv_ml, norm_ml_head=norm_ml_head, norm_gmlp=norm_gmlp,
                  w_spatial=w_spatial, b_spatial=b_spatial, w_br_att=w_br_att, w_br_ml=w_br_ml,
                  w_br_gmlp=w_br_gmlp, w_out=w_out, norm_ffn2=norm_ffn2, ffn2_gate=ffn2_gate,
                  ffn2_up=ffn2_up, ffn2_down=ffn2_down)
    return _trunk(x, params, norm_final)
```

```python
import functools

import jax
import jax.numpy as jnp
from jax import lax
from jax.experimental import pallas as pl
from jax.experimental.pallas import tpu as pltpu

D_MODEL = 1024
ATT_HEADS = 8
ATT_HEAD_DIM = 64
ATT_WIDTH = ATT_HEADS * ATT_HEAD_DIM
ML_HEADS = 4
ML_HEAD_DIM = 128
ML_WIDTH = ML_HEADS * ML_HEAD_DIM
CONV_WIDTH = 4
GM_GROUPS = 4
GM_GROUP_DIM = 128
GM_WIDTH = GM_GROUPS * GM_GROUP_DIM
CHUNK = 128
D_FF = 2816
FFN_RES = 0.5
N_BRANCH = 3
EPS = 1e-6

LANES = 128
SUBLANES = 8
ATT_SLOTS = ATT_HEADS * LANES
ATT_ROW_BLOCK = 32
ATT_V_ROWS = 80
LOG2E = 1.4426950408889634
VMEM_LIMIT = 56 * 1024 * 1024
NEG_BIG = -1e30

GATE_ATT_F = 0
GATE_ML_I = ATT_HEADS
GATE_ML_F = ATT_HEADS + ML_HEADS
GATE_ROWS = 16

F32 = jnp.float32
BF16 = jnp.bfloat16


def _cparams(sem):
    return pltpu.CompilerParams(dimension_semantics=sem, vmem_limit_bytes=VMEM_LIMIT)


def _resident(shape):
    nd = len(shape)
    return pl.BlockSpec(shape, lambda *_: (0,) * nd, pipeline_mode=pl.Buffered(1))


def _rms(x, g):
    ms = jnp.mean(x * x, axis=-1, keepdims=True)
    return x * lax.rsqrt(ms + EPS) * g


def _log_sigmoid(z):
    return jnp.minimum(z, 0.0) - jnp.log1p(jnp.exp(-jnp.abs(z)))


def _dot(a, b):
    return jnp.dot(a, b, preferred_element_type=F32)


def _dot_nt(a, b):
    return lax.dot_general(a, b, (((1,), (1,)), ((), ())), preferred_element_type=F32)


def _ffn_kernel(x_ref, g_ref, wg_ref, wu_ref, wd_ref, o_ref):
    x = x_ref[...]
    hn = _rms(x, g_ref[...]).astype(BF16)
    gate = _dot(hn, wg_ref[...])
    up = _dot(hn, wu_ref[...])
    act = (gate * jax.nn.sigmoid(gate) * up).astype(BF16)
    o_ref[...] = x + FFN_RES * _dot(act, wd_ref[...])


def _ffn(x2, g, wg, wu, wd, tm):
    n = x2.shape[0]
    return pl.pallas_call(
        _ffn_kernel,
        out_shape=jax.ShapeDtypeStruct((n, D_MODEL), F32),
        grid=(n // tm,),
        in_specs=[
            pl.BlockSpec((tm, D_MODEL), lambda i: (i, 0)),
            _resident((1, D_MODEL)),
            _resident((D_MODEL, D_FF)),
            _resident((D_MODEL, D_FF)),
            _resident((D_FF, D_MODEL)),
        ],
        out_specs=pl.BlockSpec((tm, D_MODEL), lambda i: (i, 0)),
        compiler_params=_cparams(("parallel",)),
        name="ffn",
    )(x2, g, wg, wu, wd)


def _inproj_kernel(x_ref, g_ref, wqt_ref, wk_ref, wvt_ref, wsm_ref, wqk_ref, wv_ref,
                   qt_ref, k_ref, vt_ref, sm_ref, mqk_ref, mv_ref, *, tm, tq, tk):
    hn = _rms(x_ref[0], g_ref[...])
    hb = hn.astype(BF16)
    hnt = hn.T.astype(BF16)
    q_row = lax.broadcasted_iota(jnp.int32, (ATT_SLOTS, tm), 0) % LANES
    qt = _dot(wqt_ref[...], hnt)
    qt = jnp.where((q_row >= ATT_HEAD_DIM) & (q_row < ATT_HEAD_DIM + 3), 1.0, qt).astype(BF16)
    v_row = lax.broadcasted_iota(jnp.int32, (ATT_HEADS * ATT_V_ROWS, tm), 0) % ATT_V_ROWS
    vt = _dot(wvt_ref[...], hnt)
    vt = jnp.where(v_row == ATT_HEAD_DIM, 1.0, vt).astype(BF16)
    for c in range(tm // tq):
        qt_ref[0, :, c] = qt[:, c * tq:(c + 1) * tq].reshape(ATT_HEADS, LANES, tq)
    for c in range(tm // tk):
        vt_ref[0, :, c] = vt[:, c * tk:(c + 1) * tk].reshape(ATT_HEADS, ATT_V_ROWS, tk)
    k_ref[0] = _dot(hb, wk_ref[...]).astype(BF16)
    sm_ref[0] = _dot(hb, wsm_ref[...])
    mqk_ref[0] = _dot(hb, wqk_ref[...])
    mv_ref[0] = _dot(hb, wv_ref[...]).astype(BF16)


def _inproj(x3, g, wqt, wk, wvt, wsm, wqk, wv, tm, tq, tk):
    b, s, _ = x3.shape
    row = lambda w: pl.BlockSpec((1, tm, w), lambda bi, si: (bi, si, 0))
    tr = lambda rows, t: pl.BlockSpec((1, ATT_HEADS, tm // t, rows, t),
                                      lambda bi, si: (bi, 0, si, 0, 0))
    tr_shape = lambda rows, t: jax.ShapeDtypeStruct((b, ATT_HEADS, s // t, rows, t), BF16)
    return pl.pallas_call(
        functools.partial(_inproj_kernel, tm=tm, tq=tq, tk=tk),
        out_shape=(
            tr_shape(LANES, tq),
            jax.ShapeDtypeStruct((b, s, ATT_SLOTS), BF16),
            tr_shape(ATT_V_ROWS, tk),
            jax.ShapeDtypeStruct((b, s, LANES), F32),
            jax.ShapeDtypeStruct((b, s, 2 * ML_WIDTH), F32),
            jax.ShapeDtypeStruct((b, s, ML_WIDTH), BF16),
        ),
        grid=(b, s // tm),
        in_specs=[
            row(D_MODEL),
            _resident((1, D_MODEL)),
            _resident((ATT_SLOTS, D_MODEL)),
            _resident((D_MODEL, ATT_SLOTS)),
            _resident((ATT_HEADS * ATT_V_ROWS, D_MODEL)),
            _resident((D_MODEL, LANES)),
            _resident((D_MODEL, 2 * ML_WIDTH)),
            _resident((D_MODEL, ML_WIDTH)),
        ],
        out_specs=(tr(LANES, tq), row(ATT_SLOTS), tr(ATT_V_ROWS, tk), row(LANES),
                   row(2 * ML_WIDTH), row(ML_WIDTH)),
        compiler_params=_cparams(("parallel", "parallel")),
        name="inproj",
    )(x3, g, wqt, wk, wvt, wsm, wqk, wv)


def _split3(x):
    hi = x.astype(BF16)
    r1 = x - hi.astype(F32)
    mid = r1.astype(BF16)
    lo = (r1 - mid.astype(F32)).astype(BF16)
    return hi, mid, lo


def _gates_kernel(sm_ref, bias_ref, k_ref, col_ref, row_ref, ka_ref, carry_ref, *, ts):
    @pl.when(pl.program_id(1) == 0)
    def _():
        carry_ref[...] = jnp.zeros_like(carry_ref)

    z = sm_ref[0] + bias_ref[...]
    lane = lax.broadcasted_iota(jnp.int32, z.shape, 1)
    is_att = lane < GATE_ML_I
    is_mlf = (lane >= GATE_ML_F) & (lane < GATE_ML_F + ML_HEADS)
    vals = jnp.where(is_att | is_mlf, _log_sigmoid(z), z)

    r = lax.broadcasted_iota(jnp.int32, (ts, ts), 0)
    c = lax.broadcasted_iota(jnp.int32, (ts, ts), 1)
    tri = c <= r
    tri_full = jnp.where(tri, 1.0, 0.0).astype(BF16)
    tri_chunk = jnp.where(tri & ((r // CHUNK) == (c // CHUNK)), 1.0, 0.0).astype(BF16)
    hi, mid, lo = _split3(vals)
    cs_full = _dot(tri_full, hi) + _dot(tri_full, mid) + _dot(tri_full, lo) + carry_ref[...]
    cs_chunk = _dot(tri_chunk, hi) + _dot(tri_chunk, mid) + _dot(tri_chunk, lo)
    carry_ref[...] = cs_full[ts - 1:ts, :]

    out = jnp.where(is_att, cs_full, jnp.where(is_mlf, cs_chunk, vals))
    col_ref[0] = out
    row_ref[0] = out.T[:GATE_ROWS, :]

    src = lax.broadcasted_iota(jnp.int32, (LANES, ATT_SLOTS), 0)
    dst = lax.broadcasted_iota(jnp.int32, (LANES, ATT_SLOTS), 1)
    decay_cols = jnp.zeros((ts, ATT_SLOTS), F32)
    for j, piece in enumerate(_split3(cs_full * -LOG2E)):
        place = jnp.where((src < ATT_HEADS) & (dst == src * LANES + ATT_HEAD_DIM + j), 1.0, 0.0)
        decay_cols = decay_cols + _dot(piece, place.astype(BF16))
    ka_ref[0] = (k_ref[0].astype(F32) + decay_cols).astype(BF16)


def _gates(small, bias, k, ts):
    b, s, _ = small.shape
    row = lambda w: pl.BlockSpec((1, ts, w), lambda bi, si: (bi, si, 0))
    return pl.pallas_call(
        functools.partial(_gates_kernel, ts=ts),
        out_shape=(jax.ShapeDtypeStruct((b, s, LANES), F32),
                   jax.ShapeDtypeStruct((b, GATE_ROWS, s), F32),
                   jax.ShapeDtypeStruct((b, s, ATT_SLOTS), BF16)),
        grid=(b, s // ts),
        in_specs=[row(LANES), _resident((1, LANES)), row(ATT_SLOTS)],
        out_specs=(row(LANES),
                   pl.BlockSpec((1, GATE_ROWS, ts), lambda bi, si: (bi, 0, si)),
                   row(ATT_SLOTS)),
        scratch_shapes=[pltpu.VMEM((1, LANES), F32)],
        compiler_params=_cparams(("parallel", "arbitrary")),
        name="gates",
    )(small, bias, k)


def _att_kernel(qt_ref, k_ref, vt_ref, o_ref, s0_ref, s1_ref, p0_ref, p1_ref, *, tq, tk, hb, rb):
    i = pl.program_id(2)
    nd = tq // tk
    s_refs = (s0_ref, s1_ref)
    p_refs = (p0_ref, p1_ref)

    def scores(hh, chunk):
        kj = k_ref[0, pl.ds(pl.multiple_of(chunk * tk, tk), tk), hh * LANES:(hh + 1) * LANES]
        return _dot(kj, qt_ref[0, hh, 0])

    def weigh(hh, chunk, par, alpha, acc):
        return acc * alpha + _dot(vt_ref[0, hh, chunk], p_refs[par][hh])

    def visit(par, carry, pv_chunk, next_chunk, next_mask=None):
        oth = 1 - par
        out = []
        for hh in range(hb):
            m, cmax, alpha_prev, acc = carry[hh]
            acc = weigh(hh, pv_chunk, oth, alpha_prev, acc)
            m_new = jnp.maximum(m, cmax)
            alpha = jnp.exp2(m - m_new)
            for r in range(0, tk, rb):
                p_refs[par][hh, r:r + rb, :] = jnp.exp2(
                    s_refs[par][hh, r:r + rb, :] - m_new).astype(BF16)
            s_next = scores(hh, next_chunk)
            if next_mask is not None:
                s_next = jnp.where(next_mask, s_next, NEG_BIG)
            s_refs[oth][hh] = s_next
            out.append((m_new, jnp.max(s_next, axis=0, keepdims=True), alpha, acc))
        return tuple(out)

    key = lax.broadcasted_iota(jnp.int32, (tk, tq), 0)
    qry = lax.broadcasted_iota(jnp.int32, (tk, tq), 1)
    init = []
    for hh in range(hb):
        s_first = jnp.where(key <= qry, scores(hh, nd * i), NEG_BIG)
        s0_ref[hh] = s_first
        p1_ref[hh] = jnp.zeros((tk, tq), BF16)
        init.append((jnp.full((1, tq), NEG_BIG, F32), jnp.max(s_first, axis=0, keepdims=True),
                     jnp.ones((1, tq), F32), jnp.zeros((ATT_V_ROWS, tq), F32)))
    carry = tuple(init)
    for d in range(nd):
        pv_chunk = nd * i + d - 1 if d else 0
        if d + 1 < nd:
            carry = visit(d % 2, carry, pv_chunk, nd * i + d + 1, key + (d + 1) * tk <= qry)
        else:
            carry = visit(d % 2, carry, pv_chunk, 0)

    def pair(w, carry):
        carry = visit(0, carry, jnp.where(w == 0, nd * i + nd - 1, 2 * w - 1), 2 * w + 1)
        return visit(1, carry, 2 * w, 2 * w + 2)

    carry = lax.fori_loop(0, (nd // 2) * i, pair, carry)
    last_chunk = jnp.where(i == 0, nd - 1, nd * i - 1)
    lane = lax.broadcasted_iota(jnp.int32, (1, LANES), 1)
    pad_rows = jnp.zeros((LANES - ATT_V_ROWS, tq), F32)
    for pr in range(hb // 2):
        halves = []
        for hh in (2 * pr, 2 * pr + 1):
            _, _, alpha, acc = carry[hh]
            acc = weigh(hh, last_chunk, 1, alpha, acc)
            acc_t = jnp.concatenate([acc, pad_rows], axis=0).T
            halves.append(acc_t / acc_t[:, ATT_HEAD_DIM:ATT_HEAD_DIM + 1])
        both = jnp.where(lane < ATT_HEAD_DIM, halves[0],
                         pltpu.roll(halves[1], ATT_HEAD_DIM, axis=1))
        o_ref[0, :, pr * LANES:(pr + 1) * LANES] = both.astype(o_ref.dtype)


def _attention(qt, k, vt, tq, tk, hb):
    b, s, _ = k.shape
    assert tq % (2 * tk) == 0
    return pl.pallas_call(
        functools.partial(_att_kernel, tq=tq, tk=tk, hb=hb, rb=ATT_ROW_BLOCK),
        out_shape=jax.ShapeDtypeStruct((b, s, ATT_WIDTH), BF16),
        grid=(b, ATT_HEADS // hb, s // tq),
        in_specs=[
            pl.BlockSpec((1, hb, 1, LANES, tq), lambda bi, hg, i: (bi, hg, i, 0, 0)),
            pl.BlockSpec((1, s, hb * LANES), lambda bi, hg, i: (bi, 0, hg),
                         pipeline_mode=pl.Buffered(1)),
            pl.BlockSpec((1, hb, s // tk, ATT_V_ROWS, tk), lambda bi, hg, i: (bi, hg, 0, 0, 0),
                         pipeline_mode=pl.Buffered(1)),
        ],
        out_specs=pl.BlockSpec((1, tq, hb * ATT_HEAD_DIM), lambda bi, hg, i: (bi, i, hg)),
        scratch_shapes=[pltpu.VMEM((hb, tk, tq), F32), pltpu.VMEM((hb, tk, tq), F32),
                        pltpu.VMEM((hb, tk, tq), BF16), pltpu.VMEM((hb, tk, tq), BF16)],
        compiler_params=_cparams(("parallel", "parallel", "arbitrary")),
        name="fox_attention",
    )(qt, k, vt)


def _mlstm_kernel(qk_ref, v_ref, gcol_ref, grow_ref, cw_ref, nw_ref, o_ref,
                  prev_ref, ct_ref, n_ref, m_ref):
    @pl.when(pl.program_id(1) == 0)
    def _():
        prev_ref[...] = jnp.zeros_like(prev_ref)
        ct_ref[...] = jnp.zeros_like(ct_ref)
        n_ref[...] = jnp.zeros_like(n_ref)
        m_ref[...] = jnp.zeros_like(m_ref)

    L = CHUNK
    x = qk_ref[0]
    xcat = jnp.concatenate([prev_ref[...], x], axis=0)
    cw = cw_ref[...]
    conv = x * cw[CONV_WIDTH - 1:CONV_WIDTH, :]
    for j in range(CONV_WIDTH - 1):
        off = SUBLANES - (CONV_WIDTH - 1) + j
        conv = conv + xcat[off:off + L, :] * cw[j:j + 1, :]
    prev_ref[...] = x[L - SUBLANES:, :]
    qk = conv * jax.nn.sigmoid(conv)

    gcol = gcol_ref[0]
    grow = grow_ref[0]
    vall = v_ref[0]
    r = lax.broadcasted_iota(jnp.int32, (L, L), 0)
    c = lax.broadcasted_iota(jnp.int32, (L, L), 1)
    causal = c <= r
    nw = nw_ref[...]

    for h in range(ML_HEADS):
        sl = slice(h * ML_HEAD_DIM, (h + 1) * ML_HEAD_DIM)
        q = qk[:, sl] * (ML_HEAD_DIM ** -0.5)
        k = qk[:, ML_WIDTH + h * ML_HEAD_DIM:ML_WIDTH + (h + 1) * ML_HEAD_DIM]
        v = vall[:, sl]
        li_col = gcol[:, GATE_ML_I + h:GATE_ML_I + h + 1]
        b_col = gcol[:, GATE_ML_F + h:GATE_ML_F + h + 1]
        li_row = grow[GATE_ML_I + h:GATE_ML_I + h + 1, :]
        b_row = grow[GATE_ML_F + h:GATE_ML_F + h + 1, :]
        m = m_ref[h]
        ct = ct_ref[h]
        n = n_ref[h]

        a = b_col + m
        dmat = jnp.where(causal, b_col - b_row + li_row, NEG_BIG)
        m_t = jnp.maximum(a, jnp.max(dmat, axis=-1, keepdims=True))
        w_intra = jnp.exp(dmat - m_t)
        w_inter = jnp.exp(a - m_t)
        qb = q.astype(BF16)
        qkw = _dot_nt(qb, k.astype(BF16)) * w_intra
        num = _dot(qkw.astype(BF16), v) + w_inter * _dot(qb, ct.astype(BF16))
        den = (jnp.sum(qkw, axis=-1, keepdims=True)
               + w_inter * jnp.sum(q * n, axis=-1, keepdims=True))
        hh = num / jnp.maximum(jnp.abs(den), jnp.exp(-m_t))
        o_ref[0, :, sl] = _rms(hh, nw[:, sl])

        b_last = b_row[:, L - 1:L]
        g_row = b_last - b_row + li_row
        m_new = jnp.maximum(b_last + m, jnp.max(g_row, axis=-1, keepdims=True))
        w_s = jnp.exp(b_last - b_col + li_col - m_new)
        decay = jnp.exp(b_last + m - m_new)
        wv = (w_s * v.astype(F32)).astype(BF16)
        ct_ref[h] = decay * ct + _dot(k.T.astype(BF16), wv)
        n_ref[h] = decay * n + jnp.sum(w_s * k, axis=0, keepdims=True)
        m_ref[h] = m_new


def _mlstm(mqk, mv, gcol, grow, conv_w, norm_w):
    b, s, _ = mqk.shape
    nc = s // CHUNK
    return pl.pallas_call(
        _mlstm_kernel,
        out_shape=jax.ShapeDtypeStruct((b, s, ML_WIDTH), F32),
        grid=(b, nc),
        in_specs=[
            pl.BlockSpec((1, CHUNK, 2 * ML_WIDTH), lambda bi, ci: (bi, ci, 0)),
            pl.BlockSpec((1, CHUNK, ML_WIDTH), lambda bi, ci: (bi, ci, 0)),
            pl.BlockSpec((1, CHUNK, LANES), lambda bi, ci: (bi, ci, 0)),
            pl.BlockSpec((1, GATE_ROWS, CHUNK), lambda bi, ci: (bi, 0, ci)),
            _resident((CONV_WIDTH, 2 * ML_WIDTH)),
            _resident((1, ML_WIDTH)),
        ],
        out_specs=pl.BlockSpec((1, CHUNK, ML_WIDTH), lambda bi, ci: (bi, ci, 0)),
        scratch_shapes=[
            pltpu.VMEM((SUBLANES, 2 * ML_WIDTH), F32),
            pltpu.VMEM((ML_HEADS, ML_HEAD_DIM, ML_HEAD_DIM), F32),
            pltpu.VMEM((ML_HEADS, 1, ML_HEAD_DIM), F32),
            pltpu.VMEM((ML_HEADS, 1, 1), F32),
        ],
        compiler_params=_cparams(("parallel", "arbitrary")),
        name="mlstm",
    )(mqk, mv, gcol, grow, conv_w, norm_w)


def _gelu_tanh(x):
    return 0.5 * x * (1.0 + jnp.tanh(0.7978845608028654 * (x + 0.044715 * (x * x * x))))


def _merge_kernel(x_ref, yatt_ref, hm_ref, g_ref, wo_ref, wuv_ref, wgt_ref, ng_ref,
                  wsp_ref, bsp_ref, wba_ref, wbm_ref, wbg_ref, wout_ref, o_ref, *, tm):
    x = x_ref[...]
    hn = _rms(x, g_ref[...]).astype(BF16)

    y_ml = jax.nn.sigmoid(_dot(hn, wo_ref[...])) * hm_ref[...]

    uv = _gelu_tanh(_dot(hn, wuv_ref[...]))
    u = uv[:, :GM_WIDTH]
    vn = _rms(uv[:, GM_WIDTH:], ng_ref[...]).astype(BF16)
    r = lax.broadcasted_iota(jnp.int32, (tm, tm), 0)
    c = lax.broadcasted_iota(jnp.int32, (tm, tm), 1)
    keep = (c <= r) & ((r // CHUNK) == (c // CHUNK))
    mixed = []
    for gi in range(GM_GROUPS):
        w = jnp.where(keep, wsp_ref[gi], 0.0).astype(BF16)
        mixed.append(_dot(w, vn[:, gi * GM_GROUP_DIM:(gi + 1) * GM_GROUP_DIM]))
    y_gm = u * (jnp.concatenate(mixed, axis=-1) + bsp_ref[...])

    gates = jax.nn.sigmoid(_dot(hn, wgt_ref[...]))
    merged = (gates[:, :D_MODEL] * _dot(yatt_ref[...], wba_ref[...])
              + gates[:, D_MODEL:2 * D_MODEL] * _dot(y_ml.astype(BF16), wbm_ref[...])
              + gates[:, 2 * D_MODEL:] * _dot(y_gm.astype(BF16), wbg_ref[...]))
    o_ref[...] = x + _dot(merged.astype(BF16), wout_ref[...])


def _merge(x2, yatt, hm, g, wo, wuv, wgt, ng, wsp, bsp, wba, wbm, wbg, wout, tm):
    n = x2.shape[0]
    row = lambda w: pl.BlockSpec((tm, w), lambda i: (i, 0))
    return pl.pallas_call(
        functools.partial(_merge_kernel, tm=tm),
        out_shape=jax.ShapeDtypeStruct((n, D_MODEL), F32),
        grid=(n // tm,),
        in_specs=[
            row(D_MODEL), row(ATT_WIDTH), row(ML_WIDTH),
            _resident((1, D_MODEL)),
            _resident((D_MODEL, ML_WIDTH)),
            _resident((D_MODEL, 2 * GM_WIDTH)),
            _resident((D_MODEL, N_BRANCH * D_MODEL)),
            _resident((1, GM_WIDTH)),
            _resident((GM_GROUPS, tm, tm)),
            _resident((tm, GM_WIDTH)),
            _resident((ATT_WIDTH, D_MODEL)),
            _resident((ML_WIDTH, D_MODEL)),
            _resident((GM_WIDTH, D_MODEL)),
            _resident((D_MODEL, D_MODEL)),
        ],
        out_specs=row(D_MODEL),
        compiler_params=_cparams(("parallel",)),
        name="merge",
    )(x2, yatt, hm, g, wo, wuv, wgt, ng, wsp, bsp, wba, wbm, wbg, wout)


def _final_kernel(x_ref, g_ref, o_ref):
    o_ref[...] = _rms(x_ref[...], g_ref[...])


def _final_norm(x2, g, tm):
    n = x2.shape[0]
    return pl.pallas_call(
        _final_kernel,
        out_shape=jax.ShapeDtypeStruct((n, D_MODEL), F32),
        grid=(n // tm,),
        in_specs=[pl.BlockSpec((tm, D_MODEL), lambda i: (i, 0)), _resident((1, D_MODEL))],
        out_specs=pl.BlockSpec((tm, D_MODEL), lambda i: (i, 0)),
        compiler_params=_cparams(("parallel",)),
        name="final_norm",
    )(x2, g)


def _tile(n, pref):
    t = min(n, pref)
    assert n % t == 0, (n, t)
    return t


def _prepare_layer_params(p, tm_merge):
    depth = p["w_in"].shape[0]
    o_att = 3 * ATT_WIDTH
    o_mqk = o_att + ATT_HEADS
    o_mv = o_mqk + 2 * ML_WIDTH
    o_mi = o_mv + ML_WIDTH
    o_mf = o_mi + ML_HEADS
    o_mo = o_mf + ML_HEADS
    o_uv = o_mo + ML_WIDTH
    o_gt = o_uv + 2 * GM_WIDTH
    w_in = p["w_in"]
    w_small = jnp.concatenate(
        [w_in[:, :, o_att:o_mqk], w_in[:, :, o_mi:o_mf], w_in[:, :, o_mf:o_mo],
         jnp.zeros((depth, D_MODEL, LANES - GATE_ROWS), F32)], axis=-1)
    b_small = jnp.concatenate(
        [p["b_f_att"], p["b_i_ml"], p["b_f_ml"], jnp.zeros((depth, LANES - GATE_ROWS), F32)],
        axis=-1)[:, None, :]
    reps = tm_merge // CHUNK
    bf = lambda a: a.astype(BF16)
    row = lambda a: a[:, None, :]

    def slots(w, width):
        w = w.reshape(depth, D_MODEL, ATT_HEADS, ATT_HEAD_DIM)
        w = jnp.pad(w, ((0, 0), (0, 0), (0, 0), (0, width - ATT_HEAD_DIM)))
        return w.reshape(depth, D_MODEL, ATT_HEADS * width)

    w_q = slots(w_in[:, :, :ATT_WIDTH], LANES) * (ATT_HEAD_DIM ** -0.5 * LOG2E)
    w_k = slots(w_in[:, :, ATT_WIDTH:2 * ATT_WIDTH], LANES)
    w_v = slots(w_in[:, :, 2 * ATT_WIDTH:o_att], ATT_V_ROWS)
    return dict(
        norm_ffn1=row(p["norm_ffn1"]), ffn1_gate=bf(p["ffn1_gate"]), ffn1_up=bf(p["ffn1_up"]),
        ffn1_down=bf(p["ffn1_down"]),
        norm_mix=row(p["norm_mix"]),
        w_qt=bf(jnp.swapaxes(w_q, 1, 2)), w_k=bf(w_k), w_vt=bf(jnp.swapaxes(w_v, 1, 2)),
        w_small=bf(w_small), b_small=b_small,
        w_mqk=bf(w_in[:, :, o_mqk:o_mv]), w_mv=bf(w_in[:, :, o_mv:o_mi]),
        w_mo=bf(w_in[:, :, o_mo:o_uv]), w_uv=bf(w_in[:, :, o_uv:o_gt]), w_gt=bf(w_in[:, :, o_gt:]),
        conv_ml=p["conv_ml"], norm_ml_head=row(p["norm_ml_head"]), norm_gmlp=row(p["norm_gmlp"]),
        w_spatial=jnp.tile(p["w_spatial"], (1, 1, reps, reps)),
        b_spatial=jnp.tile(jnp.repeat(jnp.swapaxes(p["b_spatial"], 1, 2), GM_GROUP_DIM, axis=2),
                           (1, reps, 1)),
        w_br_att=bf(p["w_br_att"]), w_br_ml=bf(p["w_br_ml"]), w_br_gmlp=bf(p["w_br_gmlp"]),
        w_out=bf(p["w_out"]),
        norm_ffn2=row(p["norm_ffn2"]), ffn2_gate=bf(p["ffn2_gate"]), ffn2_up=bf(p["ffn2_up"]),
        ffn2_down=bf(p["ffn2_down"]),
    )


def _layer(x2, lp, b, s, tiles):
    n = b * s
    x2 = _ffn(x2, lp["norm_ffn1"], lp["ffn1_gate"], lp["ffn1_up"], lp["ffn1_down"], tiles["ffn"])
    qt, k, vt, small, mqk, mv = _inproj(
        x2.reshape(b, s, D_MODEL), lp["norm_mix"], lp["w_qt"], lp["w_k"], lp["w_vt"],
        lp["w_small"], lp["w_mqk"], lp["w_mv"], tiles["inproj"], tiles["att_q"], tiles["att_k"])
    gcol, grow, k = _gates(small, lp["b_small"], k, tiles["gates"])
    yatt = _attention(qt, k, vt, tiles["att_q"], tiles["att_k"], tiles["att_heads"])
    hm = _mlstm(mqk, mv, gcol, grow, lp["conv_ml"], lp["norm_ml_head"])
    x2 = _merge(x2, yatt.reshape(n, ATT_WIDTH), hm.reshape(n, ML_WIDTH), lp["norm_mix"],
                lp["w_mo"], lp["w_uv"], lp["w_gt"], lp["norm_gmlp"], lp["w_spatial"],
                lp["b_spatial"], lp["w_br_att"], lp["w_br_ml"], lp["w_br_gmlp"], lp["w_out"],
                tiles["merge"])
    return _ffn(x2, lp["norm_ffn2"], lp["ffn2_gate"], lp["ffn2_up"], lp["ffn2_down"], tiles["ffn"])


def _tiles_for(n, s):
    att_q = _tile(s, 512)
    return dict(ffn=_tile(n, 512), inproj=att_q, gates=_tile(s, 512), att_q=att_q,
                att_k=min(att_q // 2, 256), att_heads=4, merge=_tile(s, 256), final=_tile(n, 1024))


def _trunk(x, params, norm_final):
    b, s, _ = x.shape
    n = b * s
    tiles = _tiles_for(n, s)
    stacked = _prepare_layer_params(params, tiles["merge"])

    def body(x2, lp):
        return _layer(x2, lp, b, s, tiles), None

    x2, _ = lax.scan(body, x.reshape(n, D_MODEL), stacked)
    return _final_norm(x2, norm_final[None, :], tiles["final"]).reshape(b, s, D_MODEL)


def kernel(x, norm_ffn1, ffn1_gate, ffn1_up, ffn1_down, norm_mix, w_in, b_f_att, b_i_ml, b_f_ml, conv_ml, norm_ml_head, norm_gmlp, w_spatial, b_spatial, w_br_att, w_br_ml, w_br_gmlp, w_out, norm_ffn2, ffn2_gate, ffn2_up, ffn2_down, norm_final):
    params = dict(norm_ffn1=norm_ffn1, ffn1_gate=ffn1_gate, ffn1_up=ffn1_up, ffn1_down=ffn1_down,
                  norm_mix=norm_mix, w_in=w_in, b_f_att=b_f_att, b_i_ml=b_i_ml, b_f_ml=b_f_ml,
                  conv_ml=conv_ml, norm_ml_head=norm_ml_head, norm_gmlp=norm_gmlp,
                  w_spatial=w_spatial, b_spatial=b_spatial, w_br_att=w_br_att, w_br_ml=w_br_ml,
                  w_br_gmlp=w_br_gmlp, w_out=w_out, norm_ffn2=norm_ffn2, ffn2_gate=ffn2_gate,
                  ffn2_up=ffn2_up, ffn2_down=ffn2_down)
    return _trunk(x, params, norm_final)
```

```python
import functools

import jax
import jax.numpy as jnp
from jax import lax
from jax.experimental import pallas as pl
from jax.experimental.pallas import tpu as pltpu

D_MODEL = 1024
ATT_HEADS = 8
ATT_HEAD_DIM = 64
ATT_WIDTH = ATT_HEADS * ATT_HEAD_DIM
ML_HEADS = 4
ML_HEAD_DIM = 128
ML_WIDTH = ML_HEADS * ML_HEAD_DIM
CONV_WIDTH = 4
GM_GROUPS = 4
GM_GROUP_DIM = 128
GM_WIDTH = GM_GROUPS * GM_GROUP_DIM
CHUNK = 128
D_FF = 2816
FFN_RES = 0.5
N_BRANCH = 3
EPS = 1e-6

LANES = 128
SUBLANES = 8
ATT_SLOTS = ATT_HEADS * LANES
ATT_ROW_BLOCK = 32
ATT_V_ROWS = 80
LOG2E = 1.4426950408889634
ATT_SKIP_EXPONENT = 160.0
VMEM_LIMIT = 56 * 1024 * 1024
NEG_BIG = -1e30

GATE_ATT_F = 0
GATE_ML_I = ATT_HEADS
GATE_ML_F = ATT_HEADS + ML_HEADS
GATE_ROWS = 16

F32 = jnp.float32
BF16 = jnp.bfloat16


def _cparams(sem):
    return pltpu.CompilerParams(dimension_semantics=sem, vmem_limit_bytes=VMEM_LIMIT)


def _resident(shape):
    nd = len(shape)
    return pl.BlockSpec(shape, lambda *_: (0,) * nd, pipeline_mode=pl.Buffered(1))


def _rms(x, g):
    ms = jnp.mean(x * x, axis=-1, keepdims=True)
    return x * lax.rsqrt(ms + EPS) * g


def _log_sigmoid(z):
    return jnp.minimum(z, 0.0) - jnp.log1p(jnp.exp(-jnp.abs(z)))


def _dot(a, b):
    return jnp.dot(a, b, preferred_element_type=F32)


def _dot_nt(a, b):
    return lax.dot_general(a, b, (((1,), (1,)), ((), ())), preferred_element_type=F32)


def _ffn_kernel(x_ref, g_ref, wg_ref, wu_ref, wd_ref, o_ref):
    x = x_ref[...]
    hn = _rms(x, g_ref[...]).astype(BF16)
    gate = _dot(hn, wg_ref[...])
    up = _dot(hn, wu_ref[...])
    act = (gate * jax.nn.sigmoid(gate) * up).astype(BF16)
    o_ref[...] = x + FFN_RES * _dot(act, wd_ref[...])


def _ffn(x2, g, wg, wu, wd, tm):
    n = x2.shape[0]
    return pl.pallas_call(
        _ffn_kernel,
        out_shape=jax.ShapeDtypeStruct((n, D_MODEL), F32),
        grid=(n // tm,),
        in_specs=[
            pl.BlockSpec((tm, D_MODEL), lambda i: (i, 0)),
            _resident((1, D_MODEL)),
            _resident((D_MODEL, D_FF)),
            _resident((D_MODEL, D_FF)),
            _resident((D_FF, D_MODEL)),
        ],
        out_specs=pl.BlockSpec((tm, D_MODEL), lambda i: (i, 0)),
        compiler_params=_cparams(("parallel",)),
        name="ffn",
    )(x2, g, wg, wu, wd)


def _inproj_kernel(x_ref, g_ref, wqt_ref, wk_ref, wvt_ref, wsm_ref, wqk_ref, wv_ref,
                   qt_ref, k_ref, vt_ref, sm_ref, mqk_ref, mv_ref, *, tm, tq, tk):
    hn = _rms(x_ref[0], g_ref[...])
    hb = hn.astype(BF16)
    hnt = hn.T.astype(BF16)
    q_row = lax.broadcasted_iota(jnp.int32, (ATT_SLOTS, tm), 0) % LANES
    qt = _dot(wqt_ref[...], hnt)
    qt = jnp.where((q_row >= ATT_HEAD_DIM) & (q_row < ATT_HEAD_DIM + 3), 1.0, qt).astype(BF16)
    v_row = lax.broadcasted_iota(jnp.int32, (ATT_HEADS * ATT_V_ROWS, tm), 0) % ATT_V_ROWS
    vt = _dot(wvt_ref[...], hnt)
    vt = jnp.where(v_row == ATT_HEAD_DIM, 1.0, vt).astype(BF16)
    for c in range(tm // tq):
        qt_ref[0, :, c] = qt[:, c * tq:(c + 1) * tq].reshape(ATT_HEADS, LANES, tq)
    for c in range(tm // tk):
        vt_ref[0, :, c] = vt[:, c * tk:(c + 1) * tk].reshape(ATT_HEADS, ATT_V_ROWS, tk)
    k_ref[0] = _dot(hb, wk_ref[...]).astype(BF16)
    sm_ref[0] = _dot(hb, wsm_ref[...])
    mqk_ref[0] = _dot(hb, wqk_ref[...])
    mv_ref[0] = _dot(hb, wv_ref[...]).astype(BF16)


def _inproj(x3, g, wqt, wk, wvt, wsm, wqk, wv, tm, tq, tk):
    b, s, _ = x3.shape
    row = lambda w: pl.BlockSpec((1, tm, w), lambda bi, si: (bi, si, 0))
    tr = lambda rows, t: pl.BlockSpec((1, ATT_HEADS, tm // t, rows, t),
                                      lambda bi, si: (bi, 0, si, 0, 0))
    tr_shape = lambda rows, t: jax.ShapeDtypeStruct((b, ATT_HEADS, s // t, rows, t), BF16)
    return pl.pallas_call(
        functools.partial(_inproj_kernel, tm=tm, tq=tq, tk=tk),
        out_shape=(
            tr_shape(LANES, tq),
            jax.ShapeDtypeStruct((b, s, ATT_SLOTS), BF16),
            tr_shape(ATT_V_ROWS, tk),
            jax.ShapeDtypeStruct((b, s, LANES), F32),
            jax.ShapeDtypeStruct((b, s, 2 * ML_WIDTH), F32),
            jax.ShapeDtypeStruct((b, s, ML_WIDTH), BF16),
        ),
        grid=(b, s // tm),
        in_specs=[
            row(D_MODEL),
            _resident((1, D_MODEL)),
            _resident((ATT_SLOTS, D_MODEL)),
            _resident((D_MODEL, ATT_SLOTS)),
            _resident((ATT_HEADS * ATT_V_ROWS, D_MODEL)),
            _resident((D_MODEL, LANES)),
            _resident((D_MODEL, 2 * ML_WIDTH)),
            _resident((D_MODEL, ML_WIDTH)),
        ],
        out_specs=(tr(LANES, tq), row(ATT_SLOTS), tr(ATT_V_ROWS, tk), row(LANES),
                   row(2 * ML_WIDTH), row(ML_WIDTH)),
        compiler_params=_cparams(("parallel", "parallel")),
        name="inproj",
    )(x3, g, wqt, wk, wvt, wsm, wqk, wv)


def _split3(x):
    hi = x.astype(BF16)
    r1 = x - hi.astype(F32)
    mid = r1.astype(BF16)
    lo = (r1 - mid.astype(F32)).astype(BF16)
    return hi, mid, lo


def _gates_kernel(sm_ref, bias_ref, k_ref, col_ref, row_ref, ka_ref, kmax_ref, bpre_ref,
                  carry_ref, bcarry_ref, *, ts, tk):
    @pl.when(pl.program_id(1) == 0)
    def _():
        carry_ref[...] = jnp.zeros_like(carry_ref)
        bcarry_ref[...] = jnp.full_like(bcarry_ref, NEG_BIG)
        kmax_ref[...] = jnp.zeros_like(kmax_ref)

    z = sm_ref[0] + bias_ref[...]
    lane = lax.broadcasted_iota(jnp.int32, z.shape, 1)
    is_att = lane < GATE_ML_I
    is_mlf = (lane >= GATE_ML_F) & (lane < GATE_ML_F + ML_HEADS)
    vals = jnp.where(is_att | is_mlf, _log_sigmoid(z), z)

    r = lax.broadcasted_iota(jnp.int32, (ts, ts), 0)
    c = lax.broadcasted_iota(jnp.int32, (ts, ts), 1)
    tri = c <= r
    tri_full = jnp.where(tri, 1.0, 0.0).astype(BF16)
    tri_chunk = jnp.where(tri & ((r // CHUNK) == (c // CHUNK)), 1.0, 0.0).astype(BF16)
    hi, mid, lo = _split3(vals)
    cs_full = _dot(tri_full, hi) + _dot(tri_full, mid) + _dot(tri_full, lo) + carry_ref[...]
    cs_chunk = _dot(tri_chunk, hi) + _dot(tri_chunk, mid) + _dot(tri_chunk, lo)
    carry_ref[...] = cs_full[ts - 1:ts, :]

    out = jnp.where(is_att, cs_full, jnp.where(is_mlf, cs_chunk, vals))
    col_ref[0] = out
    row_ref[0] = out.T[:GATE_ROWS, :]

    src = lax.broadcasted_iota(jnp.int32, (LANES, ATT_SLOTS), 0)
    dst = lax.broadcasted_iota(jnp.int32, (LANES, ATT_SLOTS), 1)
    decay_cols = jnp.zeros((ts, ATT_SLOTS), F32)
    decay = cs_full * -LOG2E
    for j, piece in enumerate(_split3(decay)):
        place = jnp.where((src < ATT_HEADS) & (dst == src * LANES + ATT_HEAD_DIM + j), 1.0, 0.0)
        decay_cols = decay_cols + _dot(piece, place.astype(BF16))
    kf = k_ref[0].astype(F32)
    ka_ref[0] = (kf + decay_cols).astype(BF16)

    lane_row = lax.broadcasted_iota(jnp.int32, (1, LANES), 1)
    knorm = jnp.zeros((1, LANES), F32)
    for h in range(ATT_HEADS):
        kh = kf[:, h * LANES:(h + 1) * LANES]
        sq = jnp.max(jnp.sum(kh * kh, axis=-1, keepdims=True), axis=0, keepdims=True)
        knorm = jnp.where(lane_row == h, jnp.sqrt(sq), knorm)
    kmax_ref[0] = jnp.maximum(kmax_ref[0], jnp.broadcast_to(knorm, (SUBLANES, LANES)))

    rows = []
    running = bcarry_ref[...]
    for c in range(ts // tk):
        running = jnp.maximum(running, jnp.max(decay[c * tk:(c + 1) * tk, :], axis=0, keepdims=True))
        rows.append(running)
    bcarry_ref[...] = running
    rows += [running] * (SUBLANES - len(rows))
    bpre_ref[0, 0] = jnp.concatenate(rows, axis=0)


def _gates(small, bias, k, ts, tk):
    b, s, _ = small.shape
    assert ts % tk == 0 and ts // tk <= SUBLANES
    row = lambda w: pl.BlockSpec((1, ts, w), lambda bi, si: (bi, si, 0))
    return pl.pallas_call(
        functools.partial(_gates_kernel, ts=ts, tk=tk),
        out_shape=(jax.ShapeDtypeStruct((b, s, LANES), F32),
                   jax.ShapeDtypeStruct((b, GATE_ROWS, s), F32),
                   jax.ShapeDtypeStruct((b, s, ATT_SLOTS), BF16),
                   jax.ShapeDtypeStruct((b, SUBLANES, LANES), F32),
                   jax.ShapeDtypeStruct((b, s // ts, SUBLANES, LANES), F32)),
        grid=(b, s // ts),
        in_specs=[row(LANES), _resident((1, LANES)), row(ATT_SLOTS)],
        out_specs=(row(LANES),
                   pl.BlockSpec((1, GATE_ROWS, ts), lambda bi, si: (bi, 0, si)),
                   row(ATT_SLOTS),
                   pl.BlockSpec((1, SUBLANES, LANES), lambda bi, si: (bi, 0, 0)),
                   pl.BlockSpec((1, 1, SUBLANES, LANES), lambda bi, si: (bi, si, 0, 0))),
        scratch_shapes=[pltpu.VMEM((1, LANES), F32), pltpu.VMEM((1, LANES), F32)],
        compiler_params=_cparams(("parallel", "arbitrary")),
        name="gates",
    )(small, bias, k)


def _att_kernel(qt_ref, k_ref, vt_ref, kmax_ref, bpre_ref, o_ref, s0_ref, s1_ref, p0_ref, p1_ref,
                *, tq, tk, hb, rb):
    i = pl.program_id(2)
    nd = tq // tk
    s_refs = (s0_ref, s1_ref)
    p_refs = (p0_ref, p1_ref)

    def scores(hh, chunk):
        kj = k_ref[0, pl.ds(pl.multiple_of(chunk * tk, tk), tk), hh * LANES:(hh + 1) * LANES]
        return _dot(kj, qt_ref[0, hh, 0])

    def weigh(hh, chunk, par, alpha, acc):
        return acc * alpha + _dot(vt_ref[0, hh, chunk], p_refs[par][hh])

    def visit(par, carry, pv_chunk, next_chunk, next_mask=None):
        oth = 1 - par
        out = []
        for hh in range(hb):
            m, cmax, alpha_prev, acc = carry[hh]
            acc = weigh(hh, pv_chunk, oth, alpha_prev, acc)
            m_new = jnp.maximum(m, cmax)
            alpha = jnp.exp2(m - m_new)
            for r in range(0, tk, rb):
                p_refs[par][hh, r:r + rb, :] = jnp.exp2(
                    s_refs[par][hh, r:r + rb, :] - m_new).astype(BF16)
            s_next = scores(hh, next_chunk)
            if next_mask is not None:
                s_next = jnp.where(next_mask, s_next, NEG_BIG)
            s_refs[oth][hh] = s_next
            out.append((m_new, jnp.max(s_next, axis=0, keepdims=True), alpha, acc))
        return tuple(out)

    key = lax.broadcasted_iota(jnp.int32, (tk, tq), 0)
    qry = lax.broadcasted_iota(jnp.int32, (tk, tq), 1)
    init = []
    for hh in range(hb):
        s_first = jnp.where(key <= qry, scores(hh, nd * i), NEG_BIG)
        s0_ref[hh] = s_first
        p1_ref[hh] = jnp.zeros((tk, tq), BF16)
        init.append((jnp.full((1, tq), NEG_BIG, F32), jnp.max(s_first, axis=0, keepdims=True),
                     jnp.ones((1, tq), F32), jnp.zeros((ATT_V_ROWS, tq), F32)))
    carry = tuple(init)
    below = nd * i
    for d in range(nd):
        pv_chunk = below + d - 1 if d else 0
        if d + 1 < nd:
            carry = visit(d % 2, carry, pv_chunk, below + d + 1, key + (d + 1) * tk <= qry)
        else:
            carry = visit(d % 2, carry, pv_chunk, jnp.maximum(below - 1, 0))

    lane = lax.broadcasted_iota(jnp.int32, (1, LANES), 1)
    bound = jnp.full((1, LANES), NEG_BIG, F32)
    for hh in range(hb):
        h = pl.program_id(1) * hb + hh
        qf = qt_ref[0, hh, 0][:ATT_HEAD_DIM, :].astype(F32)
        qmax = jnp.sqrt(jnp.max(jnp.sum(qf * qf, axis=0, keepdims=True), axis=-1, keepdims=True))
        mmin = jnp.min(carry[hh][0], axis=-1, keepdims=True)
        bound = jnp.where(lane == h, qmax * kmax_ref[0, 0:1, :] - mmin, bound)
    chunk_id = lax.broadcasted_iota(jnp.int32, bpre_ref.shape[1:], 0)
    live = (bpre_ref[0] + bound >= -ATT_SKIP_EXPONENT) & (chunk_id < below)
    n_live = jnp.max(jnp.sum(jnp.where(live, 1.0, 0.0), axis=0, keepdims=True)).astype(jnp.int32)
    n_pairs = (n_live + 1) // 2

    def pair(w, carry):
        top = below - 2 * w
        carry = visit(0, carry, jnp.where(w == 0, below + nd - 1, top), top - 2)
        return visit(1, carry, top - 1, jnp.maximum(top - 3, 0))

    carry = lax.fori_loop(0, n_pairs, pair, carry)
    last_chunk = jnp.where(n_pairs == 0, below + nd - 1, below - 2 * n_pairs)
    pad_rows = jnp.zeros((LANES - ATT_V_ROWS, tq), F32)
    for pr in range(hb // 2):
        halves = []
        for hh in (2 * pr, 2 * pr + 1):
            _, _, alpha, acc = carry[hh]
            acc = weigh(hh, last_chunk, 1, alpha, acc)
            acc_t = jnp.concatenate([acc, pad_rows], axis=0).T
            halves.append(acc_t / acc_t[:, ATT_HEAD_DIM:ATT_HEAD_DIM + 1])
        both = jnp.where(lane < ATT_HEAD_DIM, halves[0],
                         pltpu.roll(halves[1], ATT_HEAD_DIM, axis=1))
        o_ref[0, :, pr * LANES:(pr + 1) * LANES] = both.astype(o_ref.dtype)


def _attention(qt, k, vt, kmax, bpre, tq, tk, hb):
    b, s, _ = k.shape
    assert tq % (2 * tk) == 0 and bpre.shape == (b, s // tk, LANES)
    return pl.pallas_call(
        functools.partial(_att_kernel, tq=tq, tk=tk, hb=hb, rb=ATT_ROW_BLOCK),
        out_shape=jax.ShapeDtypeStruct((b, s, ATT_WIDTH), BF16),
        grid=(b, ATT_HEADS // hb, s // tq),
        in_specs=[
            pl.BlockSpec((1, hb, 1, LANES, tq), lambda bi, hg, i: (bi, hg, i, 0, 0)),
            pl.BlockSpec((1, s, hb * LANES), lambda bi, hg, i: (bi, 0, hg),
                         pipeline_mode=pl.Buffered(1)),
            pl.BlockSpec((1, hb, s // tk, ATT_V_ROWS, tk), lambda bi, hg, i: (bi, hg, 0, 0, 0),
                         pipeline_mode=pl.Buffered(1)),
            pl.BlockSpec((1, SUBLANES, LANES), lambda bi, hg, i: (bi, 0, 0)),
            pl.BlockSpec((1, s // tk, LANES), lambda bi, hg, i: (bi, 0, 0)),
        ],
        out_specs=pl.BlockSpec((1, tq, hb * ATT_HEAD_DIM), lambda bi, hg, i: (bi, i, hg)),
        scratch_shapes=[pltpu.VMEM((hb, tk, tq), F32), pltpu.VMEM((hb, tk, tq), F32),
                        pltpu.VMEM((hb, tk, tq), BF16), pltpu.VMEM((hb, tk, tq), BF16)],
        compiler_params=_cparams(("parallel", "parallel", "arbitrary")),
        name="fox_attention",
    )(qt, k, vt, kmax, bpre)


def _mlstm_kernel(qk_ref, v_ref, gcol_ref, grow_ref, cw_ref, nw_ref, o_ref,
                  prev_ref, ct_ref, n_ref, m_ref):
    @pl.when(pl.program_id(1) == 0)
    def _():
        prev_ref[...] = jnp.zeros_like(prev_ref)
        ct_ref[...] = jnp.zeros_like(ct_ref)
        n_ref[...] = jnp.zeros_like(n_ref)
        m_ref[...] = jnp.zeros_like(m_ref)

    L = CHUNK
    x = qk_ref[0]
    xcat = jnp.concatenate([prev_ref[...], x], axis=0)
    cw = cw_ref[...]
    conv = x * cw[CONV_WIDTH - 1:CONV_WIDTH, :]
    for j in range(CONV_WIDTH - 1):
        off = SUBLANES - (CONV_WIDTH - 1) + j
        conv = conv + xcat[off:off + L, :] * cw[j:j + 1, :]
    prev_ref[...] = x[L - SUBLANES:, :]
    qk = conv * jax.nn.sigmoid(conv)

    gcol = gcol_ref[0]
    grow = grow_ref[0]
    vall = v_ref[0]
    r = lax.broadcasted_iota(jnp.int32, (L, L), 0)
    c = lax.broadcasted_iota(jnp.int32, (L, L), 1)
    causal = c <= r
    nw = nw_ref[...]

    for h in range(ML_HEADS):
        sl = slice(h * ML_HEAD_DIM, (h + 1) * ML_HEAD_DIM)
        q = qk[:, sl] * (ML_HEAD_DIM ** -0.5)
        k = qk[:, ML_WIDTH + h * ML_HEAD_DIM:ML_WIDTH + (h + 1) * ML_HEAD_DIM]
        v = vall[:, sl]
        li_col = gcol[:, GATE_ML_I + h:GATE_ML_I + h + 1]
        b_col = gcol[:, GATE_ML_F + h:GATE_ML_F + h + 1]
        li_row = grow[GATE_ML_I + h:GATE_ML_I + h + 1, :]
        b_row = grow[GATE_ML_F + h:GATE_ML_F + h + 1, :]
        m = m_ref[h]
        ct = ct_ref[h]
        n = n_ref[h]

        a = b_col + m
        dmat = jnp.where(causal, b_col - b_row + li_row, NEG_BIG)
        m_t = jnp.maximum(a, jnp.max(dmat, axis=-1, keepdims=True))
        w_intra = jnp.exp(dmat - m_t)
        w_inter = jnp.exp(a - m_t)
        qb = q.astype(BF16)
        qkw = _dot_nt(qb, k.astype(BF16)) * w_intra
        num = _dot(qkw.astype(BF16), v) + w_inter * _dot(qb, ct.astype(BF16))
        den = (jnp.sum(qkw, axis=-1, keepdims=True)
               + w_inter * jnp.sum(q * n, axis=-1, keepdims=True))
        hh = num / jnp.maximum(jnp.abs(den), jnp.exp(-m_t))
        o_ref[0, :, sl] = _rms(hh, nw[:, sl])

        b_last = b_row[:, L - 1:L]
        g_row = b_last - b_row + li_row
        m_new = jnp.maximum(b_last + m, jnp.max(g_row, axis=-1, keepdims=True))
        w_s = jnp.exp(b_last - b_col + li_col - m_new)
        decay = jnp.exp(b_last + m - m_new)
        wv = (w_s * v.astype(F32)).astype(BF16)
        ct_ref[h] = decay * ct + _dot(k.T.astype(BF16), wv)
        n_ref[h] = decay * n + jnp.sum(w_s * k, axis=0, keepdims=True)
        m_ref[h] = m_new


def _mlstm(mqk, mv, gcol, grow, conv_w, norm_w):
    b, s, _ = mqk.shape
    nc = s // CHUNK
    return pl.pallas_call(
        _mlstm_kernel,
        out_shape=jax.ShapeDtypeStruct((b, s, ML_WIDTH), F32),
        grid=(b, nc),
        in_specs=[
            pl.BlockSpec((1, CHUNK, 2 * ML_WIDTH), lambda bi, ci: (bi, ci, 0)),
            pl.BlockSpec((1, CHUNK, ML_WIDTH), lambda bi, ci: (bi, ci, 0)),
            pl.BlockSpec((1, CHUNK, LANES), lambda bi, ci: (bi, ci, 0)),
            pl.BlockSpec((1, GATE_ROWS, CHUNK), lambda bi, ci: (bi, 0, ci)),
            _resident((CONV_WIDTH, 2 * ML_WIDTH)),
            _resident((1, ML_WIDTH)),
        ],
        out_specs=pl.BlockSpec((1, CHUNK, ML_WIDTH), lambda bi, ci: (bi, ci, 0)),
        scratch_shapes=[
            pltpu.VMEM((SUBLANES, 2 * ML_WIDTH), F32),
            pltpu.VMEM((ML_HEADS, ML_HEAD_DIM, ML_HEAD_DIM), F32),
            pltpu.VMEM((ML_HEADS, 1, ML_HEAD_DIM), F32),
            pltpu.VMEM((ML_HEADS, 1, 1), F32),
        ],
        compiler_params=_cparams(("parallel", "arbitrary")),
        name="mlstm",
    )(mqk, mv, gcol, grow, conv_w, norm_w)


def _gelu_tanh(x):
    return 0.5 * x * (1.0 + jnp.tanh(0.7978845608028654 * (x + 0.044715 * (x * x * x))))


def _merge_kernel(x_ref, yatt_ref, hm_ref, g_ref, wo_ref, wuv_ref, wgt_ref, ng_ref,
                  wsp_ref, bsp_ref, wba_ref, wbm_ref, wbg_ref, wout_ref, o_ref, *, tm):
    x = x_ref[...]
    hn = _rms(x, g_ref[...]).astype(BF16)

    y_ml = jax.nn.sigmoid(_dot(hn, wo_ref[...])) * hm_ref[...]

    uv = _gelu_tanh(_dot(hn, wuv_ref[...]))
    u = uv[:, :GM_WIDTH]
    vn = _rms(uv[:, GM_WIDTH:], ng_ref[...]).astype(BF16)
    r = lax.broadcasted_iota(jnp.int32, (tm, tm), 0)
    c = lax.broadcasted_iota(jnp.int32, (tm, tm), 1)
    keep = (c <= r) & ((r // CHUNK) == (c // CHUNK))
    mixed = []
    for gi in range(GM_GROUPS):
        w = jnp.where(keep, wsp_ref[gi], 0.0).astype(BF16)
        mixed.append(_dot(w, vn[:, gi * GM_GROUP_DIM:(gi + 1) * GM_GROUP_DIM]))
    y_gm = u * (jnp.concatenate(mixed, axis=-1) + bsp_ref[...])

    gates = jax.nn.sigmoid(_dot(hn, wgt_ref[...]))
    merged = (gates[:, :D_MODEL] * _dot(yatt_ref[...], wba_ref[...])
              + gates[:, D_MODEL:2 * D_MODEL] * _dot(y_ml.astype(BF16), wbm_ref[...])
              + gates[:, 2 * D_MODEL:] * _dot(y_gm.astype(BF16), wbg_ref[...]))
    o_ref[...] = x + _dot(merged.astype(BF16), wout_ref[...])


def _merge(x2, yatt, hm, g, wo, wuv, wgt, ng, wsp, bsp, wba, wbm, wbg, wout, tm):
    n = x2.shape[0]
    row = lambda w: pl.BlockSpec((tm, w), lambda i: (i, 0))
    return pl.pallas_call(
        functools.partial(_merge_kernel, tm=tm),
        out_shape=jax.ShapeDtypeStruct((n, D_MODEL), F32),
        grid=(n // tm,),
        in_specs=[
            row(D_MODEL), row(ATT_WIDTH), row(ML_WIDTH),
            _resident((1, D_MODEL)),
            _resident((D_MODEL, ML_WIDTH)),
            _resident((D_MODEL, 2 * GM_WIDTH)),
            _resident((D_MODEL, N_BRANCH * D_MODEL)),
            _resident((1, GM_WIDTH)),
            _resident((GM_GROUPS, tm, tm)),
            _resident((tm, GM_WIDTH)),
            _resident((ATT_WIDTH, D_MODEL)),
            _resident((ML_WIDTH, D_MODEL)),
            _resident((GM_WIDTH, D_MODEL)),
            _resident((D_MODEL, D_MODEL)),
        ],
        out_specs=row(D_MODEL),
        compiler_params=_cparams(("parallel",)),
        name="merge",
    )(x2, yatt, hm, g, wo, wuv, wgt, ng, wsp, bsp, wba, wbm, wbg, wout)


def _final_kernel(x_ref, g_ref, o_ref):
    o_ref[...] = _rms(x_ref[...], g_ref[...])


def _final_norm(x2, g, tm):
    n = x2.shape[0]
    return pl.pallas_call(
        _final_kernel,
        out_shape=jax.ShapeDtypeStruct((n, D_MODEL), F32),
        grid=(n // tm,),
        in_specs=[pl.BlockSpec((tm, D_MODEL), lambda i: (i, 0)), _resident((1, D_MODEL))],
        out_specs=pl.BlockSpec((tm, D_MODEL), lambda i: (i, 0)),
        compiler_params=_cparams(("parallel",)),
        name="final_norm",
    )(x2, g)


def _tile(n, pref):
    t = min(n, pref)
    assert n % t == 0, (n, t)
    return t


def _prepare_layer_params(p, tm_merge):
    depth = p["w_in"].shape[0]
    o_att = 3 * ATT_WIDTH
    o_mqk = o_att + ATT_HEADS
    o_mv = o_mqk + 2 * ML_WIDTH
    o_mi = o_mv + ML_WIDTH
    o_mf = o_mi + ML_HEADS
    o_mo = o_mf + ML_HEADS
    o_uv = o_mo + ML_WIDTH
    o_gt = o_uv + 2 * GM_WIDTH
    w_in = p["w_in"]
    w_small = jnp.concatenate(
        [w_in[:, :, o_att:o_mqk], w_in[:, :, o_mi:o_mf], w_in[:, :, o_mf:o_mo],
         jnp.zeros((depth, D_MODEL, LANES - GATE_ROWS), F32)], axis=-1)
    b_small = jnp.concatenate(
        [p["b_f_att"], p["b_i_ml"], p["b_f_ml"], jnp.zeros((depth, LANES - GATE_ROWS), F32)],
        axis=-1)[:, None, :]
    reps = tm_merge // CHUNK
    bf = lambda a: a.astype(BF16)
    row = lambda a: a[:, None, :]

    def slots(w, width):
        w = w.reshape(depth, D_MODEL, ATT_HEADS, ATT_HEAD_DIM)
        w = jnp.pad(w, ((0, 0), (0, 0), (0, 0), (0, width - ATT_HEAD_DIM)))
        return w.reshape(depth, D_MODEL, ATT_HEADS * width)

    w_q = slots(w_in[:, :, :ATT_WIDTH], LANES) * (ATT_HEAD_DIM ** -0.5 * LOG2E)
    w_k = slots(w_in[:, :, ATT_WIDTH:2 * ATT_WIDTH], LANES)
    w_v = slots(w_in[:, :, 2 * ATT_WIDTH:o_att], ATT_V_ROWS)
    return dict(
        norm_ffn1=row(p["norm_ffn1"]), ffn1_gate=bf(p["ffn1_gate"]), ffn1_up=bf(p["ffn1_up"]),
        ffn1_down=bf(p["ffn1_down"]),
        norm_mix=row(p["norm_mix"]),
        w_qt=bf(jnp.swapaxes(w_q, 1, 2)), w_k=bf(w_k), w_vt=bf(jnp.swapaxes(w_v, 1, 2)),
        w_small=bf(w_small), b_small=b_small,
        w_mqk=bf(w_in[:, :, o_mqk:o_mv]), w_mv=bf(w_in[:, :, o_mv:o_mi]),
        w_mo=bf(w_in[:, :, o_mo:o_uv]), w_uv=bf(w_in[:, :, o_uv:o_gt]), w_gt=bf(w_in[:, :, o_gt:]),
        conv_ml=p["conv_ml"], norm_ml_head=row(p["norm_ml_head"]), norm_gmlp=row(p["norm_gmlp"]),
        w_spatial=jnp.tile(p["w_spatial"], (1, 1, reps, reps)),
        b_spatial=jnp.tile(jnp.repeat(jnp.swapaxes(p["b_spatial"], 1, 2), GM_GROUP_DIM, axis=2),
                           (1, reps, 1)),
        w_br_att=bf(p["w_br_att"]), w_br_ml=bf(p["w_br_ml"]), w_br_gmlp=bf(p["w_br_gmlp"]),
        w_out=bf(p["w_out"]),
        norm_ffn2=row(p["norm_ffn2"]), ffn2_gate=bf(p["ffn2_gate"]), ffn2_up=bf(p["ffn2_up"]),
        ffn2_down=bf(p["ffn2_down"]),
    )


def _layer(x2, lp, b, s, tiles):
    n = b * s
    x2 = _ffn(x2, lp["norm_ffn1"], lp["ffn1_gate"], lp["ffn1_up"], lp["ffn1_down"], tiles["ffn"])
    qt, k, vt, small, mqk, mv = _inproj(
        x2.reshape(b, s, D_MODEL), lp["norm_mix"], lp["w_qt"], lp["w_k"], lp["w_vt"],
        lp["w_small"], lp["w_mqk"], lp["w_mv"], tiles["inproj"], tiles["att_q"], tiles["att_k"])
    gcol, grow, k, kmax, bpre = _gates(small, lp["b_small"], k, tiles["gates"], tiles["att_k"])
    bpre = bpre[:, :, :tiles["gates"] // tiles["att_k"], :].reshape(b, s // tiles["att_k"], LANES)
    yatt = _attention(qt, k, vt, kmax, bpre, tiles["att_q"], tiles["att_k"], tiles["att_heads"])
    hm = _mlstm(mqk, mv, gcol, grow, lp["conv_ml"], lp["norm_ml_head"])
    x2 = _merge(x2, yatt.reshape(n, ATT_WIDTH), hm.reshape(n, ML_WIDTH), lp["norm_mix"],
                lp["w_mo"], lp["w_uv"], lp["w_gt"], lp["norm_gmlp"], lp["w_spatial"],
                lp["b_spatial"], lp["w_br_att"], lp["w_br_ml"], lp["w_br_gmlp"], lp["w_out"],
                tiles["merge"])
    return _ffn(x2, lp["norm_ffn2"], lp["ffn2_gate"], lp["ffn2_up"], lp["ffn2_down"], tiles["ffn"])


def _tiles_for(n, s):
    att_q = _tile(s, 512)
    return dict(ffn=_tile(n, 512), inproj=att_q, gates=_tile(s, 512), att_q=att_q,
                att_k=min(att_q // 2, 256), att_heads=2, merge=_tile(s, 256), final=_tile(n, 1024))


def _trunk(x, params, norm_final):
    b, s, _ = x.shape
    n = b * s
    tiles = _tiles_for(n, s)
    stacked = _prepare_layer_params(params, tiles["merge"])

    def body(x2, lp):
        return _layer(x2, lp, b, s, tiles), None

    x2, _ = lax.scan(body, x.reshape(n, D_MODEL), stacked)
    return _final_norm(x2, norm_final[None, :], tiles["final"]).reshape(b, s, D_MODEL)


def kernel(x, norm_ffn1, ffn1_gate, ffn1_up, ffn1_down, norm_mix, w_in, b_f_att, b_i_ml, b_f_ml, conv_ml, norm_ml_head, norm_gmlp, w_spatial, b_spatial, w_br_att, w_br_ml, w_br_gmlp, w_out, norm_ffn2, ffn2_gate, ffn2_up, ffn2_down, norm_final):
    params = dict(norm_ffn1=norm_ffn1, ffn1_gate=ffn1_gate, ffn1_up=ffn1_up, ffn1_down=ffn1_down,
                  norm_mix=norm_mix, w_in=w_in, b_f_att=b_f_att, b_i_ml=b_i_ml, b_f_ml=b_f_ml,
                  conv_ml=conv_ml, norm_ml_head=norm_ml_head, norm_gmlp=norm_gmlp,
                  w_spatial=w_spatial, b_spatial=b_spatial, w_br_att=w_br_att, w_br_ml=w_br_ml,
                  w_br_gmlp=w_br_gmlp, w_out=w_out, norm_ffn2=norm_ffn2, ffn2_gate=ffn2_gate,
                  ffn2_up=ffn2_up, ffn2_down=ffn2_down)
    return _trunk(x, params, norm_final)
```

```python
import functools

import jax
import jax.numpy as jnp
from jax import lax
from jax.experimental import pallas as pl
from jax.experimental.pallas import tpu as pltpu

D_MODEL = 1024
ATT_HEADS = 8
ATT_HEAD_DIM = 64
ATT_WIDTH = ATT_HEADS * ATT_HEAD_DIM
ML_HEADS = 4
ML_HEAD_DIM = 128
ML_WIDTH = ML_HEADS * ML_HEAD_DIM
CONV_WIDTH = 4
GM_GROUPS = 4
GM_GROUP_DIM = 128
GM_WIDTH = GM_GROUPS * GM_GROUP_DIM
CHUNK = 128
D_FF = 2816
FFN_RES = 0.5
N_BRANCH = 3
EPS = 1e-6

LANES = 128
SUBLANES = 8
ATT_SLOTS = ATT_HEADS * LANES
ATT_ROW_BLOCK = 32
ATT_V_ROWS = 80
LOG2E = 1.4426950408889634
ATT_SKIP_EXPONENT = 160.0
VMEM_LIMIT = 56 * 1024 * 1024
NEG_BIG = -1e30

GATE_ATT_F = 0
GATE_ML_I = ATT_HEADS
GATE_ML_F = ATT_HEADS + ML_HEADS
GATE_ROWS = 16
ML_REP = 3 * ML_HEADS * 128

F32 = jnp.float32
BF16 = jnp.bfloat16


def _cparams(sem):
    return pltpu.CompilerParams(dimension_semantics=sem, vmem_limit_bytes=VMEM_LIMIT)


def _resident(shape):
    nd = len(shape)
    return pl.BlockSpec(shape, lambda *_: (0,) * nd, pipeline_mode=pl.Buffered(1))


def _rms(x, g):
    ms = jnp.mean(x * x, axis=-1, keepdims=True)
    return x * lax.rsqrt(ms + EPS) * g


def _log_sigmoid(z):
    return jnp.minimum(z, 0.0) - jnp.log1p(jnp.exp(-jnp.abs(z)))


def _dot(a, b):
    return jnp.dot(a, b, preferred_element_type=F32)


def _dot_nt(a, b):
    return lax.dot_general(a, b, (((1,), (1,)), ((), ())), preferred_element_type=F32)


def _ffn_kernel(x_ref, g_ref, wg_ref, wu_ref, wd_ref, o_ref):
    x = x_ref[...]
    hn = _rms(x, g_ref[...]).astype(BF16)
    gate = _dot(hn, wg_ref[...])
    up = _dot(hn, wu_ref[...])
    act = (gate * jax.nn.sigmoid(gate) * up).astype(BF16)
    o_ref[...] = x + FFN_RES * _dot(act, wd_ref[...])


def _ffn(x2, g, wg, wu, wd, tm):
    n = x2.shape[0]
    return pl.pallas_call(
        _ffn_kernel,
        out_shape=jax.ShapeDtypeStruct((n, D_MODEL), F32),
        grid=(n // tm,),
        in_specs=[
            pl.BlockSpec((tm, D_MODEL), lambda i: (i, 0)),
            _resident((1, D_MODEL)),
            _resident((D_MODEL, D_FF)),
            _resident((D_MODEL, D_FF)),
            _resident((D_FF, D_MODEL)),
        ],
        out_specs=pl.BlockSpec((tm, D_MODEL), lambda i: (i, 0)),
        compiler_params=_cparams(("parallel",)),
        name="ffn",
    )(x2, g, wg, wu, wd)


def _inproj_kernel(x_ref, xp_ref, g_ref, wqt_ref, wk_ref, wvt_ref, wsm_ref, wqk_ref, wv_ref,
                   cw_ref, qt_ref, k_ref, vt_ref, sm_ref, mq_ref, mk_ref, mv_ref, *, tm, tq, tk):
    hn = _rms(x_ref[0], g_ref[...])
    hb = hn.astype(BF16)
    hnt = hn.T.astype(BF16)
    q_row = lax.broadcasted_iota(jnp.int32, (ATT_SLOTS, tm), 0) % LANES
    qt = _dot(wqt_ref[...], hnt)
    qt = jnp.where((q_row >= ATT_HEAD_DIM) & (q_row < ATT_HEAD_DIM + 3), 1.0, qt).astype(BF16)
    v_row = lax.broadcasted_iota(jnp.int32, (ATT_HEADS * ATT_V_ROWS, tm), 0) % ATT_V_ROWS
    vt = _dot(wvt_ref[...], hnt)
    vt = jnp.where(v_row == ATT_HEAD_DIM, 1.0, vt).astype(BF16)
    for c in range(tm // tq):
        qt_ref[0, :, c] = qt[:, c * tq:(c + 1) * tq].reshape(ATT_HEADS, LANES, tq)
    for c in range(tm // tk):
        vt_ref[0, :, c] = vt[:, c * tk:(c + 1) * tk].reshape(ATT_HEADS, ATT_V_ROWS, tk)
    k_ref[0] = _dot(hb, wk_ref[...]).astype(BF16)
    sm_ref[0] = _dot(hb, wsm_ref[...])
    mv_ref[0] = _dot(hb, wv_ref[...]).astype(BF16)

    proj = _dot(hb, wqk_ref[...])
    before = _dot(_rms(xp_ref[0], g_ref[...]).astype(BF16), wqk_ref[...])
    before = jnp.where(pl.program_id(1) > 0, before, 0.0)
    xcat = jnp.concatenate([before, proj], axis=0)
    cw = cw_ref[...]
    conv = proj * cw[CONV_WIDTH - 1:CONV_WIDTH, :]
    for j in range(CONV_WIDTH - 1):
        off = SUBLANES - (CONV_WIDTH - 1) + j
        conv = conv + xcat[off:off + tm, :] * cw[j:j + 1, :]
    act = conv * jax.nn.sigmoid(conv)
    mq_ref[0] = (act[:, :ML_WIDTH] * (ML_HEAD_DIM ** -0.5)).astype(BF16)
    mk_ref[0] = act[:, ML_WIDTH:].astype(BF16)


def _inproj(x3, g, wqt, wk, wvt, wsm, wqk, wv, cw, tm, tq, tk):
    b, s, _ = x3.shape
    row = lambda w: pl.BlockSpec((1, tm, w), lambda bi, si: (bi, si, 0))
    rows_before = pl.BlockSpec((1, SUBLANES, D_MODEL),
                               lambda bi, si: (bi, jnp.maximum(si * (tm // SUBLANES) - 1, 0), 0))
    tr = lambda rows, t: pl.BlockSpec((1, ATT_HEADS, tm // t, rows, t),
                                      lambda bi, si: (bi, 0, si, 0, 0))
    tr_shape = lambda rows, t: jax.ShapeDtypeStruct((b, ATT_HEADS, s // t, rows, t), BF16)
    return pl.pallas_call(
        functools.partial(_inproj_kernel, tm=tm, tq=tq, tk=tk),
        out_shape=(
            tr_shape(LANES, tq),
            jax.ShapeDtypeStruct((b, s, ATT_SLOTS), BF16),
            tr_shape(ATT_V_ROWS, tk),
            jax.ShapeDtypeStruct((b, s, LANES), F32),
            jax.ShapeDtypeStruct((b, s, ML_WIDTH), BF16),
            jax.ShapeDtypeStruct((b, s, ML_WIDTH), BF16),
            jax.ShapeDtypeStruct((b, s, ML_WIDTH), BF16),
        ),
        grid=(b, s // tm),
        in_specs=[
            row(D_MODEL),
            rows_before,
            _resident((1, D_MODEL)),
            _resident((ATT_SLOTS, D_MODEL)),
            _resident((D_MODEL, ATT_SLOTS)),
            _resident((ATT_HEADS * ATT_V_ROWS, D_MODEL)),
            _resident((D_MODEL, LANES)),
            _resident((D_MODEL, 2 * ML_WIDTH)),
            _resident((D_MODEL, ML_WIDTH)),
            _resident((CONV_WIDTH, 2 * ML_WIDTH)),
        ],
        out_specs=(tr(LANES, tq), row(ATT_SLOTS), tr(ATT_V_ROWS, tk), row(LANES),
                   row(ML_WIDTH), row(ML_WIDTH), row(ML_WIDTH)),
        compiler_params=_cparams(("parallel", "parallel")),
        name="inproj",
    )(x3, x3, g, wqt, wk, wvt, wsm, wqk, wv, cw)


def _split3(x):
    hi = x.astype(BF16)
    r1 = x - hi.astype(F32)
    mid = r1.astype(BF16)
    lo = (r1 - mid.astype(F32)).astype(BF16)
    return hi, mid, lo


def _place(x, moves):
    r = lax.broadcasted_iota(jnp.int32, (LANES, LANES), 0)
    c = lax.broadcasted_iota(jnp.int32, (LANES, LANES), 1)
    mat = jnp.zeros((LANES, LANES), F32)
    for src, dst, sign in moves:
        mat = jnp.where((r == src) & (c == dst), sign, mat)
    mat3 = jnp.concatenate([mat.astype(BF16)] * 3, axis=0)
    return _dot(jnp.concatenate(_split3(x), axis=-1), mat3)


def _gates_kernel(sm_ref, bias_ref, k_ref, col_ref, row_ref, ka_ref, kmax_ref, bpre_ref,
                  carry_ref, bcarry_ref, *, ts, tk):
    @pl.when(pl.program_id(1) == 0)
    def _():
        carry_ref[...] = jnp.zeros_like(carry_ref)
        bcarry_ref[...] = jnp.full_like(bcarry_ref, NEG_BIG)
        kmax_ref[...] = jnp.zeros_like(kmax_ref)

    z = sm_ref[0] + bias_ref[...]
    lane = lax.broadcasted_iota(jnp.int32, z.shape, 1)
    is_att = lane < GATE_ML_I
    is_mlf = (lane >= GATE_ML_F) & (lane < GATE_ML_F + ML_HEADS)
    vals = jnp.where(is_att | is_mlf, _log_sigmoid(z), z)

    r = lax.broadcasted_iota(jnp.int32, (ts, ts), 0)
    c = lax.broadcasted_iota(jnp.int32, (ts, ts), 1)
    tri = c <= r
    tri_full = jnp.where(tri, 1.0, 0.0).astype(BF16)
    tri_chunk = jnp.where(tri & ((r // CHUNK) == (c // CHUNK)), 1.0, 0.0).astype(BF16)
    pieces = jnp.concatenate(_split3(vals), axis=-1)
    add3 = lambda y: y[:, :LANES] + y[:, LANES:2 * LANES] + y[:, 2 * LANES:]
    cs_full = add3(_dot(tri_full, pieces)) + carry_ref[...]
    cs_chunk = add3(_dot(tri_chunk, pieces))
    carry_ref[...] = cs_full[ts - 1:ts, :]

    li_moves, b_moves = [], []
    for h in range(ML_HEADS):
        li_moves += [(GATE_ML_I + h, ML_HEADS + h, 1.0), (GATE_ML_I + h, 2 * ML_HEADS + h, 1.0)]
        b_moves += [(GATE_ML_F + h, h, 1.0), (GATE_ML_F + h, ML_HEADS + h, -1.0),
                    (GATE_ML_F + h, 2 * ML_HEADS + h, -1.0)]
    g = _place(vals, li_moves) + _place(cs_chunk, b_moves)
    pos = lax.broadcasted_iota(jnp.int32, (ts, LANES), 0) % CHUNK
    cmax = g
    d = 1
    while d < CHUNK:
        cmax = jnp.where(pos >= d, jnp.maximum(cmax, pltpu.roll(cmax, d, axis=0)), cmax)
        d *= 2
    g = jnp.where(lane >= 2 * ML_HEADS, cmax, g)
    col_ref[0] = g
    row_ref[0] = g.T[:GATE_ROWS, :]

    src = lax.broadcasted_iota(jnp.int32, (3 * LANES, ATT_SLOTS), 0)
    dst = lax.broadcasted_iota(jnp.int32, (3 * LANES, ATT_SLOTS), 1)
    place = (src % LANES < ATT_HEADS) & (dst == (src % LANES) * LANES + ATT_HEAD_DIM + src // LANES)
    decay = cs_full * -LOG2E
    decay_cols = _dot(jnp.concatenate(_split3(decay), axis=-1),
                      jnp.where(place, 1.0, 0.0).astype(BF16))
    kf = k_ref[0].astype(F32)
    ka_ref[0] = (kf + decay_cols).astype(BF16)

    lane_row = lax.broadcasted_iota(jnp.int32, (1, LANES), 1)
    knorm = jnp.zeros((1, LANES), F32)
    for h in range(ATT_HEADS):
        kh = kf[:, h * LANES:(h + 1) * LANES]
        sq = jnp.max(jnp.sum(kh * kh, axis=-1, keepdims=True), axis=0, keepdims=True)
        knorm = jnp.where(lane_row == h, jnp.sqrt(sq), knorm)
    kmax_ref[0] = jnp.maximum(kmax_ref[0], jnp.broadcast_to(knorm, (SUBLANES, LANES)))

    rows = []
    running = bcarry_ref[...]
    for c in range(ts // tk):
        running = jnp.maximum(running, jnp.max(decay[c * tk:(c + 1) * tk, :], axis=0, keepdims=True))
        rows.append(running)
    bcarry_ref[...] = running
    rows += [running] * (SUBLANES - len(rows))
    bpre_ref[0, 0] = jnp.concatenate(rows, axis=0)


def _gates(small, bias, k, ts, tk):
    b, s, _ = small.shape
    assert ts % tk == 0 and ts // tk <= SUBLANES
    row = lambda w: pl.BlockSpec((1, ts, w), lambda bi, si: (bi, si, 0))
    return pl.pallas_call(
        functools.partial(_gates_kernel, ts=ts, tk=tk),
        out_shape=(jax.ShapeDtypeStruct((b, s, LANES), F32),
                   jax.ShapeDtypeStruct((b, GATE_ROWS, s), F32),
                   jax.ShapeDtypeStruct((b, s, ATT_SLOTS), BF16),
                   jax.ShapeDtypeStruct((b, SUBLANES, LANES), F32),
                   jax.ShapeDtypeStruct((b, s // ts, SUBLANES, LANES), F32)),
        grid=(b, s // ts),
        in_specs=[row(LANES), _resident((1, LANES)), row(ATT_SLOTS)],
        out_specs=(row(LANES),
                   pl.BlockSpec((1, GATE_ROWS, ts), lambda bi, si: (bi, 0, si)),
                   row(ATT_SLOTS),
                   pl.BlockSpec((1, SUBLANES, LANES), lambda bi, si: (bi, 0, 0)),
                   pl.BlockSpec((1, 1, SUBLANES, LANES), lambda bi, si: (bi, si, 0, 0))),
        scratch_shapes=[pltpu.VMEM((1, LANES), F32), pltpu.VMEM((1, LANES), F32)],
        compiler_params=_cparams(("parallel", "arbitrary")),
        name="gates",
    )(small, bias, k)


def _att_kernel(qt_ref, k_ref, vt_ref, kmax_ref, bpre_ref, o_ref, s0_ref, s1_ref, p0_ref, p1_ref,
                *, tq, tk, hb, rb):
    i = pl.program_id(2)
    nd = tq // tk
    s_refs = (s0_ref, s1_ref)
    p_refs = (p0_ref, p1_ref)

    def scores(hh, chunk):
        kj = k_ref[0, pl.ds(pl.multiple_of(chunk * tk, tk), tk), hh * LANES:(hh + 1) * LANES]
        return _dot(kj, qt_ref[0, hh, 0])

    def weigh(hh, chunk, par, alpha, acc):
        return acc * alpha + _dot(vt_ref[0, hh, chunk], p_refs[par][hh])

    def visit(par, carry, pv_chunk, next_chunk, next_mask=None):
        oth = 1 - par
        out = []
        for hh in range(hb):
            m, cmax, alpha_prev, acc = carry[hh]
            acc = weigh(hh, pv_chunk, oth, alpha_prev, acc)
            m_new = jnp.maximum(m, cmax)
            alpha = jnp.exp2(m - m_new)
            for r in range(0, tk, rb):
                p_refs[par][hh, r:r + rb, :] = jnp.exp2(
                    s_refs[par][hh, r:r + rb, :] - m_new).astype(BF16)
            s_next = scores(hh, next_chunk)
            if next_mask is not None:
                s_next = jnp.where(next_mask, s_next, NEG_BIG)
            s_refs[oth][hh] = s_next
            out.append((m_new, jnp.max(s_next, axis=0, keepdims=True), alpha, acc))
        return tuple(out)

    key = lax.broadcasted_iota(jnp.int32, (tk, tq), 0)
    qry = lax.broadcasted_iota(jnp.int32, (tk, tq), 1)
    init = []
    for hh in range(hb):
        s_first = jnp.where(key <= qry, scores(hh, nd * i), NEG_BIG)
        s0_ref[hh] = s_first
        p1_ref[hh] = jnp.zeros((tk, tq), BF16)
        init.append((jnp.full((1, tq), NEG_BIG, F32), jnp.max(s_first, axis=0, keepdims=True),
                     jnp.ones((1, tq), F32), jnp.zeros((ATT_V_ROWS, tq), F32)))
    carry = tuple(init)
    below = nd * i
    for d in range(nd):
        pv_chunk = below + d - 1 if d else 0
        if d + 1 < nd:
            carry = visit(d % 2, carry, pv_chunk, below + d + 1, key + (d + 1) * tk <= qry)
        else:
            carry = visit(d % 2, carry, pv_chunk, jnp.maximum(below - 1, 0))

    lane = lax.broadcasted_iota(jnp.int32, (1, LANES), 1)
    bound = jnp.full((1, LANES), NEG_BIG, F32)
    for hh in range(hb):
        h = pl.program_id(1) * hb + hh
        qf = qt_ref[0, hh, 0][:ATT_HEAD_DIM, :].astype(F32)
        qmax = jnp.sqrt(jnp.max(jnp.sum(qf * qf, axis=0, keepdims=True), axis=-1, keepdims=True))
        mmin = jnp.min(carry[hh][0], axis=-1, keepdims=True)
        bound = jnp.where(lane == h, qmax * kmax_ref[0, 0:1, :] - mmin, bound)
    chunk_id = lax.broadcasted_iota(jnp.int32, bpre_ref.shape[1:], 0)
    live = (bpre_ref[0] + bound >= -ATT_SKIP_EXPONENT) & (chunk_id < below)
    n_live = jnp.max(jnp.sum(jnp.where(live, 1.0, 0.0), axis=0, keepdims=True)).astype(jnp.int32)
    n_pairs = (n_live + 1) // 2

    def pair(w, carry):
        top = below - 2 * w
        carry = visit(0, carry, jnp.where(w == 0, below + nd - 1, top), top - 2)
        return visit(1, carry, top - 1, jnp.maximum(top - 3, 0))

    def two_pairs(w2, carry):
        return pair(2 * w2 + 1, pair(2 * w2, carry))

    carry = lax.fori_loop(0, n_pairs // 2, two_pairs, carry)
    carry = lax.fori_loop(2 * (n_pairs // 2), n_pairs, pair, carry)
    last_chunk = jnp.where(n_pairs == 0, below + nd - 1, below - 2 * n_pairs)
    pad_rows = jnp.zeros((LANES - ATT_V_ROWS, tq), F32)
    for pr in range(hb // 2):
        halves = []
        for hh in (2 * pr, 2 * pr + 1):
            _, _, alpha, acc = carry[hh]
            acc = weigh(hh, last_chunk, 1, alpha, acc)
            acc_t = jnp.concatenate([acc, pad_rows], axis=0).T
            halves.append(acc_t / acc_t[:, ATT_HEAD_DIM:ATT_HEAD_DIM + 1])
        both = jnp.where(lane < ATT_HEAD_DIM, halves[0],
                         pltpu.roll(halves[1], ATT_HEAD_DIM, axis=1))
        o_ref[0, :, pr * LANES:(pr + 1) * LANES] = both.astype(o_ref.dtype)


def _attention(qt, k, vt, kmax, bpre, tq, tk, hb):
    b, s, _ = k.shape
    assert tq % (2 * tk) == 0 and bpre.shape == (b, s // tk, LANES)
    return pl.pallas_call(
        functools.partial(_att_kernel, tq=tq, tk=tk, hb=hb, rb=ATT_ROW_BLOCK),
        out_shape=jax.ShapeDtypeStruct((b, s, ATT_WIDTH), BF16),
        grid=(b, ATT_HEADS // hb, s // tq),
        in_specs=[
            pl.BlockSpec((1, hb, 1, LANES, tq), lambda bi, hg, i: (bi, hg, i, 0, 0)),
            pl.BlockSpec((1, s, hb * LANES), lambda bi, hg, i: (bi, 0, hg),
                         pipeline_mode=pl.Buffered(1)),
            pl.BlockSpec((1, hb, s // tk, ATT_V_ROWS, tk), lambda bi, hg, i: (bi, hg, 0, 0, 0),
                         pipeline_mode=pl.Buffered(1)),
            pl.BlockSpec((1, SUBLANES, LANES), lambda bi, hg, i: (bi, 0, 0)),
            pl.BlockSpec((1, s // tk, LANES), lambda bi, hg, i: (bi, 0, 0)),
        ],
        out_specs=pl.BlockSpec((1, tq, hb * ATT_HEAD_DIM), lambda bi, hg, i: (bi, i, hg)),
        scratch_shapes=[pltpu.VMEM((hb, tk, tq), F32), pltpu.VMEM((hb, tk, tq), F32),
                        pltpu.VMEM((hb, tk, tq), BF16), pltpu.VMEM((hb, tk, tq), BF16)],
        compiler_params=_cparams(("parallel", "parallel", "arbitrary")),
        name="fox_attention",
    )(qt, k, vt, kmax, bpre)


def _mlstm_kernel(q_ref, k_ref, v_ref, gcol_ref, grow_ref, spread_ref, nw_ref, o_ref,
                  st_ref, m_ref):
    @pl.when(pl.program_id(1) == 0)
    def _():
        st_ref[...] = jnp.zeros_like(st_ref)
        m_ref[...] = jnp.zeros_like(m_ref)

    L = CHUNK
    D = ML_HEAD_DIM
    H = range(ML_HEADS)
    r = lax.broadcasted_iota(jnp.int32, (L, L), 0)
    c = lax.broadcasted_iota(jnp.int32, (L, L), 1)
    causal = c <= r
    ones = jnp.ones((L, D), BF16)
    mean_mat = jnp.full((D, D), 1.0 / D, BF16)
    two = lambda x: jnp.concatenate([x, x], axis=-1)
    head = lambda ref, h: ref[0, :, h * D:(h + 1) * D]

    rep = _dot(jnp.concatenate(_split3(gcol_ref[0]), axis=-1), spread_ref[...])
    col = lambda j, h: rep[:, (j * ML_HEADS + h) * LANES:(j * ML_HEADS + h + 1) * LANES]
    qk = [_dot_nt(head(q_ref, h), head(k_ref, h)) for h in H]
    st = [st_ref[h] for h in H]
    inter = [_dot(head(q_ref, h), st[h].astype(BF16)) for h in H]
    kt = [head(k_ref, h).astype(F32).T.astype(BF16) for h in H]

    m_t, w_intra, w_inter, decay, wv = [], [], [], [], []
    for h in H:
        b, gq, cm = col(0, h), col(1, h), col(2, h)
        gq_row = grow_ref[0, ML_HEADS + h:ML_HEADS + h + 1, :]
        m = m_ref[h]
        a = b + m
        m_t.append(jnp.maximum(a, b + cm))
        w_intra.append(jnp.exp(jnp.where(causal, b + gq_row, NEG_BIG) - m_t[h]))
        w_inter.append(jnp.exp(a - m_t[h]))
        b_last = b[L - 1:L, :]
        m_new = jnp.maximum(b_last + m, b_last + cm[L - 1:L, :])
        w_s = jnp.exp(b_last + gq - m_new)
        decay.append(jnp.exp(b_last + m - m_new))
        wv.append(jnp.concatenate([w_s * head(v_ref, h).astype(F32), w_s], axis=-1).astype(BF16))
        m_ref[h] = m_new

    qkw = [(qk[h] * w_intra[h]).astype(BF16) for h in H]
    both = [_dot(qkw[h], jnp.concatenate([head(v_ref, h), ones], axis=-1))
            + two(w_inter[h]) * inter[h] for h in H]
    for h in H:
        st_ref[h] = two(decay[h]) * st[h] + _dot(kt[h], wv[h])
    hh = [both[h][:, :D] / jnp.maximum(jnp.abs(both[h][:, D:]), jnp.exp(-m_t[h])) for h in H]
    ms = []
    for h in H:
        sq = hh[h] * hh[h]
        sq_hi = sq.astype(BF16)
        sq_lo = (sq - sq_hi.astype(F32)).astype(BF16)
        ms.append(_dot(sq_hi, mean_mat) + _dot(sq_lo, mean_mat))
    for h in H:
        o_ref[0, :, h * D:(h + 1) * D] = (hh[h] * lax.rsqrt(ms[h] + EPS)
                                          * nw_ref[:, h * D:(h + 1) * D])


def _mlstm(mq, mk, mv, gcol, grow, norm_w):
    b, s, _ = mq.shape
    nc = s // CHUNK
    row = lambda w: pl.BlockSpec((1, CHUNK, w), lambda bi, ci: (bi, ci, 0))
    src = lax.broadcasted_iota(jnp.int32, (3 * LANES, ML_REP), 0) % LANES
    dst = lax.broadcasted_iota(jnp.int32, (3 * LANES, ML_REP), 1) // LANES
    spread = (src == dst).astype(BF16)
    return pl.pallas_call(
        _mlstm_kernel,
        out_shape=jax.ShapeDtypeStruct((b, s, ML_WIDTH), F32),
        grid=(b, nc),
        in_specs=[
            row(ML_WIDTH), row(ML_WIDTH), row(ML_WIDTH), row(LANES),
            pl.BlockSpec((1, GATE_ROWS, CHUNK), lambda bi, ci: (bi, 0, ci)),
            _resident((3 * LANES, ML_REP)),
            _resident((1, ML_WIDTH)),
        ],
        out_specs=row(ML_WIDTH),
        scratch_shapes=[
            pltpu.VMEM((ML_HEADS, ML_HEAD_DIM, 2 * ML_HEAD_DIM), F32),
            pltpu.VMEM((ML_HEADS, 1, LANES), F32),
        ],
        compiler_params=_cparams(("parallel", "arbitrary")),
        name="mlstm",
    )(mq, mk, mv, gcol, grow, spread, norm_w)


def _gelu_tanh(x):
    return 0.5 * x * (1.0 + jnp.tanh(0.7978845608028654 * (x + 0.044715 * (x * x * x))))


def _merge_kernel(x_ref, yatt_ref, hm_ref, g_ref, wo_ref, wuv_ref, wgt_ref, ng_ref,
                  wsp_ref, bsp_ref, wba_ref, wbm_ref, wbg_ref, wout_ref, o_ref, *, tm):
    x = x_ref[...]
    hn = _rms(x, g_ref[...]).astype(BF16)

    y_ml = jax.nn.sigmoid(_dot(hn, wo_ref[...])) * hm_ref[...]

    uv = _gelu_tanh(_dot(hn, wuv_ref[...]))
    u = uv[:, :GM_WIDTH]
    vn = _rms(uv[:, GM_WIDTH:], ng_ref[...]).astype(BF16)
    r = lax.broadcasted_iota(jnp.int32, (tm, tm), 0)
    c = lax.broadcasted_iota(jnp.int32, (tm, tm), 1)
    keep = (c <= r) & ((r // CHUNK) == (c // CHUNK))
    mixed = []
    for gi in range(GM_GROUPS):
        w = jnp.where(keep, wsp_ref[gi], 0.0).astype(BF16)
        mixed.append(_dot(w, vn[:, gi * GM_GROUP_DIM:(gi + 1) * GM_GROUP_DIM]))
    y_gm = u * (jnp.concatenate(mixed, axis=-1) + bsp_ref[...])

    gates = jax.nn.sigmoid(_dot(hn, wgt_ref[...]))
    merged = (gates[:, :D_MODEL] * _dot(yatt_ref[...], wba_ref[...])
              + gates[:, D_MODEL:2 * D_MODEL] * _dot(y_ml.astype(BF16), wbm_ref[...])
              + gates[:, 2 * D_MODEL:] * _dot(y_gm.astype(BF16), wbg_ref[...]))
    o_ref[...] = x + _dot(merged.astype(BF16), wout_ref[...])


def _merge(x2, yatt, hm, g, wo, wuv, wgt, ng, wsp, bsp, wba, wbm, wbg, wout, tm):
    n = x2.shape[0]
    row = lambda w: pl.BlockSpec((tm, w), lambda i: (i, 0))
    return pl.pallas_call(
        functools.partial(_merge_kernel, tm=tm),
        out_shape=jax.ShapeDtypeStruct((n, D_MODEL), F32),
        grid=(n // tm,),
        in_specs=[
            row(D_MODEL), row(ATT_WIDTH), row(ML_WIDTH),
            _resident((1, D_MODEL)),
            _resident((D_MODEL, ML_WIDTH)),
            _resident((D_MODEL, 2 * GM_WIDTH)),
            _resident((D_MODEL, N_BRANCH * D_MODEL)),
            _resident((1, GM_WIDTH)),
            _resident((GM_GROUPS, tm, tm)),
            _resident((tm, GM_WIDTH)),
            _resident((ATT_WIDTH, D_MODEL)),
            _resident((ML_WIDTH, D_MODEL)),
            _resident((GM_WIDTH, D_MODEL)),
            _resident((D_MODEL, D_MODEL)),
        ],
        out_specs=row(D_MODEL),
        compiler_params=_cparams(("parallel",)),
        name="merge",
    )(x2, yatt, hm, g, wo, wuv, wgt, ng, wsp, bsp, wba, wbm, wbg, wout)


def _final_kernel(x_ref, g_ref, o_ref):
    o_ref[...] = _rms(x_ref[...], g_ref[...])


def _final_norm(x2, g, tm):
    n = x2.shape[0]
    return pl.pallas_call(
        _final_kernel,
        out_shape=jax.ShapeDtypeStruct((n, D_MODEL), F32),
        grid=(n // tm,),
        in_specs=[pl.BlockSpec((tm, D_MODEL), lambda i: (i, 0)), _resident((1, D_MODEL))],
        out_specs=pl.BlockSpec((tm, D_MODEL), lambda i: (i, 0)),
        compiler_params=_cparams(("parallel",)),
        name="final_norm",
    )(x2, g)


def _tile(n, pref):
    t = min(n, pref)
    assert n % t == 0, (n, t)
    return t


def _prepare_layer_params(p, tm_merge):
    depth = p["w_in"].shape[0]
    o_att = 3 * ATT_WIDTH
    o_mqk = o_att + ATT_HEADS
    o_mv = o_mqk + 2 * ML_WIDTH
    o_mi = o_mv + ML_WIDTH
    o_mf = o_mi + ML_HEADS
    o_mo = o_mf + ML_HEADS
    o_uv = o_mo + ML_WIDTH
    o_gt = o_uv + 2 * GM_WIDTH
    w_in = p["w_in"]
    w_small = jnp.concatenate(
        [w_in[:, :, o_att:o_mqk], w_in[:, :, o_mi:o_mf], w_in[:, :, o_mf:o_mo],
         jnp.zeros((depth, D_MODEL, LANES - GATE_ROWS), F32)], axis=-1)
    b_small = jnp.concatenate(
        [p["b_f_att"], p["b_i_ml"], p["b_f_ml"], jnp.zeros((depth, LANES - GATE_ROWS), F32)],
        axis=-1)[:, None, :]
    reps = tm_merge // CHUNK
    bf = lambda a: a.astype(BF16)
    row = lambda a: a[:, None, :]

    def slots(w, width):
        w = w.reshape(depth, D_MODEL, ATT_HEADS, ATT_HEAD_DIM)
        w = jnp.pad(w, ((0, 0), (0, 0), (0, 0), (0, width - ATT_HEAD_DIM)))
        return w.reshape(depth, D_MODEL, ATT_HEADS * width)

    w_q = slots(w_in[:, :, :ATT_WIDTH], LANES) * (ATT_HEAD_DIM ** -0.5 * LOG2E)
    w_k = slots(w_in[:, :, ATT_WIDTH:2 * ATT_WIDTH], LANES)
    w_v = slots(w_in[:, :, 2 * ATT_WIDTH:o_att], ATT_V_ROWS)
    return dict(
        norm_ffn1=row(p["norm_ffn1"]), ffn1_gate=bf(p["ffn1_gate"]), ffn1_up=bf(p["ffn1_up"]),
        ffn1_down=bf(p["ffn1_down"]),
        norm_mix=row(p["norm_mix"]),
        w_qt=bf(jnp.swapaxes(w_q, 1, 2)), w_k=bf(w_k), w_vt=bf(jnp.swapaxes(w_v, 1, 2)),
        w_small=bf(w_small), b_small=b_small,
        w_mqk=bf(w_in[:, :, o_mqk:o_mv]), w_mv=bf(w_in[:, :, o_mv:o_mi]),
        w_mo=bf(w_in[:, :, o_mo:o_uv]), w_uv=bf(w_in[:, :, o_uv:o_gt]), w_gt=bf(w_in[:, :, o_gt:]),
        conv_ml=p["conv_ml"], norm_ml_head=row(p["norm_ml_head"]), norm_gmlp=row(p["norm_gmlp"]),
        w_spatial=jnp.tile(p["w_spatial"], (1, 1, reps, reps)),
        b_spatial=jnp.tile(jnp.repeat(jnp.swapaxes(p["b_spatial"], 1, 2), GM_GROUP_DIM, axis=2),
                           (1, reps, 1)),
        w_br_att=bf(p["w_br_att"]), w_br_ml=bf(p["w_br_ml"]), w_br_gmlp=bf(p["w_br_gmlp"]),
        w_out=bf(p["w_out"]),
        norm_ffn2=row(p["norm_ffn2"]), ffn2_gate=bf(p["ffn2_gate"]), ffn2_up=bf(p["ffn2_up"]),
        ffn2_down=bf(p["ffn2_down"]),
    )


def _layer(x2, lp, b, s, tiles):
    n = b * s
    x2 = _ffn(x2, lp["norm_ffn1"], lp["ffn1_gate"], lp["ffn1_up"], lp["ffn1_down"], tiles["ffn"])
    qt, k, vt, small, mq, mk, mv = _inproj(
        x2.reshape(b, s, D_MODEL), lp["norm_mix"], lp["w_qt"], lp["w_k"], lp["w_vt"],
        lp["w_small"], lp["w_mqk"], lp["w_mv"], lp["conv_ml"], tiles["inproj"], tiles["att_q"],
        tiles["att_k"])
    gcol, grow, k, kmax, bpre = _gates(small, lp["b_small"], k, tiles["gates"], tiles["att_k"])
    bpre = bpre[:, :, :tiles["gates"] // tiles["att_k"], :].reshape(b, s // tiles["att_k"], LANES)
    yatt = _attention(qt, k, vt, kmax, bpre, tiles["att_q"], tiles["att_k"], tiles["att_heads"])
    hm = _mlstm(mq, mk, mv, gcol, grow, lp["norm_ml_head"])
    x2 = _merge(x2, yatt.reshape(n, ATT_WIDTH), hm.reshape(n, ML_WIDTH), lp["norm_mix"],
                lp["w_mo"], lp["w_uv"], lp["w_gt"], lp["norm_gmlp"], lp["w_spatial"],
                lp["b_spatial"], lp["w_br_att"], lp["w_br_ml"], lp["w_br_gmlp"], lp["w_out"],
                tiles["merge"])
    return _ffn(x2, lp["norm_ffn2"], lp["ffn2_gate"], lp["ffn2_up"], lp["ffn2_down"], tiles["ffn"])


def _tiles_for(n, s):
    att_q = _tile(s, 512)
    return dict(ffn=_tile(n, 512), inproj=att_q, gates=_tile(s, 512), att_q=att_q,
                att_k=min(att_q // 2, 256), att_heads=2, merge=_tile(s, 256), final=_tile(n, 1024))


def _trunk(x, params, norm_final):
    b, s, _ = x.shape
    n = b * s
    tiles = _tiles_for(n, s)
    stacked = _prepare_layer_params(params, tiles["merge"])

    def body(x2, lp):
        return _layer(x2, lp, b, s, tiles), None

    x2, _ = lax.scan(body, x.reshape(n, D_MODEL), stacked)
    return _final_norm(x2, norm_final[None, :], tiles["final"]).reshape(b, s, D_MODEL)


def kernel(x, norm_ffn1, ffn1_gate, ffn1_up, ffn1_down, norm_mix, w_in, b_f_att, b_i_ml, b_f_ml, conv_ml, norm_ml_head, norm_gmlp, w_spatial, b_spatial, w_br_att, w_br_ml, w_br_gmlp, w_out, norm_ffn2, ffn2_gate, ffn2_up, ffn2_down, norm_final):
    params = dict(norm_ffn1=norm_ffn1, ffn1_gate=ffn1_gate, ffn1_up=ffn1_up, ffn1_down=ffn1_down,
                  norm_mix=norm_mix, w_in=w_in, b_f_att=b_f_att, b_i_ml=b_i_ml, b_f_ml=b_f_ml,
                  conv_ml=conv_ml, norm_ml_head=norm_ml_head, norm_gmlp=norm_gmlp,
                  w_spatial=w_spatial, b_spatial=b_spatial, w_br_att=w_br_att, w_br_ml=w_br_ml,
                  w_br_gmlp=w_br_gmlp, w_out=w_out, norm_ffn2=norm_ffn2, ffn2_gate=ffn2_gate,
                  ffn2_up=ffn2_up, ffn2_down=ffn2_down)
    return _trunk(x, params, norm_final)
```

```python
import functools

import jax
import jax.numpy as jnp
from jax import lax
from jax.experimental import pallas as pl
from jax.experimental.pallas import tpu as pltpu

D_MODEL = 1024
ATT_HEADS = 8
ATT_HEAD_DIM = 64
ATT_WIDTH = ATT_HEADS * ATT_HEAD_DIM
ML_HEADS = 4
ML_HEAD_DIM = 128
ML_WIDTH = ML_HEADS * ML_HEAD_DIM
CONV_WIDTH = 4
GM_GROUPS = 4
GM_GROUP_DIM = 128
GM_WIDTH = GM_GROUPS * GM_GROUP_DIM
CHUNK = 128
D_FF = 2816
FFN_RES = 0.5
N_BRANCH = 3
EPS = 1e-6

LANES = 128
SUBLANES = 8
ATT_SLOTS = ATT_HEADS * LANES
ATT_ROW_BLOCK = 32
ATT_V_ROWS = 80
LOG2E = 1.4426950408889634
ATT_SKIP_EXPONENT = 160.0
VMEM_LIMIT = 56 * 1024 * 1024
NEG_BIG = -1e30

GATE_ATT_F = 0
GATE_ML_I = ATT_HEADS
GATE_ML_F = ATT_HEADS + ML_HEADS
GATE_ROWS = 16
ML_REP = 3 * ML_HEADS * 128

F32 = jnp.float32
BF16 = jnp.bfloat16


def _cparams(sem):
    return pltpu.CompilerParams(dimension_semantics=sem, vmem_limit_bytes=VMEM_LIMIT)


def _resident(shape, layer=None):
    nd = len(shape)
    if layer is None:
        return pl.BlockSpec(shape, lambda *_: (0,) * nd, pipeline_mode=pl.Buffered(1))
    return pl.BlockSpec((None,) + tuple(shape), lambda *_: (layer,) + (0,) * nd,
                        pipeline_mode=pl.Buffered(1))


def _rms(x, g):
    ms = jnp.mean(x * x, axis=-1, keepdims=True)
    return x * lax.rsqrt(ms + EPS) * g


def _log_sigmoid(z):
    return jnp.minimum(z, 0.0) - jnp.log1p(jnp.exp(-jnp.abs(z)))


def _dot(a, b):
    return jnp.dot(a, b, preferred_element_type=F32)


def _dot_nt(a, b):
    return lax.dot_general(a, b, (((1,), (1,)), ((), ())), preferred_element_type=F32)


def _ffn_kernel(x_ref, g_ref, wg_ref, wu_ref, wd_ref, *rest, final):
    o_ref = rest[-1]
    x = x_ref[...]
    hn = _rms(x, g_ref[...]).astype(BF16)
    gate = _dot(hn, wg_ref[...])
    up = _dot(hn, wu_ref[...])
    act = (gate * jax.nn.sigmoid(gate) * up).astype(BF16)
    y = x + FFN_RES * _dot(act, wd_ref[...])
    o_ref[...] = _rms(y, rest[0][...]) if final else y


def _ffn(x2, g, wg, wu, wd, l, tm, final_g=None):
    n = x2.shape[0]
    final = final_g is not None
    return pl.pallas_call(
        functools.partial(_ffn_kernel, final=final),
        out_shape=jax.ShapeDtypeStruct((n, D_MODEL), F32),
        grid=(n // tm,),
        in_specs=[
            pl.BlockSpec((tm, D_MODEL), lambda i: (i, 0)),
            _resident((1, D_MODEL), l),
            _resident((D_MODEL, D_FF), l),
            _resident((D_MODEL, D_FF), l),
            _resident((D_FF, D_MODEL), l),
        ] + ([_resident((1, D_MODEL))] if final else []),
        out_specs=pl.BlockSpec((tm, D_MODEL), lambda i: (i, 0)),
        compiler_params=_cparams(("parallel",)),
        name="ffn_final" if final else "ffn",
    )(x2, g, wg, wu, wd, *([final_g] if final else []))


def _inproj_kernel(x_ref, xp_ref, g_ref, wqt_ref, wk_ref, wvt_ref, wsm_ref, wqk_ref, wv_ref,
                   cw_ref, qt_ref, k_ref, vt_ref, sm_ref, mq_ref, mk_ref, mv_ref, *, tm, tq, tk):
    hn = _rms(x_ref[0], g_ref[...])
    hb = hn.astype(BF16)
    hnt = hn.T.astype(BF16)
    q_row = lax.broadcasted_iota(jnp.int32, (ATT_SLOTS, tm), 0) % LANES
    qt = _dot(wqt_ref[...], hnt)
    qt = jnp.where((q_row >= ATT_HEAD_DIM) & (q_row < ATT_HEAD_DIM + 3), 1.0, qt).astype(BF16)
    v_row = lax.broadcasted_iota(jnp.int32, (ATT_HEADS * ATT_V_ROWS, tm), 0) % ATT_V_ROWS
    vt = _dot(wvt_ref[...], hnt)
    vt = jnp.where(v_row == ATT_HEAD_DIM, 1.0, vt).astype(BF16)
    for c in range(tm // tq):
        qt_ref[0, :, c] = qt[:, c * tq:(c + 1) * tq].reshape(ATT_HEADS, LANES, tq)
    for c in range(tm // tk):
        vt_ref[0, :, c] = vt[:, c * tk:(c + 1) * tk].reshape(ATT_HEADS, ATT_V_ROWS, tk)
    k_ref[0] = _dot(hb, wk_ref[...]).astype(BF16)
    sm_ref[0] = _dot(hb, wsm_ref[...])
    mv_ref[0] = _dot(hb, wv_ref[...]).astype(BF16)

    proj = _dot(hb, wqk_ref[...])
    before = _dot(_rms(xp_ref[0], g_ref[...]).astype(BF16), wqk_ref[...])
    before = jnp.where(pl.program_id(1) > 0, before, 0.0)
    xcat = jnp.concatenate([before, proj], axis=0)
    cw = cw_ref[...]
    conv = proj * cw[CONV_WIDTH - 1:CONV_WIDTH, :]
    for j in range(CONV_WIDTH - 1):
        off = SUBLANES - (CONV_WIDTH - 1) + j
        conv = conv + xcat[off:off + tm, :] * cw[j:j + 1, :]
    act = conv * jax.nn.sigmoid(conv)
    mq_ref[0] = (act[:, :ML_WIDTH] * (ML_HEAD_DIM ** -0.5)).astype(BF16)
    mk_ref[0] = act[:, ML_WIDTH:].astype(BF16)


def _inproj(x3, g, wqt, wk, wvt, wsm, wqk, wv, cw, l, tm, tq, tk):
    b, s, _ = x3.shape
    row = lambda w: pl.BlockSpec((1, tm, w), lambda bi, si: (bi, si, 0))
    rows_before = pl.BlockSpec((1, SUBLANES, D_MODEL),
                               lambda bi, si: (bi, jnp.maximum(si * (tm // SUBLANES) - 1, 0), 0))
    tr = lambda rows, t: pl.BlockSpec((1, ATT_HEADS, tm // t, rows, t),
                                      lambda bi, si: (bi, 0, si, 0, 0))
    tr_shape = lambda rows, t: jax.ShapeDtypeStruct((b, ATT_HEADS, s // t, rows, t), BF16)
    return pl.pallas_call(
        functools.partial(_inproj_kernel, tm=tm, tq=tq, tk=tk),
        out_shape=(
            tr_shape(LANES, tq),
            jax.ShapeDtypeStruct((b, s, ATT_SLOTS), BF16),
            tr_shape(ATT_V_ROWS, tk),
            jax.ShapeDtypeStruct((b, s, LANES), F32),
            jax.ShapeDtypeStruct((b, s, ML_WIDTH), BF16),
            jax.ShapeDtypeStruct((b, s, ML_WIDTH), BF16),
            jax.ShapeDtypeStruct((b, s, ML_WIDTH), BF16),
        ),
        grid=(b, s // tm),
        in_specs=[
            row(D_MODEL),
            rows_before,
            _resident((1, D_MODEL), l),
            _resident((ATT_SLOTS, D_MODEL), l),
            _resident((D_MODEL, ATT_SLOTS), l),
            _resident((ATT_HEADS * ATT_V_ROWS, D_MODEL), l),
            _resident((D_MODEL, LANES), l),
            _resident((D_MODEL, 2 * ML_WIDTH), l),
            _resident((D_MODEL, ML_WIDTH), l),
            _resident((CONV_WIDTH, 2 * ML_WIDTH), l),
        ],
        out_specs=(tr(LANES, tq), row(ATT_SLOTS), tr(ATT_V_ROWS, tk), row(LANES),
                   row(ML_WIDTH), row(ML_WIDTH), row(ML_WIDTH)),
        compiler_params=_cparams(("parallel", "parallel")),
        name="inproj",
    )(x3, x3, g, wqt, wk, wvt, wsm, wqk, wv, cw)


def _split3(x):
    hi = x.astype(BF16)
    r1 = x - hi.astype(F32)
    mid = r1.astype(BF16)
    lo = (r1 - mid.astype(F32)).astype(BF16)
    return hi, mid, lo


def _place(x, moves):
    r = lax.broadcasted_iota(jnp.int32, (LANES, LANES), 0)
    c = lax.broadcasted_iota(jnp.int32, (LANES, LANES), 1)
    mat = jnp.zeros((LANES, LANES), F32)
    for src, dst, sign in moves:
        mat = jnp.where((r == src) & (c == dst), sign, mat)
    mat3 = jnp.concatenate([mat.astype(BF16)] * 3, axis=0)
    return _dot(jnp.concatenate(_split3(x), axis=-1), mat3)


def _gates_kernel(sm_ref, bias_ref, k_ref, col_ref, row_ref, ka_ref, kmax_ref, bpre_ref,
                  carry_ref, bcarry_ref, *, ts, tk):
    @pl.when(pl.program_id(1) == 0)
    def _():
        carry_ref[...] = jnp.zeros_like(carry_ref)
        bcarry_ref[...] = jnp.full_like(bcarry_ref, NEG_BIG)
        kmax_ref[...] = jnp.zeros_like(kmax_ref)

    z = sm_ref[0] + bias_ref[...]
    lane = lax.broadcasted_iota(jnp.int32, z.shape, 1)
    is_att = lane < GATE_ML_I
    is_mlf = (lane >= GATE_ML_F) & (lane < GATE_ML_F + ML_HEADS)
    vals = jnp.where(is_att | is_mlf, _log_sigmoid(z), z)

    r = lax.broadcasted_iota(jnp.int32, (ts, ts), 0)
    c = lax.broadcasted_iota(jnp.int32, (ts, ts), 1)
    tri = c <= r
    tri_full = jnp.where(tri, 1.0, 0.0).astype(BF16)
    tri_chunk = jnp.where(tri & ((r // CHUNK) == (c // CHUNK)), 1.0, 0.0).astype(BF16)
    pieces = jnp.concatenate(_split3(vals), axis=-1)
    add3 = lambda y: y[:, :LANES] + y[:, LANES:2 * LANES] + y[:, 2 * LANES:]
    cs_full = add3(_dot(tri_full, pieces)) + carry_ref[...]
    cs_chunk = add3(_dot(tri_chunk, pieces))
    carry_ref[...] = cs_full[ts - 1:ts, :]

    li_moves, b_moves = [], []
    for h in range(ML_HEADS):
        li_moves += [(GATE_ML_I + h, ML_HEADS + h, 1.0), (GATE_ML_I + h, 2 * ML_HEADS + h, 1.0)]
        b_moves += [(GATE_ML_F + h, h, 1.0), (GATE_ML_F + h, ML_HEADS + h, -1.0),
                    (GATE_ML_F + h, 2 * ML_HEADS + h, -1.0)]
    g = _place(vals, li_moves) + _place(cs_chunk, b_moves)
    pos = lax.broadcasted_iota(jnp.int32, (ts, LANES), 0) % CHUNK
    cmax = g
    d = 1
    while d < CHUNK:
        cmax = jnp.where(pos >= d, jnp.maximum(cmax, pltpu.roll(cmax, d, axis=0)), cmax)
        d *= 2
    g = jnp.where(lane >= 2 * ML_HEADS, cmax, g)
    col_ref[0] = g
    row_ref[0] = g.T[:GATE_ROWS, :]

    src = lax.broadcasted_iota(jnp.int32, (3 * LANES, ATT_SLOTS), 0)
    dst = lax.broadcasted_iota(jnp.int32, (3 * LANES, ATT_SLOTS), 1)
    place = (src % LANES < ATT_HEADS) & (dst == (src % LANES) * LANES + ATT_HEAD_DIM + src // LANES)
    decay = cs_full * -LOG2E
    decay_cols = _dot(jnp.concatenate(_split3(decay), axis=-1),
                      jnp.where(place, 1.0, 0.0).astype(BF16))
    kf = k_ref[0].astype(F32)
    ka_ref[0] = (kf + decay_cols).astype(BF16)

    lane_row = lax.broadcasted_iota(jnp.int32, (1, LANES), 1)
    knorm = jnp.zeros((1, LANES), F32)
    for h in range(ATT_HEADS):
        kh = kf[:, h * LANES:(h + 1) * LANES]
        sq = jnp.max(jnp.sum(kh * kh, axis=-1, keepdims=True), axis=0, keepdims=True)
        knorm = jnp.where(lane_row == h, jnp.sqrt(sq), knorm)
    kmax_ref[0] = jnp.maximum(kmax_ref[0], jnp.broadcast_to(knorm, (SUBLANES, LANES)))

    rows = []
    running = bcarry_ref[...]
    for c in range(ts // tk):
        running = jnp.maximum(running, jnp.max(decay[c * tk:(c + 1) * tk, :], axis=0, keepdims=True))
        rows.append(running)
    bcarry_ref[...] = running
    rows += [running] * (SUBLANES - len(rows))
    bpre_ref[0, 0] = jnp.concatenate(rows, axis=0)


def _gates(small, bias, k, l, ts, tk):
    b, s, _ = small.shape
    assert ts % tk == 0 and ts // tk <= SUBLANES
    row = lambda w: pl.BlockSpec((1, ts, w), lambda bi, si: (bi, si, 0))
    return pl.pallas_call(
        functools.partial(_gates_kernel, ts=ts, tk=tk),
        out_shape=(jax.ShapeDtypeStruct((b, s, LANES), F32),
                   jax.ShapeDtypeStruct((b, GATE_ROWS, s), F32),
                   jax.ShapeDtypeStruct((b, s, ATT_SLOTS), BF16),
                   jax.ShapeDtypeStruct((b, SUBLANES, LANES), F32),
                   jax.ShapeDtypeStruct((b, s // ts, SUBLANES, LANES), F32)),
        grid=(b, s // ts),
        in_specs=[row(LANES), _resident((1, LANES), l), row(ATT_SLOTS)],
        out_specs=(row(LANES),
                   pl.BlockSpec((1, GATE_ROWS, ts), lambda bi, si: (bi, 0, si)),
                   row(ATT_SLOTS),
                   pl.BlockSpec((1, SUBLANES, LANES), lambda bi, si: (bi, 0, 0)),
                   pl.BlockSpec((1, 1, SUBLANES, LANES), lambda bi, si: (bi, si, 0, 0))),
        scratch_shapes=[pltpu.VMEM((1, LANES), F32), pltpu.VMEM((1, LANES), F32)],
        compiler_params=_cparams(("parallel", "arbitrary")),
        name="gates",
    )(small, bias, k)


def _att_kernel(qt_ref, k_ref, vt_ref, kmax_ref, bpre_ref, o_ref, s0_ref, s1_ref, p0_ref, p1_ref,
                *, tq, tk, hb, rb):
    i = pl.program_id(2)
    nd = tq // tk
    s_refs = (s0_ref, s1_ref)
    p_refs = (p0_ref, p1_ref)

    def scores(hh, chunk):
        kj = k_ref[0, pl.ds(pl.multiple_of(chunk * tk, tk), tk), hh * LANES:(hh + 1) * LANES]
        return _dot(kj, qt_ref[0, hh, 0])

    def weigh(hh, chunk, par, alpha, acc):
        return acc * alpha + _dot(vt_ref[0, hh, chunk], p_refs[par][hh])

    def visit(par, carry, pv_chunk, next_chunk, next_mask=None):
        oth = 1 - par
        out = []
        for hh in range(hb):
            m, cmax, alpha_prev, acc = carry[hh]
            acc = weigh(hh, pv_chunk, oth, alpha_prev, acc)
            m_new = jnp.maximum(m, cmax)
            alpha = jnp.exp2(m - m_new)
            for r in range(0, tk, rb):
                p_refs[par][hh, r:r + rb, :] = jnp.exp2(
                    s_refs[par][hh, r:r + rb, :] - m_new).astype(BF16)
            s_next = scores(hh, next_chunk)
            if next_mask is not None:
                s_next = jnp.where(next_mask, s_next, NEG_BIG)
            s_refs[oth][hh] = s_next
            out.append((m_new, jnp.max(s_next, axis=0, keepdims=True), alpha, acc))
        return tuple(out)

    key = lax.broadcasted_iota(jnp.int32, (tk, tq), 0)
    qry = lax.broadcasted_iota(jnp.int32, (tk, tq), 1)
    init = []
    for hh in range(hb):
        s_first = jnp.where(key <= qry, scores(hh, nd * i), NEG_BIG)
        s0_ref[hh] = s_first
        p1_ref[hh] = jnp.zeros((tk, tq), BF16)
        init.append((jnp.full((1, tq), NEG_BIG, F32), jnp.max(s_first, axis=0, keepdims=True),
                     jnp.ones((1, tq), F32), jnp.zeros((ATT_V_ROWS, tq), F32)))
    carry = tuple(init)
    below = nd * i
    for d in range(nd):
        pv_chunk = below + d - 1 if d else 0
        if d + 1 < nd:
            carry = visit(d % 2, carry, pv_chunk, below + d + 1, key + (d + 1) * tk <= qry)
        else:
            carry = visit(d % 2, carry, pv_chunk, jnp.maximum(below - 1, 0))

    lane = lax.broadcasted_iota(jnp.int32, (1, LANES), 1)
    bound = jnp.full((1, LANES), NEG_BIG, F32)
    for hh in range(hb):
        h = pl.program_id(1) * hb + hh
        qf = qt_ref[0, hh, 0][:ATT_HEAD_DIM, :].astype(F32)
        qmax = jnp.sqrt(jnp.max(jnp.sum(qf * qf, axis=0, keepdims=True), axis=-1, keepdims=True))
        mmin = jnp.min(carry[hh][0], axis=-1, keepdims=True)
        bound = jnp.where(lane == h, qmax * kmax_ref[0, 0:1, :] - mmin, bound)
    chunk_id = lax.broadcasted_iota(jnp.int32, bpre_ref.shape[1:], 0)
    live = (bpre_ref[0] + bound >= -ATT_SKIP_EXPONENT) & (chunk_id < below)
    n_live = jnp.max(jnp.sum(jnp.where(live, 1.0, 0.0), axis=0, keepdims=True)).astype(jnp.int32)
    n_pairs = (n_live + 1) // 2

    def pair(w, carry):
        top = below - 2 * w
        carry = visit(0, carry, jnp.where(w == 0, below + nd - 1, top), top - 2)
        return visit(1, carry, top - 1, jnp.maximum(top - 3, 0))

    def two_pairs(w2, carry):
        return pair(2 * w2 + 1, pair(2 * w2, carry))

    carry = lax.fori_loop(0, n_pairs // 2, two_pairs, carry)
    carry = lax.fori_loop(2 * (n_pairs // 2), n_pairs, pair, carry)
    last_chunk = jnp.where(n_pairs == 0, below + nd - 1, below - 2 * n_pairs)
    pad_rows = jnp.zeros((LANES - ATT_V_ROWS, tq), F32)
    for pr in range(hb // 2):
        halves = []
        for hh in (2 * pr, 2 * pr + 1):
            _, _, alpha, acc = carry[hh]
            acc = weigh(hh, last_chunk, 1, alpha, acc)
            acc_t = jnp.concatenate([acc, pad_rows], axis=0).T
            halves.append(acc_t / acc_t[:, ATT_HEAD_DIM:ATT_HEAD_DIM + 1])
        both = jnp.where(lane < ATT_HEAD_DIM, halves[0],
                         pltpu.roll(halves[1], ATT_HEAD_DIM, axis=1))
        o_ref[0, :, pr * LANES:(pr + 1) * LANES] = both.astype(o_ref.dtype)


def _attention(qt, k, vt, kmax, bpre, tq, tk, hb):
    b, s, _ = k.shape
    assert tq % (2 * tk) == 0 and bpre.shape == (b, s // tk, LANES)
    return pl.pallas_call(
        functools.partial(_att_kernel, tq=tq, tk=tk, hb=hb, rb=ATT_ROW_BLOCK),
        out_shape=jax.ShapeDtypeStruct((b, s, ATT_WIDTH), BF16),
        grid=(b, ATT_HEADS // hb, s // tq),
        in_specs=[
            pl.BlockSpec((1, hb, 1, LANES, tq), lambda bi, hg, i: (bi, hg, i, 0, 0)),
            pl.BlockSpec((1, s, hb * LANES), lambda bi, hg, i: (bi, 0, hg),
                         pipeline_mode=pl.Buffered(1)),
            pl.BlockSpec((1, hb, s // tk, ATT_V_ROWS, tk), lambda bi, hg, i: (bi, hg, 0, 0, 0),
                         pipeline_mode=pl.Buffered(1)),
            pl.BlockSpec((1, SUBLANES, LANES), lambda bi, hg, i: (bi, 0, 0)),
            pl.BlockSpec((1, s // tk, LANES), lambda bi, hg, i: (bi, 0, 0)),
        ],
        out_specs=pl.BlockSpec((1, tq, hb * ATT_HEAD_DIM), lambda bi, hg, i: (bi, i, hg)),
        scratch_shapes=[pltpu.VMEM((hb, tk, tq), F32), pltpu.VMEM((hb, tk, tq), F32),
                        pltpu.VMEM((hb, tk, tq), BF16), pltpu.VMEM((hb, tk, tq), BF16)],
        compiler_params=_cparams(("parallel", "parallel", "arbitrary")),
        name="fox_attention",
    )(qt, k, vt, kmax, bpre)


def _mlstm_kernel(q_ref, k_ref, v_ref, gcol_ref, grow_ref, spread_ref, nw_ref, o_ref,
                  st_ref, m_ref):
    @pl.when(pl.program_id(1) == 0)
    def _():
        st_ref[...] = jnp.zeros_like(st_ref)
        m_ref[...] = jnp.zeros_like(m_ref)

    L = CHUNK
    D = ML_HEAD_DIM
    H = range(ML_HEADS)
    r = lax.broadcasted_iota(jnp.int32, (L, L), 0)
    c = lax.broadcasted_iota(jnp.int32, (L, L), 1)
    causal = c <= r
    ones = jnp.ones((L, D), BF16)
    mean_mat = jnp.full((D, D), 1.0 / D, BF16)
    two = lambda x: jnp.concatenate([x, x], axis=-1)
    head = lambda ref, h: ref[0, :, h * D:(h + 1) * D]

    rep = _dot(jnp.concatenate(_split3(gcol_ref[0]), axis=-1), spread_ref[...])
    col = lambda j, h: rep[:, (j * ML_HEADS + h) * LANES:(j * ML_HEADS + h + 1) * LANES]
    qk = [_dot_nt(head(q_ref, h), head(k_ref, h)) for h in H]
    st = [st_ref[h] for h in H]
    inter = [_dot(head(q_ref, h), st[h].astype(BF16)) for h in H]
    kt = [head(k_ref, h).astype(F32).T.astype(BF16) for h in H]

    m_t, w_intra, w_inter, decay, wv = [], [], [], [], []
    for h in H:
        b, gq, cm = col(0, h), col(1, h), col(2, h)
        gq_row = grow_ref[0, ML_HEADS + h:ML_HEADS + h + 1, :]
        m = m_ref[h]
        a = b + m
        m_t.append(jnp.maximum(a, b + cm))
        w_intra.append(jnp.exp(jnp.where(causal, b + gq_row, NEG_BIG) - m_t[h]))
        w_inter.append(jnp.exp(a - m_t[h]))
        b_last = b[L - 1:L, :]
        m_new = jnp.maximum(b_last + m, b_last + cm[L - 1:L, :])
        w_s = jnp.exp(b_last + gq - m_new)
        decay.append(jnp.exp(b_last + m - m_new))
        wv.append(jnp.concatenate([w_s * head(v_ref, h).astype(F32), w_s], axis=-1).astype(BF16))
        m_ref[h] = m_new

    qkw = [(qk[h] * w_intra[h]).astype(BF16) for h in H]
    both = [_dot(qkw[h], jnp.concatenate([head(v_ref, h), ones], axis=-1))
            + two(w_inter[h]) * inter[h] for h in H]
    for h in H:
        st_ref[h] = two(decay[h]) * st[h] + _dot(kt[h], wv[h])
    hh = [both[h][:, :D] / jnp.maximum(jnp.abs(both[h][:, D:]), jnp.exp(-m_t[h])) for h in H]
    ms = []
    for h in H:
        sq = hh[h] * hh[h]
        sq_hi = sq.astype(BF16)
        sq_lo = (sq - sq_hi.astype(F32)).astype(BF16)
        ms.append(_dot(sq_hi, mean_mat) + _dot(sq_lo, mean_mat))
    for h in H:
        o_ref[0, :, h * D:(h + 1) * D] = (hh[h] * lax.rsqrt(ms[h] + EPS)
                                          * nw_ref[:, h * D:(h + 1) * D])


def _mlstm(mq, mk, mv, gcol, grow, norm_w, l):
    b, s, _ = mq.shape
    nc = s // CHUNK
    row = lambda w: pl.BlockSpec((1, CHUNK, w), lambda bi, ci: (bi, ci, 0))
    src = lax.broadcasted_iota(jnp.int32, (3 * LANES, ML_REP), 0) % LANES
    dst = lax.broadcasted_iota(jnp.int32, (3 * LANES, ML_REP), 1) // LANES
    spread = (src == dst).astype(BF16)
    return pl.pallas_call(
        _mlstm_kernel,
        out_shape=jax.ShapeDtypeStruct((b, s, ML_WIDTH), F32),
        grid=(b, nc),
        in_specs=[
            row(ML_WIDTH), row(ML_WIDTH), row(ML_WIDTH), row(LANES),
            pl.BlockSpec((1, GATE_ROWS, CHUNK), lambda bi, ci: (bi, 0, ci)),
            _resident((3 * LANES, ML_REP)),
            _resident((1, ML_WIDTH), l),
        ],
        out_specs=row(ML_WIDTH),
        scratch_shapes=[
            pltpu.VMEM((ML_HEADS, ML_HEAD_DIM, 2 * ML_HEAD_DIM), F32),
            pltpu.VMEM((ML_HEADS, 1, LANES), F32),
        ],
        compiler_params=_cparams(("parallel", "arbitrary")),
        name="mlstm",
    )(mq, mk, mv, gcol, grow, spread, norm_w)


def _gelu_tanh(x):
    return 0.5 * x * (1.0 + jnp.tanh(0.7978845608028654 * (x + 0.044715 * (x * x * x))))


def _merge_kernel(x_ref, yatt_ref, hm_ref, g_ref, wo_ref, wuv_ref, wgt_ref, ng_ref,
                  wsp_ref, bsp_ref, wba_ref, wbm_ref, wbg_ref, wout_ref, o_ref, *, tm):
    x = x_ref[...]
    hn = _rms(x, g_ref[...]).astype(BF16)

    y_ml = jax.nn.sigmoid(_dot(hn, wo_ref[...])) * hm_ref[...]

    uv = _gelu_tanh(_dot(hn, wuv_ref[...]))
    u = uv[:, :GM_WIDTH]
    vn = _rms(uv[:, GM_WIDTH:], ng_ref[...]).astype(BF16)
    r = lax.broadcasted_iota(jnp.int32, (CHUNK, CHUNK), 0)
    c = lax.broadcasted_iota(jnp.int32, (CHUNK, CHUNK), 1)
    mixed = []
    for gi in range(GM_GROUPS):
        w = jnp.where(c <= r, wsp_ref[gi], 0.0).astype(BF16)
        lanes = slice(gi * GM_GROUP_DIM, (gi + 1) * GM_GROUP_DIM)
        mixed.append(jnp.concatenate(
            [_dot(w, vn[ci * CHUNK:(ci + 1) * CHUNK, lanes]) for ci in range(tm // CHUNK)], axis=0))
    y_gm = u * (jnp.concatenate(mixed, axis=-1) + bsp_ref[...])

    gates = jax.nn.sigmoid(_dot(hn, wgt_ref[...]))
    merged = (gates[:, :D_MODEL] * _dot(yatt_ref[...], wba_ref[...])
              + gates[:, D_MODEL:2 * D_MODEL] * _dot(y_ml.astype(BF16), wbm_ref[...])
              + gates[:, 2 * D_MODEL:] * _dot(y_gm.astype(BF16), wbg_ref[...]))
    o_ref[...] = x + _dot(merged.astype(BF16), wout_ref[...])


def _merge(x2, yatt, hm, g, wo, wuv, wgt, ng, wsp, bsp, wba, wbm, wbg, wout, l, tm):
    n = x2.shape[0]
    row = lambda w: pl.BlockSpec((tm, w), lambda i: (i, 0))
    return pl.pallas_call(
        functools.partial(_merge_kernel, tm=tm),
        out_shape=jax.ShapeDtypeStruct((n, D_MODEL), F32),
        grid=(n // tm,),
        in_specs=[
            row(D_MODEL), row(ATT_WIDTH), row(ML_WIDTH),
            _resident((1, D_MODEL), l),
            _resident((D_MODEL, ML_WIDTH), l),
            _resident((D_MODEL, 2 * GM_WIDTH), l),
            _resident((D_MODEL, N_BRANCH * D_MODEL), l),
            _resident((1, GM_WIDTH), l),
            _resident((GM_GROUPS, CHUNK, CHUNK), l),
            _resident((tm, GM_WIDTH), l),
            _resident((ATT_WIDTH, D_MODEL), l),
            _resident((ML_WIDTH, D_MODEL), l),
            _resident((GM_WIDTH, D_MODEL), l),
            _resident((D_MODEL, D_MODEL), l),
        ],
        out_specs=row(D_MODEL),
        compiler_params=_cparams(("parallel",)),
        name="merge",
    )(x2, yatt, hm, g, wo, wuv, wgt, ng, wsp, bsp, wba, wbm, wbg, wout)


def _tile(n, pref):
    t = min(n, pref)
    assert n % t == 0, (n, t)
    return t


def _prepare_layer_params(p, tm_merge):
    depth = p["w_in"].shape[0]
    o_att = 3 * ATT_WIDTH
    o_mqk = o_att + ATT_HEADS
    o_mv = o_mqk + 2 * ML_WIDTH
    o_mi = o_mv + ML_WIDTH
    o_mf = o_mi + ML_HEADS
    o_mo = o_mf + ML_HEADS
    o_uv = o_mo + ML_WIDTH
    o_gt = o_uv + 2 * GM_WIDTH
    w_in = p["w_in"]
    w_small = jnp.concatenate(
        [w_in[:, :, o_att:o_mqk], w_in[:, :, o_mi:o_mf], w_in[:, :, o_mf:o_mo],
         jnp.zeros((depth, D_MODEL, LANES - GATE_ROWS), F32)], axis=-1)
    b_small = jnp.concatenate(
        [p["b_f_att"], p["b_i_ml"], p["b_f_ml"], jnp.zeros((depth, LANES - GATE_ROWS), F32)],
        axis=-1)[:, None, :]
    reps = tm_merge // CHUNK
    bf = lambda a: a.astype(BF16)
    row = lambda a: a[:, None, :]

    def slots(w, width):
        w = w.reshape(depth, D_MODEL, ATT_HEADS, ATT_HEAD_DIM)
        w = jnp.pad(w, ((0, 0), (0, 0), (0, 0), (0, width - ATT_HEAD_DIM)))
        return w.reshape(depth, D_MODEL, ATT_HEADS * width)

    w_q = slots(w_in[:, :, :ATT_WIDTH], LANES) * (ATT_HEAD_DIM ** -0.5 * LOG2E)
    w_k = slots(w_in[:, :, ATT_WIDTH:2 * ATT_WIDTH], LANES)
    w_v = slots(w_in[:, :, 2 * ATT_WIDTH:o_att], ATT_V_ROWS)
    return dict(
        norm_ffn1=row(p["norm_ffn1"]), ffn1_gate=bf(p["ffn1_gate"]), ffn1_up=bf(p["ffn1_up"]),
        ffn1_down=bf(p["ffn1_down"]),
        norm_mix=row(p["norm_mix"]),
        w_qt=bf(jnp.swapaxes(w_q, 1, 2)), w_k=bf(w_k), w_vt=bf(jnp.swapaxes(w_v, 1, 2)),
        w_small=bf(w_small), b_small=b_small,
        w_mqk=bf(w_in[:, :, o_mqk:o_mv]), w_mv=bf(w_in[:, :, o_mv:o_mi]),
        w_mo=bf(w_in[:, :, o_mo:o_uv]), w_uv=bf(w_in[:, :, o_uv:o_gt]), w_gt=bf(w_in[:, :, o_gt:]),
        conv_ml=p["conv_ml"], norm_ml_head=row(p["norm_ml_head"]), norm_gmlp=row(p["norm_gmlp"]),
        w_spatial=p["w_spatial"],
        b_spatial=jnp.tile(jnp.repeat(jnp.swapaxes(p["b_spatial"], 1, 2), GM_GROUP_DIM, axis=2),
                           (1, reps, 1)),
        w_br_att=bf(p["w_br_att"]), w_br_ml=bf(p["w_br_ml"]), w_br_gmlp=bf(p["w_br_gmlp"]),
        w_out=bf(p["w_out"]),
        norm_ffn2=row(p["norm_ffn2"]), ffn2_gate=bf(p["ffn2_gate"]), ffn2_up=bf(p["ffn2_up"]),
        ffn2_down=bf(p["ffn2_down"]),
    )


def _layer(x2, lp, l, b, s, tiles, final_g):
    n = b * s
    x2 = _ffn(x2, lp["norm_ffn1"], lp["ffn1_gate"], lp["ffn1_up"], lp["ffn1_down"], l,
              tiles["ffn"])
    qt, k, vt, small, mq, mk, mv = _inproj(
        x2.reshape(b, s, D_MODEL), lp["norm_mix"], lp["w_qt"], lp["w_k"], lp["w_vt"],
        lp["w_small"], lp["w_mqk"], lp["w_mv"], lp["conv_ml"], l, tiles["inproj"], tiles["att_q"],
        tiles["att_k"])
    gcol, grow, k, kmax, bpre = _gates(small, lp["b_small"], k, l, tiles["gates"], tiles["att_k"])
    bpre = bpre[:, :, :tiles["gates"] // tiles["att_k"], :].reshape(b, s // tiles["att_k"], LANES)
    yatt = _attention(qt, k, vt, kmax, bpre, tiles["att_q"], tiles["att_k"], tiles["att_heads"])
    hm = _mlstm(mq, mk, mv, gcol, grow, lp["norm_ml_head"], l)
    x2 = _merge(x2, yatt.reshape(n, ATT_WIDTH), hm.reshape(n, ML_WIDTH), lp["norm_mix"],
                lp["w_mo"], lp["w_uv"], lp["w_gt"], lp["norm_gmlp"], lp["w_spatial"],
                lp["b_spatial"], lp["w_br_att"], lp["w_br_ml"], lp["w_br_gmlp"], lp["w_out"],
                l, tiles["merge"])
    return _ffn(x2, lp["norm_ffn2"], lp["ffn2_gate"], lp["ffn2_up"], lp["ffn2_down"], l,
                tiles["ffn"], final_g)


def _tiles_for(n, s):
    att_q = _tile(s, 512)
    return dict(ffn=_tile(n, 512), inproj=_tile(s, 1024), gates=_tile(s, 512), att_q=att_q,
                att_k=min(att_q // 2, 256), att_heads=2, merge=_tile(s, 512))


def _trunk(x, params, norm_final):
    b, s, _ = x.shape
    n = b * s
    tiles = _tiles_for(n, s)
    stacked = _prepare_layer_params(params, tiles["merge"])
    depth = params["w_in"].shape[0]
    x2 = x.reshape(n, D_MODEL)
    for l in range(depth):
        x2 = _layer(x2, stacked, l, b, s, tiles, norm_final[None, :] if l == depth - 1 else None)
    return x2.reshape(b, s, D_MODEL)


def kernel(x, norm_ffn1, ffn1_gate, ffn1_up, ffn1_down, norm_mix, w_in, b_f_att, b_i_ml, b_f_ml, conv_ml, norm_ml_head, norm_gmlp, w_spatial, b_spatial, w_br_att, w_br_ml, w_br_gmlp, w_out, norm_ffn2, ffn2_gate, ffn2_up, ffn2_down, norm_final):
    params = dict(norm_ffn1=norm_ffn1, ffn1_gate=ffn1_gate, ffn1_up=ffn1_up, ffn1_down=ffn1_down,
                  norm_mix=norm_mix, w_in=w_in, b_f_att=b_f_att, b_i_ml=b_i_ml, b_f_ml=b_f_ml,
                  conv_ml=conv_ml, norm_ml_head=norm_ml_head, norm_gmlp=norm_gmlp,
                  w_spatial=w_spatial, b_spatial=b_spatial, w_br_att=w_br_att, w_br_ml=w_br_ml,
                  w_br_gmlp=w_br_gmlp, w_out=w_out, norm_ffn2=norm_ffn2, ffn2_gate=ffn2_gate,
                  ffn2_up=ffn2_up, ffn2_down=ffn2_down)
    return _trunk(x, params, norm_final)
```

```python
import functools

import jax
import jax.numpy as jnp
from jax import lax
from jax.experimental import pallas as pl
from jax.experimental.pallas import tpu as pltpu

D_MODEL = 1024
ATT_HEADS = 8
ATT_HEAD_DIM = 64
ATT_WIDTH = ATT_HEADS * ATT_HEAD_DIM
ML_HEADS = 4
ML_HEAD_DIM = 128
ML_WIDTH = ML_HEADS * ML_HEAD_DIM
CONV_WIDTH = 4
GM_GROUPS = 4
GM_GROUP_DIM = 128
GM_WIDTH = GM_GROUPS * GM_GROUP_DIM
CHUNK = 128
D_FF = 2816
FFN_RES = 0.5
N_BRANCH = 3
EPS = 1e-6

LANES = 128
SUBLANES = 8
ATT_SLOTS = ATT_HEADS * LANES
ATT_ROW_BLOCK = 32
ATT_V_ROWS = 80
LOG2E = 1.4426950408889634
ATT_SKIP_EXPONENT = 160.0
VMEM_LIMIT = 56 * 1024 * 1024
NEG_BIG = -1e30

GATE_ATT_F = 0
GATE_ML_I = ATT_HEADS
GATE_ML_F = ATT_HEADS + ML_HEADS
GATE_ROWS = 16
ML_REP = 3 * ML_HEADS * 128

F32 = jnp.float32
BF16 = jnp.bfloat16


def _cparams(sem):
    return pltpu.CompilerParams(dimension_semantics=sem, vmem_limit_bytes=VMEM_LIMIT)


def _resident(shape, layer=None):
    nd = len(shape)
    if layer is None:
        return pl.BlockSpec(shape, lambda *_: (0,) * nd, pipeline_mode=pl.Buffered(1))
    return pl.BlockSpec((None,) + tuple(shape), lambda *_: (layer,) + (0,) * nd,
                        pipeline_mode=pl.Buffered(1))


def _rms(x, g):
    ms = jnp.mean(x * x, axis=-1, keepdims=True)
    return x * lax.rsqrt(ms + EPS) * g


def _log_sigmoid(z):
    return jnp.minimum(z, 0.0) - jnp.log1p(jnp.exp(-jnp.abs(z)))


def _dot(a, b):
    return jnp.dot(a, b, preferred_element_type=F32)


def _dot_nt(a, b):
    return lax.dot_general(a, b, (((1,), (1,)), ((), ())), preferred_element_type=F32)


def _ffn_kernel(x_ref, g_ref, wg_ref, wu_ref, wd_ref, *rest, final):
    o_ref = rest[-1]
    x = x_ref[...]
    hn = _rms(x, g_ref[...]).astype(BF16)
    gate = _dot(hn, wg_ref[...])
    up = _dot(hn, wu_ref[...])
    act = (gate * jax.nn.sigmoid(gate) * up).astype(BF16)
    y = x + FFN_RES * _dot(act, wd_ref[...])
    o_ref[...] = _rms(y, rest[0][...]) if final else y


def _ffn(x2, g, wg, wu, wd, l, tm, final_g=None):
    n = x2.shape[0]
    final = final_g is not None
    return pl.pallas_call(
        functools.partial(_ffn_kernel, final=final),
        out_shape=jax.ShapeDtypeStruct((n, D_MODEL), F32),
        grid=(n // tm,),
        in_specs=[
            pl.BlockSpec((tm, D_MODEL), lambda i: (i, 0)),
            _resident((1, D_MODEL), l),
            _resident((D_MODEL, D_FF), l),
            _resident((D_MODEL, D_FF), l),
            _resident((D_FF, D_MODEL), l),
        ] + ([_resident((1, D_MODEL))] if final else []),
        out_specs=pl.BlockSpec((tm, D_MODEL), lambda i: (i, 0)),
        compiler_params=_cparams(("parallel",)),
        name="ffn_final" if final else "ffn",
    )(x2, g, wg, wu, wd, *([final_g] if final else []))


def _inproj_kernel(x_ref, xp_ref, g_ref, wqt_ref, wk_ref, wvt_ref, wsm_ref, wqk_ref, wv_ref,
                   cw_ref, qt_ref, k_ref, vt_ref, sm_ref, mq_ref, mk_ref, mv_ref, *, tm, tq, tk):
    hn = _rms(x_ref[0], g_ref[...])
    hb = hn.astype(BF16)
    hnt = hn.T.astype(BF16)
    slot_row = lax.broadcasted_iota(jnp.int32, (ATT_HEADS * ATT_V_ROWS, tm), 0) % ATT_V_ROWS
    qt = _dot(wqt_ref[...], hnt)
    qt = jnp.where((slot_row >= ATT_HEAD_DIM) & (slot_row < ATT_HEAD_DIM + 3), 1.0, qt)
    qt = qt.astype(BF16)
    vt = _dot(wvt_ref[...], hnt)
    vt = jnp.where(slot_row == ATT_HEAD_DIM, 1.0, vt).astype(BF16)
    for c in range(tm // tq):
        qt_ref[0, :, c] = qt[:, c * tq:(c + 1) * tq].reshape(ATT_HEADS, ATT_V_ROWS, tq)
    for c in range(tm // tk):
        vt_ref[0, :, c] = vt[:, c * tk:(c + 1) * tk].reshape(ATT_HEADS, ATT_V_ROWS, tk)
    k_ref[0] = _dot(hb, wk_ref[...]).astype(BF16)
    sm_ref[0] = _dot(hb, wsm_ref[...])
    mv_ref[0] = _dot(hb, wv_ref[...]).astype(BF16)

    hn_before = _rms(xp_ref[0], g_ref[...])
    hn_before = jnp.where(pl.program_id(1) > 0, hn_before, 0.0)
    xcat = _dot(jnp.concatenate([hn_before, hn], axis=0).astype(BF16), wqk_ref[...])
    proj = xcat[SUBLANES:, :]
    cw = cw_ref[...]
    conv = proj * cw[CONV_WIDTH - 1:CONV_WIDTH, :]
    for j in range(CONV_WIDTH - 1):
        off = SUBLANES - (CONV_WIDTH - 1) + j
        conv = conv + xcat[off:off + tm, :] * cw[j:j + 1, :]
    act = conv * jax.nn.sigmoid(conv)
    mq_ref[0] = (act[:, :ML_WIDTH] * (ML_HEAD_DIM ** -0.5)).astype(BF16)
    mk_ref[0] = act[:, ML_WIDTH:].astype(BF16)


def _inproj(x3, g, wqt, wk, wvt, wsm, wqk, wv, cw, l, tm, tq, tk):
    b, s, _ = x3.shape
    row = lambda w: pl.BlockSpec((1, tm, w), lambda bi, si: (bi, si, 0))
    rows_before = pl.BlockSpec((1, SUBLANES, D_MODEL),
                               lambda bi, si: (bi, jnp.maximum(si * (tm // SUBLANES) - 1, 0), 0))
    tr = lambda rows, t: pl.BlockSpec((1, ATT_HEADS, tm // t, rows, t),
                                      lambda bi, si: (bi, 0, si, 0, 0))
    tr_shape = lambda rows, t: jax.ShapeDtypeStruct((b, ATT_HEADS, s // t, rows, t), BF16)
    return pl.pallas_call(
        functools.partial(_inproj_kernel, tm=tm, tq=tq, tk=tk),
        out_shape=(
            tr_shape(ATT_V_ROWS, tq),
            jax.ShapeDtypeStruct((b, s, ATT_SLOTS), BF16),
            tr_shape(ATT_V_ROWS, tk),
            jax.ShapeDtypeStruct((b, s, LANES), F32),
            jax.ShapeDtypeStruct((b, s, ML_WIDTH), BF16),
            jax.ShapeDtypeStruct((b, s, ML_WIDTH), BF16),
            jax.ShapeDtypeStruct((b, s, ML_WIDTH), BF16),
        ),
        grid=(b, s // tm),
        in_specs=[
            row(D_MODEL),
            rows_before,
            _resident((1, D_MODEL), l),
            _resident((ATT_HEADS * ATT_V_ROWS, D_MODEL), l),
            _resident((D_MODEL, ATT_SLOTS), l),
            _resident((ATT_HEADS * ATT_V_ROWS, D_MODEL), l),
            _resident((D_MODEL, LANES), l),
            _resident((D_MODEL, 2 * ML_WIDTH), l),
            _resident((D_MODEL, ML_WIDTH), l),
            _resident((CONV_WIDTH, 2 * ML_WIDTH), l),
        ],
        out_specs=(tr(ATT_V_ROWS, tq), row(ATT_SLOTS), tr(ATT_V_ROWS, tk), row(LANES),
                   row(ML_WIDTH), row(ML_WIDTH), row(ML_WIDTH)),
        compiler_params=_cparams(("parallel", "parallel")),
        name="inproj",
    )(x3, x3, g, wqt, wk, wvt, wsm, wqk, wv, cw)


def _split3(x):
    hi = x.astype(BF16)
    r1 = x - hi.astype(F32)
    mid = r1.astype(BF16)
    lo = (r1 - mid.astype(F32)).astype(BF16)
    return hi, mid, lo


def _place(x, moves):
    r = lax.broadcasted_iota(jnp.int32, (LANES, LANES), 0)
    c = lax.broadcasted_iota(jnp.int32, (LANES, LANES), 1)
    mat = jnp.zeros((LANES, LANES), F32)
    for src, dst, sign in moves:
        mat = jnp.where((r == src) & (c == dst), sign, mat)
    mat3 = jnp.concatenate([mat.astype(BF16)] * 3, axis=0)
    return _dot(jnp.concatenate(_split3(x), axis=-1), mat3)


def _gates_kernel(sm_ref, bias_ref, k_ref, col_ref, row_ref, ka_ref, kmax_ref, bpre_ref,
                  carry_ref, bcarry_ref, *, ts, tk):
    @pl.when(pl.program_id(1) == 0)
    def _():
        carry_ref[...] = jnp.zeros_like(carry_ref)
        bcarry_ref[...] = jnp.full_like(bcarry_ref, NEG_BIG)
        kmax_ref[...] = jnp.zeros_like(kmax_ref)

    z = sm_ref[0] + bias_ref[...]
    lane = lax.broadcasted_iota(jnp.int32, z.shape, 1)
    is_att = lane < GATE_ML_I
    is_mlf = (lane >= GATE_ML_F) & (lane < GATE_ML_F + ML_HEADS)
    vals = jnp.where(is_att | is_mlf, _log_sigmoid(z), z)

    r = lax.broadcasted_iota(jnp.int32, (ts, ts), 0)
    c = lax.broadcasted_iota(jnp.int32, (ts, ts), 1)
    tri = c <= r
    tri_full = jnp.where(tri, 1.0, 0.0).astype(BF16)
    tri_chunk = jnp.where(tri & ((r // CHUNK) == (c // CHUNK)), 1.0, 0.0).astype(BF16)
    pieces = jnp.concatenate(_split3(vals), axis=-1)
    add3 = lambda y: y[:, :LANES] + y[:, LANES:2 * LANES] + y[:, 2 * LANES:]
    cs_full = add3(_dot(tri_full, pieces)) + carry_ref[...]
    cs_chunk = add3(_dot(tri_chunk, pieces))
    carry_ref[...] = cs_full[ts - 1:ts, :]

    li_moves, b_moves = [], []
    for h in range(ML_HEADS):
        li_moves += [(GATE_ML_I + h, ML_HEADS + h, 1.0), (GATE_ML_I + h, 2 * ML_HEADS + h, 1.0)]
        b_moves += [(GATE_ML_F + h, h, 1.0), (GATE_ML_F + h, ML_HEADS + h, -1.0),
                    (GATE_ML_F + h, 2 * ML_HEADS + h, -1.0)]
    g = _place(vals, li_moves) + _place(cs_chunk, b_moves)
    pos = lax.broadcasted_iota(jnp.int32, (ts, LANES), 0) % CHUNK
    cmax = g
    d = 1
    while d < CHUNK:
        cmax = jnp.where(pos >= d, jnp.maximum(cmax, pltpu.roll(cmax, d, axis=0)), cmax)
        d *= 2
    g = jnp.where(lane >= 2 * ML_HEADS, cmax, g)
    col_ref[0] = g
    row_ref[0] = g.T[:GATE_ROWS, :]

    src = lax.broadcasted_iota(jnp.int32, (3 * LANES, ATT_SLOTS), 0)
    dst = lax.broadcasted_iota(jnp.int32, (3 * LANES, ATT_SLOTS), 1)
    place = (src % LANES < ATT_HEADS) & (dst == (src % LANES) * LANES + ATT_HEAD_DIM + src // LANES)
    decay = cs_full * -LOG2E
    decay_cols = _dot(jnp.concatenate(_split3(decay), axis=-1),
                      jnp.where(place, 1.0, 0.0).astype(BF16))
    kf = k_ref[0].astype(F32)
    ka_ref[0] = (kf + decay_cols).astype(BF16)

    lane_row = lax.broadcasted_iota(jnp.int32, (1, LANES), 1)
    knorm = jnp.zeros((1, LANES), F32)
    for h in range(ATT_HEADS):
        kh = kf[:, h * LANES:(h + 1) * LANES]
        sq = jnp.max(jnp.sum(kh * kh, axis=-1, keepdims=True), axis=0, keepdims=True)
        knorm = jnp.where(lane_row == h, jnp.sqrt(sq), knorm)
    kmax_ref[0] = jnp.maximum(kmax_ref[0], jnp.broadcast_to(knorm, (SUBLANES, LANES)))

    rows = []
    running = bcarry_ref[...]
    for c in range(ts // tk):
        running = jnp.maximum(running, jnp.max(decay[c * tk:(c + 1) * tk, :], axis=0, keepdims=True))
        rows.append(running)
    bcarry_ref[...] = running
    rows += [running] * (SUBLANES - len(rows))
    bpre_ref[0, 0] = jnp.concatenate(rows, axis=0)


def _gates(small, bias, k, l, ts, tk):
    b, s, _ = small.shape
    assert ts % tk == 0 and ts // tk <= SUBLANES
    row = lambda w: pl.BlockSpec((1, ts, w), lambda bi, si: (bi, si, 0))
    return pl.pallas_call(
        functools.partial(_gates_kernel, ts=ts, tk=tk),
        out_shape=(jax.ShapeDtypeStruct((b, s, LANES), F32),
                   jax.ShapeDtypeStruct((b, GATE_ROWS, s), F32),
                   jax.ShapeDtypeStruct((b, s, ATT_SLOTS), BF16),
                   jax.ShapeDtypeStruct((b, SUBLANES, LANES), F32),
                   jax.ShapeDtypeStruct((b, s // ts, SUBLANES, LANES), F32)),
        grid=(b, s // ts),
        in_specs=[row(LANES), _resident((1, LANES), l), row(ATT_SLOTS)],
        out_specs=(row(LANES),
                   pl.BlockSpec((1, GATE_ROWS, ts), lambda bi, si: (bi, 0, si)),
                   row(ATT_SLOTS),
                   pl.BlockSpec((1, SUBLANES, LANES), lambda bi, si: (bi, 0, 0)),
                   pl.BlockSpec((1, 1, SUBLANES, LANES), lambda bi, si: (bi, si, 0, 0))),
        scratch_shapes=[pltpu.VMEM((1, LANES), F32), pltpu.VMEM((1, LANES), F32)],
        compiler_params=_cparams(("parallel", "arbitrary")),
        name="gates",
    )(small, bias, k)


def _att_kernel(qt_ref, k_ref, vt_ref, kmax_ref, bpre_ref, o_ref, s0_ref, s1_ref, p0_ref, p1_ref,
                acc_ref, *, tq, tk, hb, rb):
    i = pl.program_id(2)
    nd = tq // tk
    s_refs = (s0_ref, s1_ref)
    p_refs = (p0_ref, p1_ref)

    q_pad = jnp.zeros((LANES - ATT_V_ROWS, tq), BF16)
    q_slots = [jnp.concatenate([qt_ref[0, hh, 0], q_pad], axis=0) for hh in range(hb)]

    def scores(hh, chunk):
        kj = k_ref[0, pl.ds(pl.multiple_of(chunk * tk, tk), tk), hh * LANES:(hh + 1) * LANES]
        return _dot(kj, q_slots[hh])

    def weigh(hh, chunk, par, alpha):
        acc_ref[hh] = acc_ref[hh] * alpha + _dot(vt_ref[0, hh, chunk], p_refs[par][hh])

    def visit(par, carry, pv_chunk, next_chunk, next_mask=None):
        oth = 1 - par
        out = []
        for hh in range(hb):
            m, cmax, alpha_prev = carry[hh]
            if pv_chunk is not None:
                weigh(hh, pv_chunk, oth, alpha_prev)
            m_new = jnp.maximum(m, cmax)
            alpha = jnp.exp2(m - m_new)
            for r in range(0, tk, rb):
                p_refs[par][hh, r:r + rb, :] = jnp.exp2(
                    s_refs[par][hh, r:r + rb, :] - m_new).astype(BF16)
            s_next = scores(hh, next_chunk)
            if next_mask is not None:
                s_next = jnp.where(next_mask, s_next, NEG_BIG)
            s_refs[oth][hh] = s_next
            out.append((m_new, jnp.max(s_next, axis=0, keepdims=True), alpha))
        return tuple(out)

    key = lax.broadcasted_iota(jnp.int32, (tk, tq), 0)
    qry = lax.broadcasted_iota(jnp.int32, (tk, tq), 1)
    init = []
    for hh in range(hb):
        s_first = jnp.where(key <= qry, scores(hh, nd * i), NEG_BIG)
        s0_ref[hh] = s_first
        acc_ref[hh] = jnp.zeros((ATT_V_ROWS, tq), F32)
        init.append((jnp.full((1, tq), NEG_BIG, F32), jnp.max(s_first, axis=0, keepdims=True),
                     jnp.ones((1, tq), F32)))
    carry = tuple(init)
    below = nd * i
    for d in range(nd):
        pv_chunk = below + d - 1 if d else None
        if d + 1 < nd:
            carry = visit(d % 2, carry, pv_chunk, below + d + 1, key + (d + 1) * tk <= qry)
        else:
            carry = visit(d % 2, carry, pv_chunk, jnp.maximum(below - 1, 0))

    lane = lax.broadcasted_iota(jnp.int32, (1, LANES), 1)
    bound = jnp.full((1, LANES), NEG_BIG, F32)
    for hh in range(hb):
        h = pl.program_id(1) * hb + hh
        qf = qt_ref[0, hh, 0][:ATT_HEAD_DIM, :].astype(F32)
        qmax = jnp.sqrt(jnp.max(jnp.sum(qf * qf, axis=0, keepdims=True), axis=-1, keepdims=True))
        mmin = jnp.min(carry[hh][0], axis=-1, keepdims=True)
        bound = jnp.where(lane == h, qmax * kmax_ref[0, 0:1, :] - mmin, bound)
    chunk_id = lax.broadcasted_iota(jnp.int32, bpre_ref.shape[1:], 0)
    live = (bpre_ref[0] + bound >= -ATT_SKIP_EXPONENT) & (chunk_id < below)
    n_live = jnp.max(jnp.sum(jnp.where(live, 1.0, 0.0), axis=0, keepdims=True)).astype(jnp.int32)
    n_pairs = (n_live + 1) // 2

    def pair(w, carry):
        top = below - 2 * w
        carry = visit(0, carry, jnp.where(w == 0, below + nd - 1, top), top - 2)
        return visit(1, carry, top - 1, jnp.maximum(top - 3, 0))

    def two_pairs(w2, carry):
        return pair(2 * w2 + 1, pair(2 * w2, carry))

    carry = lax.fori_loop(0, n_pairs // 2, two_pairs, carry)
    carry = lax.fori_loop(2 * (n_pairs // 2), n_pairs, pair, carry)
    last_chunk = jnp.where(n_pairs == 0, below + nd - 1, below - 2 * n_pairs)
    pad_rows = jnp.zeros((LANES - ATT_V_ROWS, tq), F32)
    for pr in range(hb // 2):
        halves = []
        for hh in (2 * pr, 2 * pr + 1):
            weigh(hh, last_chunk, 1, carry[hh][2])
            acc_t = jnp.concatenate([acc_ref[hh], pad_rows], axis=0).T
            halves.append(acc_t / acc_t[:, ATT_HEAD_DIM:ATT_HEAD_DIM + 1])
        both = jnp.where(lane < ATT_HEAD_DIM, halves[0],
                         pltpu.roll(halves[1], ATT_HEAD_DIM, axis=1))
        o_ref[0, :, pr * LANES:(pr + 1) * LANES] = both.astype(o_ref.dtype)


def _attention(qt, k, vt, kmax, bpre, tq, tk, hb):
    b, s, _ = k.shape
    assert tq % (2 * tk) == 0 and bpre.shape == (b, s // tk, LANES)
    return pl.pallas_call(
        functools.partial(_att_kernel, tq=tq, tk=tk, hb=hb, rb=ATT_ROW_BLOCK),
        out_shape=jax.ShapeDtypeStruct((b, s, ATT_WIDTH), BF16),
        grid=(b, ATT_HEADS // hb, s // tq),
        in_specs=[
            pl.BlockSpec((1, hb, 1, ATT_V_ROWS, tq), lambda bi, hg, i: (bi, hg, i, 0, 0)),
            pl.BlockSpec((1, s, hb * LANES), lambda bi, hg, i: (bi, 0, hg),
                         pipeline_mode=pl.Buffered(1)),
            pl.BlockSpec((1, hb, s // tk, ATT_V_ROWS, tk), lambda bi, hg, i: (bi, hg, 0, 0, 0),
                         pipeline_mode=pl.Buffered(1)),
            pl.BlockSpec((1, SUBLANES, LANES), lambda bi, hg, i: (bi, 0, 0)),
            pl.BlockSpec((1, s // tk, LANES), lambda bi, hg, i: (bi, 0, 0)),
        ],
        out_specs=pl.BlockSpec((1, tq, hb * ATT_HEAD_DIM), lambda bi, hg, i: (bi, i, hg)),
        scratch_shapes=[pltpu.VMEM((hb, tk, tq), F32), pltpu.VMEM((hb, tk, tq), F32),
                        pltpu.VMEM((hb, tk, tq), BF16), pltpu.VMEM((hb, tk, tq), BF16),
                        pltpu.VMEM((hb, ATT_V_ROWS, tq), F32)],
        compiler_params=_cparams(("parallel", "parallel", "arbitrary")),
        name="fox_attention",
    )(qt, k, vt, kmax, bpre)


def _mlstm_kernel(q_ref, k_ref, v_ref, gcol_ref, grow_ref, spread_ref, nw_ref, o_ref,
                  st_ref, m_ref):
    @pl.when(pl.program_id(1) == 0)
    def _():
        st_ref[...] = jnp.zeros_like(st_ref)
        m_ref[...] = jnp.zeros_like(m_ref)

    L = CHUNK
    D = ML_HEAD_DIM
    H = range(ML_HEADS)
    r = lax.broadcasted_iota(jnp.int32, (L, L), 0)
    c = lax.broadcasted_iota(jnp.int32, (L, L), 1)
    causal = c <= r
    ones = jnp.ones((L, D), BF16)
    mean_mat = jnp.full((D, D), 1.0 / D, BF16)
    two = lambda x: jnp.concatenate([x, x], axis=-1)
    head = lambda ref, h: ref[0, :, h * D:(h + 1) * D]

    rep = _dot(jnp.concatenate(_split3(gcol_ref[0]), axis=-1), spread_ref[...])
    col = lambda j, h: rep[:, (j * ML_HEADS + h) * LANES:(j * ML_HEADS + h + 1) * LANES]
    qk = [_dot_nt(head(q_ref, h), head(k_ref, h)) for h in H]
    st = [st_ref[h] for h in H]
    inter = [_dot(head(q_ref, h), st[h].astype(BF16)) for h in H]
    kt = [head(k_ref, h).astype(F32).T.astype(BF16) for h in H]

    m_t, w_intra, w_inter, decay, wv = [], [], [], [], []
    for h in H:
        b, gq, cm = col(0, h), col(1, h), col(2, h)
        gq_row = grow_ref[0, ML_HEADS + h:ML_HEADS + h + 1, :]
        m = m_ref[h]
        a = b + m
        m_t.append(jnp.maximum(a, b + cm))
        w_intra.append(jnp.exp(jnp.where(causal, b + gq_row, NEG_BIG) - m_t[h]))
        w_inter.append(jnp.exp(a - m_t[h]))
        b_last = b[L - 1:L, :]
        m_new = jnp.maximum(b_last + m, b_last + cm[L - 1:L, :])
        w_s = jnp.exp(b_last + gq - m_new)
        decay.append(jnp.exp(b_last + m - m_new))
        wv.append(jnp.concatenate([w_s * head(v_ref, h).astype(F32), w_s], axis=-1).astype(BF16))
        m_ref[h] = m_new

    qkw = [(qk[h] * w_intra[h]).astype(BF16) for h in H]
    both = [_dot(qkw[h], jnp.concatenate([head(v_ref, h), ones], axis=-1))
            + two(w_inter[h]) * inter[h] for h in H]
    for h in H:
        st_ref[h] = two(decay[h]) * st[h] + _dot(kt[h], wv[h])
    hh = [both[h][:, :D] / jnp.maximum(jnp.abs(both[h][:, D:]), jnp.exp(-m_t[h])) for h in H]
    ms = []
    for h in H:
        sq = hh[h] * hh[h]
        sq_hi = sq.astype(BF16)
        sq_lo = (sq - sq_hi.astype(F32)).astype(BF16)
        ms.append(_dot(sq_hi, mean_mat) + _dot(sq_lo, mean_mat))
    for h in H:
        o_ref[0, :, h * D:(h + 1) * D] = (hh[h] * lax.rsqrt(ms[h] + EPS)
                                          * nw_ref[:, h * D:(h + 1) * D])


def _mlstm(mq, mk, mv, gcol, grow, norm_w, l):
    b, s, _ = mq.shape
    nc = s // CHUNK
    row = lambda w: pl.BlockSpec((1, CHUNK, w), lambda bi, ci: (bi, ci, 0))
    src = lax.broadcasted_iota(jnp.int32, (3 * LANES, ML_REP), 0) % LANES
    dst = lax.broadcasted_iota(jnp.int32, (3 * LANES, ML_REP), 1) // LANES
    spread = (src == dst).astype(BF16)
    return pl.pallas_call(
        _mlstm_kernel,
        out_shape=jax.ShapeDtypeStruct((b, s, ML_WIDTH), F32),
        grid=(b, nc),
        in_specs=[
            row(ML_WIDTH), row(ML_WIDTH), row(ML_WIDTH), row(LANES),
            pl.BlockSpec((1, GATE_ROWS, CHUNK), lambda bi, ci: (bi, 0, ci)),
            _resident((3 * LANES, ML_REP)),
            _resident((1, ML_WIDTH), l),
        ],
        out_specs=row(ML_WIDTH),
        scratch_shapes=[
            pltpu.VMEM((ML_HEADS, ML_HEAD_DIM, 2 * ML_HEAD_DIM), F32),
            pltpu.VMEM((ML_HEADS, 1, LANES), F32),
        ],
        compiler_params=_cparams(("parallel", "arbitrary")),
        name="mlstm",
    )(mq, mk, mv, gcol, grow, spread, norm_w)


def _gelu_tanh(x):
    return 0.5 * x * (1.0 + jnp.tanh(0.7978845608028654 * (x + 0.044715 * (x * x * x))))


def _merge_kernel(x_ref, yatt_ref, hm_ref, g_ref, wo_ref, wuv_ref, wgt_ref, ng_ref,
                  wsp_ref, bsp_ref, wba_ref, wbm_ref, wbg_ref, wout_ref, o_ref, *, tm):
    x = x_ref[...]
    hn = _rms(x, g_ref[...]).astype(BF16)

    y_ml = jax.nn.sigmoid(_dot(hn, wo_ref[...])) * hm_ref[...]

    uv = _gelu_tanh(_dot(hn, wuv_ref[...]))
    u = uv[:, :GM_WIDTH]
    vn = _rms(uv[:, GM_WIDTH:], ng_ref[...]).astype(BF16)
    r = lax.broadcasted_iota(jnp.int32, (CHUNK, CHUNK), 0)
    c = lax.broadcasted_iota(jnp.int32, (CHUNK, CHUNK), 1)
    mixed = []
    for gi in range(GM_GROUPS):
        w = jnp.where(c <= r, wsp_ref[gi], 0.0).astype(BF16)
        lanes = slice(gi * GM_GROUP_DIM, (gi + 1) * GM_GROUP_DIM)
        mixed.append(jnp.concatenate(
            [_dot(w, vn[ci * CHUNK:(ci + 1) * CHUNK, lanes]) for ci in range(tm // CHUNK)], axis=0))
    y_gm = u * (jnp.concatenate(mixed, axis=-1) + bsp_ref[...])

    gates = jax.nn.sigmoid(_dot(hn, wgt_ref[...]))
    merged = (gates[:, :D_MODEL] * _dot(yatt_ref[...], wba_ref[...])
              + gates[:, D_MODEL:2 * D_MODEL] * _dot(y_ml.astype(BF16), wbm_ref[...])
              + gates[:, 2 * D_MODEL:] * _dot(y_gm.astype(BF16), wbg_ref[...]))
    o_ref[...] = x + _dot(merged.astype(BF16), wout_ref[...])


def _merge(x2, yatt, hm, g, wo, wuv, wgt, ng, wsp, bsp, wba, wbm, wbg, wout, l, tm):
    n = x2.shape[0]
    row = lambda w: pl.BlockSpec((tm, w), lambda i: (i, 0))
    return pl.pallas_call(
        functools.partial(_merge_kernel, tm=tm),
        out_shape=jax.ShapeDtypeStruct((n, D_MODEL), F32),
        grid=(n // tm,),
        in_specs=[
            row(D_MODEL), row(ATT_WIDTH), row(ML_WIDTH),
            _resident((1, D_MODEL), l),
            _resident((D_MODEL, ML_WIDTH), l),
            _resident((D_MODEL, 2 * GM_WIDTH), l),
            _resident((D_MODEL, N_BRANCH * D_MODEL), l),
            _resident((1, GM_WIDTH), l),
            _resident((GM_GROUPS, CHUNK, CHUNK), l),
            _resident((tm, GM_WIDTH), l),
            _resident((ATT_WIDTH, D_MODEL), l),
            _resident((ML_WIDTH, D_MODEL), l),
            _resident((GM_WIDTH, D_MODEL), l),
            _resident((D_MODEL, D_MODEL), l),
        ],
        out_specs=row(D_MODEL),
        compiler_params=_cparams(("parallel",)),
        name="merge",
    )(x2, yatt, hm, g, wo, wuv, wgt, ng, wsp, bsp, wba, wbm, wbg, wout)


def _tile(n, pref):
    t = min(n, pref)
    assert n % t == 0, (n, t)
    return t


def _prepare_layer_params(p, tm_merge):
    depth = p["w_in"].shape[0]
    o_att = 3 * ATT_WIDTH
    o_mqk = o_att + ATT_HEADS
    o_mv = o_mqk + 2 * ML_WIDTH
    o_mi = o_mv + ML_WIDTH
    o_mf = o_mi + ML_HEADS
    o_mo = o_mf + ML_HEADS
    o_uv = o_mo + ML_WIDTH
    o_gt = o_uv + 2 * GM_WIDTH
    w_in = p["w_in"]
    w_small = jnp.concatenate(
        [w_in[:, :, o_att:o_mqk], w_in[:, :, o_mi:o_mf], w_in[:, :, o_mf:o_mo],
         jnp.zeros((depth, D_MODEL, LANES - GATE_ROWS), F32)], axis=-1)
    b_small = jnp.concatenate(
        [p["b_f_att"], p["b_i_ml"], p["b_f_ml"], jnp.zeros((depth, LANES - GATE_ROWS), F32)],
        axis=-1)[:, None, :]
    reps = tm_merge // CHUNK
    bf = lambda a: a.astype(BF16)
    row = lambda a: a[:, None, :]

    def slots(w, width):
        w = w.reshape(depth, D_MODEL, ATT_HEADS, ATT_HEAD_DIM)
        w = jnp.pad(w, ((0, 0), (0, 0), (0, 0), (0, width - ATT_HEAD_DIM)))
        return w.reshape(depth, D_MODEL, ATT_HEADS * width)

    w_q = slots(w_in[:, :, :ATT_WIDTH], ATT_V_ROWS) * (ATT_HEAD_DIM ** -0.5 * LOG2E)
    w_k = slots(w_in[:, :, ATT_WIDTH:2 * ATT_WIDTH], LANES)
    w_v = slots(w_in[:, :, 2 * ATT_WIDTH:o_att], ATT_V_ROWS)
    return dict(
        norm_ffn1=row(p["norm_ffn1"]), ffn1_gate=bf(p["ffn1_gate"]), ffn1_up=bf(p["ffn1_up"]),
        ffn1_down=bf(p["ffn1_down"]),
        norm_mix=row(p["norm_mix"]),
        w_qt=bf(jnp.swapaxes(w_q, 1, 2)), w_k=bf(w_k), w_vt=bf(jnp.swapaxes(w_v, 1, 2)),
        w_small=bf(w_small), b_small=b_small,
        w_mqk=bf(w_in[:, :, o_mqk:o_mv]), w_mv=bf(w_in[:, :, o_mv:o_mi]),
        w_mo=bf(w_in[:, :, o_mo:o_uv]), w_uv=bf(w_in[:, :, o_uv:o_gt]), w_gt=bf(w_in[:, :, o_gt:]),
        conv_ml=p["conv_ml"], norm_ml_head=row(p["norm_ml_head"]), norm_gmlp=row(p["norm_gmlp"]),
        w_spatial=p["w_spatial"],
        b_spatial=jnp.tile(jnp.repeat(jnp.swapaxes(p["b_spatial"], 1, 2), GM_GROUP_DIM, axis=2),
                           (1, reps, 1)),
        w_br_att=bf(p["w_br_att"]), w_br_ml=bf(p["w_br_ml"]), w_br_gmlp=bf(p["w_br_gmlp"]),
        w_out=bf(p["w_out"]),
        norm_ffn2=row(p["norm_ffn2"]), ffn2_gate=bf(p["ffn2_gate"]), ffn2_up=bf(p["ffn2_up"]),
        ffn2_down=bf(p["ffn2_down"]),
    )


def _layer(x2, lp, l, b, s, tiles, final_g):
    n = b * s
    x2 = _ffn(x2, lp["norm_ffn1"], lp["ffn1_gate"], lp["ffn1_up"], lp["ffn1_down"], l,
              tiles["ffn"])
    qt, k, vt, small, mq, mk, mv = _inproj(
        x2.reshape(b, s, D_MODEL), lp["norm_mix"], lp["w_qt"], lp["w_k"], lp["w_vt"],
        lp["w_small"], lp["w_mqk"], lp["w_mv"], lp["conv_ml"], l, tiles["inproj"], tiles["att_q"],
        tiles["att_k"])
    gcol, grow, k, kmax, bpre = _gates(small, lp["b_small"], k, l, tiles["gates"], tiles["att_k"])
    bpre = bpre[:, :, :tiles["gates"] // tiles["att_k"], :].reshape(b, s // tiles["att_k"], LANES)
    yatt = _attention(qt, k, vt, kmax, bpre, tiles["att_q"], tiles["att_k"], tiles["att_heads"])
    hm = _mlstm(mq, mk, mv, gcol, grow, lp["norm_ml_head"], l)
    x2 = _merge(x2, yatt.reshape(n, ATT_WIDTH), hm.reshape(n, ML_WIDTH), lp["norm_mix"],
                lp["w_mo"], lp["w_uv"], lp["w_gt"], lp["norm_gmlp"], lp["w_spatial"],
                lp["b_spatial"], lp["w_br_att"], lp["w_br_ml"], lp["w_br_gmlp"], lp["w_out"],
                l, tiles["merge"])
    return _ffn(x2, lp["norm_ffn2"], lp["ffn2_gate"], lp["ffn2_up"], lp["ffn2_down"], l,
                tiles["ffn"], final_g)


def _tiles_for(n, s):
    att_q = _tile(s, 512)
    return dict(ffn=_tile(n, 512), inproj=_tile(s, 1024), gates=_tile(s, 512), att_q=att_q,
                att_k=min(att_q // 2, 256), att_heads=2, merge=_tile(s, 512))


def _trunk(x, params, norm_final):
    b, s, _ = x.shape
    n = b * s
    tiles = _tiles_for(n, s)
    stacked = _prepare_layer_params(params, tiles["merge"])
    depth = params["w_in"].shape[0]
    x2 = x.reshape(n, D_MODEL)
    for l in range(depth):
        x2 = _layer(x2, stacked, l, b, s, tiles, norm_final[None, :] if l == depth - 1 else None)
    return x2.reshape(b, s, D_MODEL)


def kernel(x, norm_ffn1, ffn1_gate, ffn1_up, ffn1_down, norm_mix, w_in, b_f_att, b_i_ml, b_f_ml, conv_ml, norm_ml_head, norm_gmlp, w_spatial, b_spatial, w_br_att, w_br_ml, w_br_gmlp, w_out, norm_ffn2, ffn2_gate, ffn2_up, ffn2_down, norm_final):
    params = dict(norm_ffn1=norm_ffn1, ffn1_gate=ffn1_gate, ffn1_up=ffn1_up, ffn1_down=ffn1_down,
                  norm_mix=norm_mix, w_in=w_in, b_f_att=b_f_att, b_i_ml=b_i_ml, b_f_ml=b_f_ml,
                  conv_ml=conv_ml, norm_ml_head=norm_ml_head, norm_gmlp=norm_gmlp,
                  w_spatial=w_spatial, b_spatial=b_spatial, w_br_att=w_br_att, w_br_ml=w_br_ml,
                  w_br_gmlp=w_br_gmlp, w_out=w_out, norm_ffn2=norm_ffn2, ffn2_gate=ffn2_gate,
                  ffn2_up=ffn2_up, ffn2_down=ffn2_down)
    return _trunk(x, params, norm_final)
```

```python
import functools

import jax
import jax.numpy as jnp
from jax import lax
from jax.experimental import pallas as pl
from jax.experimental.pallas import tpu as pltpu

D_MODEL = 1024
ATT_HEADS = 8
ATT_HEAD_DIM = 64
ATT_WIDTH = ATT_HEADS * ATT_HEAD_DIM
ML_HEADS = 4
ML_HEAD_DIM = 128
ML_WIDTH = ML_HEADS * ML_HEAD_DIM
CONV_WIDTH = 4
GM_GROUPS = 4
GM_GROUP_DIM = 128
GM_WIDTH = GM_GROUPS * GM_GROUP_DIM
CHUNK = 128
D_FF = 2816
FFN_RES = 0.5
N_BRANCH = 3
EPS = 1e-6

LANES = 128
SUBLANES = 8
ATT_SLOTS = ATT_HEADS * LANES
ATT_V_ROWS = 80
LOG2E = 1.4426950408889634
ATT_SKIP_EXPONENT = 160.0
VMEM_LIMIT = 56 * 1024 * 1024
NEG_BIG = -1e30

GATE_ATT_F = 0
GATE_ML_I = ATT_HEADS
GATE_ML_F = ATT_HEADS + ML_HEADS
GATE_ROWS = 16
ML_REP = 3 * ML_HEADS * 128

F32 = jnp.float32
BF16 = jnp.bfloat16


def _cparams(sem):
    return pltpu.CompilerParams(dimension_semantics=sem, vmem_limit_bytes=VMEM_LIMIT)


def _resident(shape, layer=None):
    nd = len(shape)
    if layer is None:
        return pl.BlockSpec(shape, lambda *_: (0,) * nd, pipeline_mode=pl.Buffered(1))
    return pl.BlockSpec((None,) + tuple(shape), lambda *_: (layer,) + (0,) * nd,
                        pipeline_mode=pl.Buffered(1))


def _rms(x, g):
    ms = jnp.mean(x * x, axis=-1, keepdims=True)
    return x * lax.rsqrt(ms + EPS) * g


def _log_sigmoid(z):
    return jnp.minimum(z, 0.0) - jnp.log1p(jnp.exp(-jnp.abs(z)))


def _dot(a, b):
    return jnp.dot(a, b, preferred_element_type=F32)


def _dot_nt(a, b):
    return lax.dot_general(a, b, (((1,), (1,)), ((), ())), preferred_element_type=F32)


def _ffn_kernel(x_ref, g_ref, wg_ref, wu_ref, wd_ref, *rest, final):
    o_ref = rest[-1]
    x = x_ref[...]
    hn = _rms(x, g_ref[...]).astype(BF16)
    gate = _dot(hn, wg_ref[...])
    up = _dot(hn, wu_ref[...])
    act = (gate * jax.nn.sigmoid(gate) * up).astype(BF16)
    y = x + FFN_RES * _dot(act, wd_ref[...])
    o_ref[...] = _rms(y, rest[0][...]) if final else y


def _ffn(x2, g, wg, wu, wd, l, tm, final_g=None):
    n = x2.shape[0]
    final = final_g is not None
    return pl.pallas_call(
        functools.partial(_ffn_kernel, final=final),
        out_shape=jax.ShapeDtypeStruct((n, D_MODEL), F32),
        grid=(n // tm,),
        in_specs=[
            pl.BlockSpec((tm, D_MODEL), lambda i: (i, 0)),
            _resident((1, D_MODEL), l),
            _resident((D_MODEL, D_FF), l),
            _resident((D_MODEL, D_FF), l),
            _resident((D_FF, D_MODEL), l),
        ] + ([_resident((1, D_MODEL))] if final else []),
        out_specs=pl.BlockSpec((tm, D_MODEL), lambda i: (i, 0)),
        compiler_params=_cparams(("parallel",)),
        name="ffn_final" if final else "ffn",
    )(x2, g, wg, wu, wd, *([final_g] if final else []))


def _inproj_kernel(x_ref, xp_ref, g_ref, wqt_ref, wk_ref, wvt_ref, wsm_ref, wqk_ref, wv_ref,
                   cw_ref, qt_ref, k_ref, vt_ref, sm_ref, mq_ref, mk_ref, mv_ref, *, tm, tq, tk):
    hn = _rms(x_ref[0], g_ref[...])
    hb = hn.astype(BF16)
    hnt = hn.T.astype(BF16)
    slot_row = lax.broadcasted_iota(jnp.int32, (ATT_HEADS * ATT_V_ROWS, tm), 0) % ATT_V_ROWS
    qt = _dot(wqt_ref[...], hnt)
    qt = jnp.where((slot_row >= ATT_HEAD_DIM) & (slot_row < ATT_HEAD_DIM + 3), 1.0, qt)
    qt = qt.astype(BF16)
    vt = _dot(wvt_ref[...], hnt)
    vt = jnp.where(slot_row == ATT_HEAD_DIM, 1.0, vt).astype(BF16)
    for c in range(tm // tq):
        qt_ref[0, :, c] = qt[:, c * tq:(c + 1) * tq].reshape(ATT_HEADS, ATT_V_ROWS, tq)
    for c in range(tm // tk):
        vt_ref[0, :, c] = vt[:, c * tk:(c + 1) * tk].reshape(ATT_HEADS, ATT_V_ROWS, tk)
    k_ref[0] = _dot(hb, wk_ref[...]).astype(BF16)
    sm_ref[0] = _dot(hb, wsm_ref[...])
    mv_ref[0] = _dot(hb, wv_ref[...]).astype(BF16)

    hn_before = _rms(xp_ref[0], g_ref[...])
    hn_before = jnp.where(pl.program_id(1) > 0, hn_before, 0.0)
    xcat = _dot(jnp.concatenate([hn_before, hn], axis=0).astype(BF16), wqk_ref[...])
    proj = xcat[SUBLANES:, :]
    cw = cw_ref[...]
    conv = proj * cw[CONV_WIDTH - 1:CONV_WIDTH, :]
    for j in range(CONV_WIDTH - 1):
        off = SUBLANES - (CONV_WIDTH - 1) + j
        conv = conv + xcat[off:off + tm, :] * cw[j:j + 1, :]
    act = conv * jax.nn.sigmoid(conv)
    mq_ref[0] = (act[:, :ML_WIDTH] * (ML_HEAD_DIM ** -0.5)).astype(BF16)
    mk_ref[0] = act[:, ML_WIDTH:].astype(BF16)


def _inproj(x3, g, wqt, wk, wvt, wsm, wqk, wv, cw, l, tm, tq, tk):
    b, s, _ = x3.shape
    row = lambda w: pl.BlockSpec((1, tm, w), lambda bi, si: (bi, si, 0))
    rows_before = pl.BlockSpec((1, SUBLANES, D_MODEL),
                               lambda bi, si: (bi, jnp.maximum(si * (tm // SUBLANES) - 1, 0), 0))
    tr = lambda rows, t: pl.BlockSpec((1, ATT_HEADS, tm // t, rows, t),
                                      lambda bi, si: (bi, 0, si, 0, 0))
    tr_shape = lambda rows, t: jax.ShapeDtypeStruct((b, ATT_HEADS, s // t, rows, t), BF16)
    return pl.pallas_call(
        functools.partial(_inproj_kernel, tm=tm, tq=tq, tk=tk),
        out_shape=(
            tr_shape(ATT_V_ROWS, tq),
            jax.ShapeDtypeStruct((b, s, ATT_SLOTS), BF16),
            tr_shape(ATT_V_ROWS, tk),
            jax.ShapeDtypeStruct((b, s, LANES), F32),
            jax.ShapeDtypeStruct((b, s, ML_WIDTH), BF16),
            jax.ShapeDtypeStruct((b, s, ML_WIDTH), BF16),
            jax.ShapeDtypeStruct((b, s, ML_WIDTH), BF16),
        ),
        grid=(b, s // tm),
        in_specs=[
            row(D_MODEL),
            rows_before,
            _resident((1, D_MODEL), l),
            _resident((ATT_HEADS * ATT_V_ROWS, D_MODEL), l),
            _resident((D_MODEL, ATT_SLOTS), l),
            _resident((ATT_HEADS * ATT_V_ROWS, D_MODEL), l),
            _resident((D_MODEL, LANES), l),
            _resident((D_MODEL, 2 * ML_WIDTH), l),
            _resident((D_MODEL, ML_WIDTH), l),
            _resident((CONV_WIDTH, 2 * ML_WIDTH), l),
        ],
        out_specs=(tr(ATT_V_ROWS, tq), row(ATT_SLOTS), tr(ATT_V_ROWS, tk), row(LANES),
                   row(ML_WIDTH), row(ML_WIDTH), row(ML_WIDTH)),
        compiler_params=_cparams(("parallel", "parallel")),
        name="inproj",
    )(x3, x3, g, wqt, wk, wvt, wsm, wqk, wv, cw)


def _split3(x):
    hi = x.astype(BF16)
    r1 = x - hi.astype(F32)
    mid = r1.astype(BF16)
    lo = (r1 - mid.astype(F32)).astype(BF16)
    return hi, mid, lo


def _place(x, moves):
    r = lax.broadcasted_iota(jnp.int32, (LANES, LANES), 0)
    c = lax.broadcasted_iota(jnp.int32, (LANES, LANES), 1)
    mat = jnp.zeros((LANES, LANES), F32)
    for src, dst, sign in moves:
        mat = jnp.where((r == src) & (c == dst), sign, mat)
    mat3 = jnp.concatenate([mat.astype(BF16)] * 3, axis=0)
    return _dot(jnp.concatenate(_split3(x), axis=-1), mat3)


def _gates_kernel(sm_ref, bias_ref, k_ref, col_ref, row_ref, ka_ref, kmax_ref, bpre_ref,
                  carry_ref, bcarry_ref, *, ts, tk):
    @pl.when(pl.program_id(1) == 0)
    def _():
        carry_ref[...] = jnp.zeros_like(carry_ref)
        bcarry_ref[...] = jnp.full_like(bcarry_ref, NEG_BIG)
        kmax_ref[...] = jnp.zeros_like(kmax_ref)

    z = sm_ref[0] + bias_ref[...]
    lane = lax.broadcasted_iota(jnp.int32, z.shape, 1)
    is_att = lane < GATE_ML_I
    is_mlf = (lane >= GATE_ML_F) & (lane < GATE_ML_F + ML_HEADS)
    vals = jnp.where(is_att | is_mlf, _log_sigmoid(z), z)

    r = lax.broadcasted_iota(jnp.int32, (ts, ts), 0)
    c = lax.broadcasted_iota(jnp.int32, (ts, ts), 1)
    tri = c <= r
    tri_full = jnp.where(tri, 1.0, 0.0).astype(BF16)
    tri_chunk = jnp.where(tri & ((r // CHUNK) == (c // CHUNK)), 1.0, 0.0).astype(BF16)
    pieces = jnp.concatenate(_split3(vals), axis=-1)
    add3 = lambda y: y[:, :LANES] + y[:, LANES:2 * LANES] + y[:, 2 * LANES:]
    cs_full = add3(_dot(tri_full, pieces)) + carry_ref[...]
    cs_chunk = add3(_dot(tri_chunk, pieces))
    carry_ref[...] = cs_full[ts - 1:ts, :]

    li_moves, b_moves = [], []
    for h in range(ML_HEADS):
        li_moves += [(GATE_ML_I + h, ML_HEADS + h, 1.0), (GATE_ML_I + h, 2 * ML_HEADS + h, 1.0)]
        b_moves += [(GATE_ML_F + h, h, 1.0), (GATE_ML_F + h, ML_HEADS + h, -1.0),
                    (GATE_ML_F + h, 2 * ML_HEADS + h, -1.0)]
    g = _place(vals, li_moves) + _place(cs_chunk, b_moves)
    pos = lax.broadcasted_iota(jnp.int32, (ts, LANES), 0) % CHUNK
    cmax = g
    d = 1
    while d < CHUNK:
        cmax = jnp.where(pos >= d, jnp.maximum(cmax, pltpu.roll(cmax, d, axis=0)), cmax)
        d *= 2
    g = jnp.where(lane >= 2 * ML_HEADS, cmax, g)
    col_ref[0] = g
    row_ref[0] = g.T[:GATE_ROWS, :]

    src = lax.broadcasted_iota(jnp.int32, (3 * LANES, ATT_SLOTS), 0)
    dst = lax.broadcasted_iota(jnp.int32, (3 * LANES, ATT_SLOTS), 1)
    place = (src % LANES < ATT_HEADS) & (dst == (src % LANES) * LANES + ATT_HEAD_DIM + src // LANES)
    decay = cs_full * -LOG2E
    decay_cols = _dot(jnp.concatenate(_split3(decay), axis=-1),
                      jnp.where(place, 1.0, 0.0).astype(BF16))
    kf = k_ref[0].astype(F32)
    ka_ref[0] = (kf + decay_cols).astype(BF16)

    lane_row = lax.broadcasted_iota(jnp.int32, (1, LANES), 1)
    knorm = jnp.zeros((1, LANES), F32)
    for h in range(ATT_HEADS):
        kh = kf[:, h * LANES:(h + 1) * LANES]
        sq = jnp.max(jnp.sum(kh * kh, axis=-1, keepdims=True), axis=0, keepdims=True)
        knorm = jnp.where(lane_row == h, jnp.sqrt(sq), knorm)
    kmax_ref[0] = jnp.maximum(kmax_ref[0], jnp.broadcast_to(knorm, (SUBLANES, LANES)))

    rows = []
    running = bcarry_ref[...]
    for c in range(ts // tk):
        running = jnp.maximum(running, jnp.max(decay[c * tk:(c + 1) * tk, :], axis=0, keepdims=True))
        rows.append(running)
    bcarry_ref[...] = running
    rows += [running] * (SUBLANES - len(rows))
    bpre_ref[0, 0] = jnp.concatenate(rows, axis=0)


def _gates(small, bias, k, l, ts, tk):
    b, s, _ = small.shape
    assert ts % tk == 0 and ts // tk <= SUBLANES
    row = lambda w: pl.BlockSpec((1, ts, w), lambda bi, si: (bi, si, 0))
    return pl.pallas_call(
        functools.partial(_gates_kernel, ts=ts, tk=tk),
        out_shape=(jax.ShapeDtypeStruct((b, s, LANES), F32),
                   jax.ShapeDtypeStruct((b, GATE_ROWS, s), F32),
                   jax.ShapeDtypeStruct((b, s, ATT_SLOTS), BF16),
                   jax.ShapeDtypeStruct((b, SUBLANES, LANES), F32),
                   jax.ShapeDtypeStruct((b, s // ts, SUBLANES, LANES), F32)),
        grid=(b, s // ts),
        in_specs=[row(LANES), _resident((1, LANES), l), row(ATT_SLOTS)],
        out_specs=(row(LANES),
                   pl.BlockSpec((1, GATE_ROWS, ts), lambda bi, si: (bi, 0, si)),
                   row(ATT_SLOTS),
                   pl.BlockSpec((1, SUBLANES, LANES), lambda bi, si: (bi, 0, 0)),
                   pl.BlockSpec((1, 1, SUBLANES, LANES), lambda bi, si: (bi, si, 0, 0))),
        scratch_shapes=[pltpu.VMEM((1, LANES), F32), pltpu.VMEM((1, LANES), F32)],
        compiler_params=_cparams(("parallel", "arbitrary")),
        name="gates",
    )(small, bias, k)


def _att_kernel(qt_ref, k_ref, vt_ref, kmax_ref, bpre_ref, o_ref, s0_ref, s1_ref, acc_ref,
                *, tq, tk, hb):
    i = pl.program_id(2)
    nd = tq // tk
    s_refs = (s0_ref, s1_ref)

    q_pad = jnp.zeros((LANES - ATT_V_ROWS, tq), BF16)
    q_slots = [jnp.concatenate([qt_ref[0, hh, 0], q_pad], axis=0) for hh in range(hb)]

    def scores(hh, chunk):
        kj = k_ref[0, pl.ds(pl.multiple_of(chunk * tk, tk), tk), hh * LANES:(hh + 1) * LANES]
        return _dot(kj, q_slots[hh])

    def visit(par, carry, cur_chunk, next_chunk, next_mask=None):
        oth = 1 - par
        out = []
        for hh in range(hb):
            m, cmax = carry[hh]
            s_next = scores(hh, next_chunk)
            m_new = jnp.maximum(m, cmax)
            alpha = jnp.exp2(m - m_new)
            p = jnp.exp2(s_refs[par][hh] - m_new).astype(BF16)
            acc_ref[hh] = acc_ref[hh] * alpha + _dot(vt_ref[0, hh, cur_chunk], p)
            if next_mask is not None:
                s_next = jnp.where(next_mask, s_next, NEG_BIG)
            s_refs[oth][hh] = s_next
            out.append((m_new, jnp.max(s_next, axis=0, keepdims=True)))
        return tuple(out)

    key = lax.broadcasted_iota(jnp.int32, (tk, tq), 0)
    qry = lax.broadcasted_iota(jnp.int32, (tk, tq), 1)
    init = []
    for hh in range(hb):
        s_first = jnp.where(key <= qry, scores(hh, nd * i), NEG_BIG)
        s0_ref[hh] = s_first
        acc_ref[hh] = jnp.zeros((ATT_V_ROWS, tq), F32)
        init.append((jnp.full((1, tq), NEG_BIG, F32), jnp.max(s_first, axis=0, keepdims=True)))
    carry = tuple(init)
    below = nd * i
    for d in range(nd):
        if d + 1 < nd:
            carry = visit(d % 2, carry, below + d, below + d + 1, key + (d + 1) * tk <= qry)
        else:
            carry = visit(d % 2, carry, below + d, jnp.maximum(below - 1, 0))

    lane = lax.broadcasted_iota(jnp.int32, (1, LANES), 1)
    bound = jnp.full((1, LANES), NEG_BIG, F32)
    for hh in range(hb):
        h = pl.program_id(1) * hb + hh
        qf = qt_ref[0, hh, 0][:ATT_HEAD_DIM, :].astype(F32)
        qmax = jnp.sqrt(jnp.max(jnp.sum(qf * qf, axis=0, keepdims=True), axis=-1, keepdims=True))
        mmin = jnp.min(carry[hh][0], axis=-1, keepdims=True)
        bound = jnp.where(lane == h, qmax * kmax_ref[0, 0:1, :] - mmin, bound)
    chunk_id = lax.broadcasted_iota(jnp.int32, bpre_ref.shape[1:], 0)
    live = (bpre_ref[0] + bound >= -ATT_SKIP_EXPONENT) & (chunk_id < below)
    n_live = jnp.max(jnp.sum(jnp.where(live, 1.0, 0.0), axis=0, keepdims=True)).astype(jnp.int32)
    n_pairs = (n_live + 1) // 2

    def pair(w, carry):
        top = below - 2 * w
        carry = visit(0, carry, top - 1, top - 2)
        return visit(1, carry, top - 2, jnp.maximum(top - 3, 0))

    def two_pairs(w2, carry):
        return pair(2 * w2 + 1, pair(2 * w2, carry))

    carry = lax.fori_loop(0, n_pairs // 2, two_pairs, carry)
    carry = lax.fori_loop(2 * (n_pairs // 2), n_pairs, pair, carry)
    pad_rows = jnp.zeros((LANES - ATT_V_ROWS, tq), F32)
    for pr in range(hb // 2):
        halves = []
        for hh in (2 * pr, 2 * pr + 1):
            acc_t = jnp.concatenate([acc_ref[hh], pad_rows], axis=0).T
            halves.append(acc_t / acc_t[:, ATT_HEAD_DIM:ATT_HEAD_DIM + 1])
        both = jnp.where(lane < ATT_HEAD_DIM, halves[0],
                         pltpu.roll(halves[1], ATT_HEAD_DIM, axis=1))
        o_ref[0, :, pr * LANES:(pr + 1) * LANES] = both.astype(o_ref.dtype)


def _attention(qt, k, vt, kmax, bpre, tq, tk, hb):
    b, s, _ = k.shape
    assert tq % (2 * tk) == 0 and bpre.shape == (b, s // tk, LANES)
    return pl.pallas_call(
        functools.partial(_att_kernel, tq=tq, tk=tk, hb=hb),
        out_shape=jax.ShapeDtypeStruct((b, s, ATT_WIDTH), BF16),
        grid=(b, ATT_HEADS // hb, s // tq),
        in_specs=[
            pl.BlockSpec((1, hb, 1, ATT_V_ROWS, tq), lambda bi, hg, i: (bi, hg, i, 0, 0)),
            pl.BlockSpec((1, s, hb * LANES), lambda bi, hg, i: (bi, 0, hg),
                         pipeline_mode=pl.Buffered(1)),
            pl.BlockSpec((1, hb, s // tk, ATT_V_ROWS, tk), lambda bi, hg, i: (bi, hg, 0, 0, 0),
                         pipeline_mode=pl.Buffered(1)),
            pl.BlockSpec((1, SUBLANES, LANES), lambda bi, hg, i: (bi, 0, 0)),
            pl.BlockSpec((1, s // tk, LANES), lambda bi, hg, i: (bi, 0, 0)),
        ],
        out_specs=pl.BlockSpec((1, tq, hb * ATT_HEAD_DIM), lambda bi, hg, i: (bi, i, hg)),
        scratch_shapes=[pltpu.VMEM((hb, tk, tq), F32), pltpu.VMEM((hb, tk, tq), F32),
                        pltpu.VMEM((hb, ATT_V_ROWS, tq), F32)],
        compiler_params=_cparams(("parallel", "parallel", "arbitrary")),
        name="fox_attention",
    )(qt, k, vt, kmax, bpre)


def _mlstm_kernel(q_ref, k_ref, v_ref, gcol_ref, grow_ref, spread_ref, nw_ref, o_ref,
                  st_ref, m_ref):
    @pl.when(pl.program_id(1) == 0)
    def _():
        st_ref[...] = jnp.zeros_like(st_ref)
        m_ref[...] = jnp.zeros_like(m_ref)

    L = CHUNK
    D = ML_HEAD_DIM
    H = range(ML_HEADS)
    r = lax.broadcasted_iota(jnp.int32, (L, L), 0)
    c = lax.broadcasted_iota(jnp.int32, (L, L), 1)
    causal = c <= r
    ones = jnp.ones((L, D), BF16)
    mean_mat = jnp.full((D, D), 1.0 / D, BF16)
    two = lambda x: jnp.concatenate([x, x], axis=-1)
    head = lambda ref, h: ref[0, :, h * D:(h + 1) * D]

    rep = _dot(jnp.concatenate(_split3(gcol_ref[0]), axis=-1), spread_ref[...])
    col = lambda j, h: rep[:, (j * ML_HEADS + h) * LANES:(j * ML_HEADS + h + 1) * LANES]
    qk = [_dot_nt(head(q_ref, h), head(k_ref, h)) for h in H]
    st = [st_ref[h] for h in H]
    inter = [_dot(head(q_ref, h), st[h].astype(BF16)) for h in H]
    kt = [head(k_ref, h).astype(F32).T.astype(BF16) for h in H]

    m_t, w_intra, w_inter, decay, wv = [], [], [], [], []
    for h in H:
        b, gq, cm = col(0, h), col(1, h), col(2, h)
        gq_row = grow_ref[0, ML_HEADS + h:ML_HEADS + h + 1, :]
        m = m_ref[h]
        a = b + m
        m_t.append(jnp.maximum(a, b + cm))
        w_intra.append(jnp.exp(jnp.where(causal, b + gq_row, NEG_BIG) - m_t[h]))
        w_inter.append(jnp.exp(a - m_t[h]))
        b_last = b[L - 1:L, :]
        m_new = jnp.maximum(b_last + m, b_last + cm[L - 1:L, :])
        w_s = jnp.exp(b_last + gq - m_new)
        decay.append(jnp.exp(b_last + m - m_new))
        wv.append(jnp.concatenate([w_s * head(v_ref, h).astype(F32), w_s], axis=-1).astype(BF16))
        m_ref[h] = m_new

    qkw = [(qk[h] * w_intra[h]).astype(BF16) for h in H]
    both = [_dot(qkw[h], jnp.concatenate([head(v_ref, h), ones], axis=-1))
            + two(w_inter[h]) * inter[h] for h in H]
    for h in H:
        st_ref[h] = two(decay[h]) * st[h] + _dot(kt[h], wv[h])
    hh = [both[h][:, :D] / jnp.maximum(jnp.abs(both[h][:, D:]), jnp.exp(-m_t[h])) for h in H]
    ms = []
    for h in H:
        sq = hh[h] * hh[h]
        sq_hi = sq.astype(BF16)
        sq_lo = (sq - sq_hi.astype(F32)).astype(BF16)
        ms.append(_dot(sq_hi, mean_mat) + _dot(sq_lo, mean_mat))
    for h in H:
        o_ref[0, :, h * D:(h + 1) * D] = (hh[h] * lax.rsqrt(ms[h] + EPS)
                                          * nw_ref[:, h * D:(h + 1) * D])


def _mlstm(mq, mk, mv, gcol, grow, norm_w, l):
    b, s, _ = mq.shape
    nc = s // CHUNK
    row = lambda w: pl.BlockSpec((1, CHUNK, w), lambda bi, ci: (bi, ci, 0))
    src = lax.broadcasted_iota(jnp.int32, (3 * LANES, ML_REP), 0) % LANES
    dst = lax.broadcasted_iota(jnp.int32, (3 * LANES, ML_REP), 1) // LANES
    spread = (src == dst).astype(BF16)
    return pl.pallas_call(
        _mlstm_kernel,
        out_shape=jax.ShapeDtypeStruct((b, s, ML_WIDTH), F32),
        grid=(b, nc),
        in_specs=[
            row(ML_WIDTH), row(ML_WIDTH), row(ML_WIDTH), row(LANES),
            pl.BlockSpec((1, GATE_ROWS, CHUNK), lambda bi, ci: (bi, 0, ci)),
            _resident((3 * LANES, ML_REP)),
            _resident((1, ML_WIDTH), l),
        ],
        out_specs=row(ML_WIDTH),
        scratch_shapes=[
            pltpu.VMEM((ML_HEADS, ML_HEAD_DIM, 2 * ML_HEAD_DIM), F32),
            pltpu.VMEM((ML_HEADS, 1, LANES), F32),
        ],
        compiler_params=_cparams(("parallel", "arbitrary")),
        name="mlstm",
    )(mq, mk, mv, gcol, grow, spread, norm_w)


def _gelu_tanh(x):
    return 0.5 * x * (1.0 + jnp.tanh(0.7978845608028654 * (x + 0.044715 * (x * x * x))))


def _merge_kernel(x_ref, yatt_ref, hm_ref, g_ref, wo_ref, wuv_ref, wgt_ref, ng_ref,
                  wsp_ref, bsp_ref, wba_ref, wbm_ref, wbg_ref, wout_ref, o_ref, *, tm):
    x = x_ref[...]
    hn = _rms(x, g_ref[...]).astype(BF16)

    y_ml = jax.nn.sigmoid(_dot(hn, wo_ref[...])) * hm_ref[...]

    uv = _gelu_tanh(_dot(hn, wuv_ref[...]))
    u = uv[:, :GM_WIDTH]
    vn = _rms(uv[:, GM_WIDTH:], ng_ref[...]).astype(BF16)
    r = lax.broadcasted_iota(jnp.int32, (CHUNK, CHUNK), 0)
    c = lax.broadcasted_iota(jnp.int32, (CHUNK, CHUNK), 1)
    mixed = []
    for gi in range(GM_GROUPS):
        w = jnp.where(c <= r, wsp_ref[gi], 0.0).astype(BF16)
        lanes = slice(gi * GM_GROUP_DIM, (gi + 1) * GM_GROUP_DIM)
        mixed.append(jnp.concatenate(
            [_dot(w, vn[ci * CHUNK:(ci + 1) * CHUNK, lanes]) for ci in range(tm // CHUNK)], axis=0))
    y_gm = u * (jnp.concatenate(mixed, axis=-1) + bsp_ref[...])

    gates = jax.nn.sigmoid(_dot(hn, wgt_ref[...]))
    merged = (gates[:, :D_MODEL] * _dot(yatt_ref[...], wba_ref[...])
              + gates[:, D_MODEL:2 * D_MODEL] * _dot(y_ml.astype(BF16), wbm_ref[...])
              + gates[:, 2 * D_MODEL:] * _dot(y_gm.astype(BF16), wbg_ref[...]))
    o_ref[...] = x + _dot(merged.astype(BF16), wout_ref[...])


def _merge(x2, yatt, hm, g, wo, wuv, wgt, ng, wsp, bsp, wba, wbm, wbg, wout, l, tm):
    n = x2.shape[0]
    row = lambda w: pl.BlockSpec((tm, w), lambda i: (i, 0))
    return pl.pallas_call(
        functools.partial(_merge_kernel, tm=tm),
        out_shape=jax.ShapeDtypeStruct((n, D_MODEL), F32),
        grid=(n // tm,),
        in_specs=[
            row(D_MODEL), row(ATT_WIDTH), row(ML_WIDTH),
            _resident((1, D_MODEL), l),
            _resident((D_MODEL, ML_WIDTH), l),
            _resident((D_MODEL, 2 * GM_WIDTH), l),
            _resident((D_MODEL, N_BRANCH * D_MODEL), l),
            _resident((1, GM_WIDTH), l),
            _resident((GM_GROUPS, CHUNK, CHUNK), l),
            _resident((tm, GM_WIDTH), l),
            _resident((ATT_WIDTH, D_MODEL), l),
            _resident((ML_WIDTH, D_MODEL), l),
            _resident((GM_WIDTH, D_MODEL), l),
            _resident((D_MODEL, D_MODEL), l),
        ],
        out_specs=row(D_MODEL),
        compiler_params=_cparams(("parallel",)),
        name="merge",
    )(x2, yatt, hm, g, wo, wuv, wgt, ng, wsp, bsp, wba, wbm, wbg, wout)


def _tile(n, pref):
    t = min(n, pref)
    assert n % t == 0, (n, t)
    return t


def _prepare_layer_params(p, tm_merge):
    depth = p["w_in"].shape[0]
    o_att = 3 * ATT_WIDTH
    o_mqk = o_att + ATT_HEADS
    o_mv = o_mqk + 2 * ML_WIDTH
    o_mi = o_mv + ML_WIDTH
    o_mf = o_mi + ML_HEADS
    o_mo = o_mf + ML_HEADS
    o_uv = o_mo + ML_WIDTH
    o_gt = o_uv + 2 * GM_WIDTH
    w_in = p["w_in"]
    w_small = jnp.concatenate(
        [w_in[:, :, o_att:o_mqk], w_in[:, :, o_mi:o_mf], w_in[:, :, o_mf:o_mo],
         jnp.zeros((depth, D_MODEL, LANES - GATE_ROWS), F32)], axis=-1)
    b_small = jnp.concatenate(
        [p["b_f_att"], p["b_i_ml"], p["b_f_ml"], jnp.zeros((depth, LANES - GATE_ROWS), F32)],
        axis=-1)[:, None, :]
    reps = tm_merge // CHUNK
    bf = lambda a: a.astype(BF16)
    row = lambda a: a[:, None, :]

    def slots(w, width):
        w = w.reshape(depth, D_MODEL, ATT_HEADS, ATT_HEAD_DIM)
        w = jnp.pad(w, ((0, 0), (0, 0), (0, 0), (0, width - ATT_HEAD_DIM)))
        return w.reshape(depth, D_MODEL, ATT_HEADS * width)

    w_q = slots(w_in[:, :, :ATT_WIDTH], ATT_V_ROWS) * (ATT_HEAD_DIM ** -0.5 * LOG2E)
    w_k = slots(w_in[:, :, ATT_WIDTH:2 * ATT_WIDTH], LANES)
    w_v = slots(w_in[:, :, 2 * ATT_WIDTH:o_att], ATT_V_ROWS)
    return dict(
        norm_ffn1=row(p["norm_ffn1"]), ffn1_gate=bf(p["ffn1_gate"]), ffn1_up=bf(p["ffn1_up"]),
        ffn1_down=bf(p["ffn1_down"]),
        norm_mix=row(p["norm_mix"]),
        w_qt=bf(jnp.swapaxes(w_q, 1, 2)), w_k=bf(w_k), w_vt=bf(jnp.swapaxes(w_v, 1, 2)),
        w_small=bf(w_small), b_small=b_small,
        w_mqk=bf(w_in[:, :, o_mqk:o_mv]), w_mv=bf(w_in[:, :, o_mv:o_mi]),
        w_mo=bf(w_in[:, :, o_mo:o_uv]), w_uv=bf(w_in[:, :, o_uv:o_gt]), w_gt=bf(w_in[:, :, o_gt:]),
        conv_ml=p["conv_ml"], norm_ml_head=row(p["norm_ml_head"]), norm_gmlp=row(p["norm_gmlp"]),
        w_spatial=p["w_spatial"],
        b_spatial=jnp.tile(jnp.repeat(jnp.swapaxes(p["b_spatial"], 1, 2), GM_GROUP_DIM, axis=2),
                           (1, reps, 1)),
        w_br_att=bf(p["w_br_att"]), w_br_ml=bf(p["w_br_ml"]), w_br_gmlp=bf(p["w_br_gmlp"]),
        w_out=bf(p["w_out"]),
        norm_ffn2=row(p["norm_ffn2"]), ffn2_gate=bf(p["ffn2_gate"]), ffn2_up=bf(p["ffn2_up"]),
        ffn2_down=bf(p["ffn2_down"]),
    )


def _layer(x2, lp, l, b, s, tiles, final_g):
    n = b * s
    x2 = _ffn(x2, lp["norm_ffn1"], lp["ffn1_gate"], lp["ffn1_up"], lp["ffn1_down"], l,
              tiles["ffn"])
    qt, k, vt, small, mq, mk, mv = _inproj(
        x2.reshape(b, s, D_MODEL), lp["norm_mix"], lp["w_qt"], lp["w_k"], lp["w_vt"],
        lp["w_small"], lp["w_mqk"], lp["w_mv"], lp["conv_ml"], l, tiles["inproj"], tiles["att_q"],
        tiles["att_k"])
    gcol, grow, k, kmax, bpre = _gates(small, lp["b_small"], k, l, tiles["gates"], tiles["att_k"])
    bpre = bpre[:, :, :tiles["gates"] // tiles["att_k"], :].reshape(b, s // tiles["att_k"], LANES)
    yatt = _attention(qt, k, vt, kmax, bpre, tiles["att_q"], tiles["att_k"], tiles["att_heads"])
    hm = _mlstm(mq, mk, mv, gcol, grow, lp["norm_ml_head"], l)
    x2 = _merge(x2, yatt.reshape(n, ATT_WIDTH), hm.reshape(n, ML_WIDTH), lp["norm_mix"],
                lp["w_mo"], lp["w_uv"], lp["w_gt"], lp["norm_gmlp"], lp["w_spatial"],
                lp["b_spatial"], lp["w_br_att"], lp["w_br_ml"], lp["w_br_gmlp"], lp["w_out"],
                l, tiles["merge"])
    return _ffn(x2, lp["norm_ffn2"], lp["ffn2_gate"], lp["ffn2_up"], lp["ffn2_down"], l,
                tiles["ffn"], final_g)


def _tiles_for(n, s):
    att_q = _tile(s, 512)
    return dict(ffn=_tile(n, 512), inproj=_tile(s, 1024), gates=_tile(s, 512), att_q=att_q,
                att_k=min(att_q // 2, 256), att_heads=2, merge=_tile(s, 512))


def _trunk(x, params, norm_final):
    b, s, _ = x.shape
    n = b * s
    tiles = _tiles_for(n, s)
    stacked = _prepare_layer_params(params, tiles["merge"])
    depth = params["w_in"].shape[0]
    x2 = x.reshape(n, D_MODEL)
    for l in range(depth):
        x2 = _layer(x2, stacked, l, b, s, tiles, norm_final[None, :] if l == depth - 1 else None)
    return x2.reshape(b, s, D_MODEL)


def kernel(x, norm_ffn1, ffn1_gate, ffn1_up, ffn1_down, norm_mix, w_in, b_f_att, b_i_ml, b_f_ml, conv_ml, norm_ml_head, norm_gmlp, w_spatial, b_spatial, w_br_att, w_br_ml, w_br_gmlp, w_out, norm_ffn2, ffn2_gate, ffn2_up, ffn2_down, norm_final):
    params = dict(norm_ffn1=norm_ffn1, ffn1_gate=ffn1_gate, ffn1_up=ffn1_up, ffn1_down=ffn1_down,
                  norm_mix=norm_mix, w_in=w_in, b_f_att=b_f_att, b_i_ml=b_i_ml, b_f_ml=b_f_ml,
                  conv_ml=conv_ml, norm_ml_head=norm_ml_head, norm_gmlp=norm_gmlp,
                  w_spatial=w_spatial, b_spatial=b_spatial, w_br_att=w_br_att, w_br_ml=w_br_ml,
                  w_br_gmlp=w_br_gmlp, w_out=w_out, norm_ffn2=norm_ffn2, ffn2_gate=ffn2_gate,
                  ffn2_up=ffn2_up, ffn2_down=ffn2_down)
    return _trunk(x, params, norm_final)
```

```python
import functools

import jax
import jax.numpy as jnp
from jax import lax
from jax.experimental import pallas as pl
from jax.experimental.pallas import tpu as pltpu

D_MODEL = 1024
ATT_HEADS = 8
ATT_HEAD_DIM = 64
ATT_WIDTH = ATT_HEADS * ATT_HEAD_DIM
ML_HEADS = 4
ML_HEAD_DIM = 128
ML_WIDTH = ML_HEADS * ML_HEAD_DIM
CONV_WIDTH = 4
GM_GROUPS = 4
GM_GROUP_DIM = 128
GM_WIDTH = GM_GROUPS * GM_GROUP_DIM
CHUNK = 128
D_FF = 2816
FFN_RES = 0.5
N_BRANCH = 3
EPS = 1e-6

LANES = 128
SUBLANES = 8
ATT_SLOTS = ATT_HEADS * LANES
ATT_V_ROWS = 80
LOG2E = 1.4426950408889634
ATT_SKIP_EXPONENT = 160.0
VMEM_LIMIT = 56 * 1024 * 1024
NEG_BIG = -1e30

GATE_ATT_F = 0
GATE_ML_I = ATT_HEADS
GATE_ML_F = ATT_HEADS + ML_HEADS
GATE_ROWS = 16
ML_REP = 3 * ML_HEADS * 128

F32 = jnp.float32
BF16 = jnp.bfloat16


def _cparams(sem):
    return pltpu.CompilerParams(dimension_semantics=sem, vmem_limit_bytes=VMEM_LIMIT)


def _resident(shape, layer=None):
    nd = len(shape)
    if layer is None:
        return pl.BlockSpec(shape, lambda *_: (0,) * nd, pipeline_mode=pl.Buffered(1))
    return pl.BlockSpec((None,) + tuple(shape), lambda *_: (layer,) + (0,) * nd,
                        pipeline_mode=pl.Buffered(1))


def _rms(x, g):
    ms = jnp.mean(x * x, axis=-1, keepdims=True)
    return x * lax.rsqrt(ms + EPS) * g


def _log_sigmoid(z):
    return jnp.minimum(z, 0.0) - jnp.log1p(jnp.exp(-jnp.abs(z)))


def _dot(a, b):
    return jnp.dot(a, b, preferred_element_type=F32)


def _dot_nt(a, b):
    return lax.dot_general(a, b, (((1,), (1,)), ((), ())), preferred_element_type=F32)


def _ffn_kernel(x_ref, g_ref, wg_ref, wu_ref, wd_ref, *rest, final):
    o_ref = rest[-1]
    x = x_ref[...]
    hn = _rms(x, g_ref[...]).astype(BF16)
    gate = _dot(hn, wg_ref[...])
    up = _dot(hn, wu_ref[...])
    act = (gate * jax.nn.sigmoid(gate) * up).astype(BF16)
    y = x + FFN_RES * _dot(act, wd_ref[...])
    o_ref[...] = _rms(y, rest[0][...]) if final else y


def _ffn(x2, g, wg, wu, wd, l, tm, final_g=None):
    n = x2.shape[0]
    final = final_g is not None
    return pl.pallas_call(
        functools.partial(_ffn_kernel, final=final),
        out_shape=jax.ShapeDtypeStruct((n, D_MODEL), F32),
        grid=(n // tm,),
        in_specs=[
            pl.BlockSpec((tm, D_MODEL), lambda i: (i, 0)),
            _resident((1, D_MODEL), l),
            _resident((D_MODEL, D_FF), l),
            _resident((D_MODEL, D_FF), l),
            _resident((D_FF, D_MODEL), l),
        ] + ([_resident((1, D_MODEL))] if final else []),
        out_specs=pl.BlockSpec((tm, D_MODEL), lambda i: (i, 0)),
        compiler_params=_cparams(("parallel",)),
        name="ffn_final" if final else "ffn",
    )(x2, g, wg, wu, wd, *([final_g] if final else []))


def _inproj_kernel(x_ref, xp_ref, g_ref, wqt_ref, wk_ref, wvt_ref, wsm_ref, wqk_ref, wv_ref,
                   cw_ref, qt_ref, k_ref, vt_ref, sm_ref, mq_ref, mk_ref, mv_ref, *, tm, tq, tk):
    hn = _rms(x_ref[0], g_ref[...])
    hb = hn.astype(BF16)
    hnt = hn.T.astype(BF16)
    slot_row = lax.broadcasted_iota(jnp.int32, (ATT_HEADS * ATT_V_ROWS, tm), 0) % ATT_V_ROWS
    qt = _dot(wqt_ref[...], hnt)
    qt = jnp.where((slot_row >= ATT_HEAD_DIM) & (slot_row < ATT_HEAD_DIM + 3), 1.0, qt)
    qt = qt.astype(BF16)
    vt = _dot(wvt_ref[...], hnt)
    vt = jnp.where(slot_row == ATT_HEAD_DIM, 1.0, vt).astype(BF16)
    for c in range(tm // tq):
        qt_ref[0, :, c] = qt[:, c * tq:(c + 1) * tq].reshape(ATT_HEADS, ATT_V_ROWS, tq)
    for c in range(tm // tk):
        vt_ref[0, :, c] = vt[:, c * tk:(c + 1) * tk].reshape(ATT_HEADS, ATT_V_ROWS, tk)
    k_ref[0] = _dot(hb, wk_ref[...]).astype(BF16)
    sm_ref[0] = _dot(hb, wsm_ref[...])
    mv_ref[0] = _dot(hb, wv_ref[...]).astype(BF16)

    hn_before = _rms(xp_ref[0], g_ref[...])
    hn_before = jnp.where(pl.program_id(1) > 0, hn_before, 0.0)
    xcat = _dot(jnp.concatenate([hn_before, hn], axis=0).astype(BF16), wqk_ref[...])
    proj = xcat[SUBLANES:, :]
    cw = cw_ref[...]
    conv = proj * cw[CONV_WIDTH - 1:CONV_WIDTH, :]
    for j in range(CONV_WIDTH - 1):
        off = SUBLANES - (CONV_WIDTH - 1) + j
        conv = conv + xcat[off:off + tm, :] * cw[j:j + 1, :]
    act = conv * jax.nn.sigmoid(conv)
    mq_ref[0] = (act[:, :ML_WIDTH] * (ML_HEAD_DIM ** -0.5)).astype(BF16)
    mk_ref[0] = act[:, ML_WIDTH:].astype(BF16)


def _inproj(x3, g, wqt, wk, wvt, wsm, wqk, wv, cw, l, tm, tq, tk):
    b, s, _ = x3.shape
    row = lambda w: pl.BlockSpec((1, tm, w), lambda bi, si: (bi, si, 0))
    rows_before = pl.BlockSpec((1, SUBLANES, D_MODEL),
                               lambda bi, si: (bi, jnp.maximum(si * (tm // SUBLANES) - 1, 0), 0))
    tr = lambda rows, t: pl.BlockSpec((1, ATT_HEADS, tm // t, rows, t),
                                      lambda bi, si: (bi, 0, si, 0, 0))
    tr_shape = lambda rows, t: jax.ShapeDtypeStruct((b, ATT_HEADS, s // t, rows, t), BF16)
    return pl.pallas_call(
        functools.partial(_inproj_kernel, tm=tm, tq=tq, tk=tk),
        out_shape=(
            tr_shape(ATT_V_ROWS, tq),
            jax.ShapeDtypeStruct((b, s, ATT_SLOTS), BF16),
            tr_shape(ATT_V_ROWS, tk),
            jax.ShapeDtypeStruct((b, s, LANES), F32),
            jax.ShapeDtypeStruct((b, s, ML_WIDTH), BF16),
            jax.ShapeDtypeStruct((b, s, ML_WIDTH), BF16),
            jax.ShapeDtypeStruct((b, s, ML_WIDTH), BF16),
        ),
        grid=(b, s // tm),
        in_specs=[
            row(D_MODEL),
            rows_before,
            _resident((1, D_MODEL), l),
            _resident((ATT_HEADS * ATT_V_ROWS, D_MODEL), l),
            _resident((D_MODEL, ATT_SLOTS), l),
            _resident((ATT_HEADS * ATT_V_ROWS, D_MODEL), l),
            _resident((D_MODEL, LANES), l),
            _resident((D_MODEL, 2 * ML_WIDTH), l),
            _resident((D_MODEL, ML_WIDTH), l),
            _resident((CONV_WIDTH, 2 * ML_WIDTH), l),
        ],
        out_specs=(tr(ATT_V_ROWS, tq), row(ATT_SLOTS), tr(ATT_V_ROWS, tk), row(LANES),
                   row(ML_WIDTH), row(ML_WIDTH), row(ML_WIDTH)),
        compiler_params=_cparams(("parallel", "parallel")),
        name="inproj",
    )(x3, x3, g, wqt, wk, wvt, wsm, wqk, wv, cw)


def _split3(x):
    hi = x.astype(BF16)
    r1 = x - hi.astype(F32)
    mid = r1.astype(BF16)
    lo = (r1 - mid.astype(F32)).astype(BF16)
    return hi, mid, lo


def _place(x, moves):
    r = lax.broadcasted_iota(jnp.int32, (LANES, LANES), 0)
    c = lax.broadcasted_iota(jnp.int32, (LANES, LANES), 1)
    mat = jnp.zeros((LANES, LANES), F32)
    for src, dst, sign in moves:
        mat = jnp.where((r == src) & (c == dst), sign, mat)
    mat3 = jnp.concatenate([mat.astype(BF16)] * 3, axis=0)
    return _dot(jnp.concatenate(_split3(x), axis=-1), mat3)


def _gates_kernel(sm_ref, bias_ref, k_ref, col_ref, row_ref, ka_ref, kmax_ref, bpre_ref,
                  carry_ref, bcarry_ref, *, ts, tk):
    @pl.when(pl.program_id(1) == 0)
    def _():
        carry_ref[...] = jnp.zeros_like(carry_ref)
        bcarry_ref[...] = jnp.full_like(bcarry_ref, NEG_BIG)
        kmax_ref[...] = jnp.zeros_like(kmax_ref)

    z = sm_ref[0] + bias_ref[...]
    lane = lax.broadcasted_iota(jnp.int32, z.shape, 1)
    is_att = lane < GATE_ML_I
    is_mlf = (lane >= GATE_ML_F) & (lane < GATE_ML_F + ML_HEADS)
    vals = jnp.where(is_att | is_mlf, _log_sigmoid(z), z)

    r = lax.broadcasted_iota(jnp.int32, (ts, ts), 0)
    c = lax.broadcasted_iota(jnp.int32, (ts, ts), 1)
    tri = c <= r
    tri_full = jnp.where(tri, 1.0, 0.0).astype(BF16)
    tri_chunk = jnp.where(tri & ((r // CHUNK) == (c // CHUNK)), 1.0, 0.0).astype(BF16)
    pieces = jnp.concatenate(_split3(vals), axis=-1)
    add3 = lambda y: y[:, :LANES] + y[:, LANES:2 * LANES] + y[:, 2 * LANES:]
    cs_full = add3(_dot(tri_full, pieces)) + carry_ref[...]
    cs_chunk = add3(_dot(tri_chunk, pieces))
    carry_ref[...] = cs_full[ts - 1:ts, :]

    li_moves, b_moves = [], []
    for h in range(ML_HEADS):
        li_moves += [(GATE_ML_I + h, ML_HEADS + h, 1.0), (GATE_ML_I + h, 2 * ML_HEADS + h, 1.0)]
        b_moves += [(GATE_ML_F + h, h, 1.0), (GATE_ML_F + h, ML_HEADS + h, -1.0),
                    (GATE_ML_F + h, 2 * ML_HEADS + h, -1.0)]
    g = _place(vals, li_moves) + _place(cs_chunk, b_moves)
    pos = lax.broadcasted_iota(jnp.int32, (ts, LANES), 0) % CHUNK
    cmax = g
    d = 1
    while d < CHUNK:
        cmax = jnp.where(pos >= d, jnp.maximum(cmax, pltpu.roll(cmax, d, axis=0)), cmax)
        d *= 2
    g = jnp.where(lane >= 2 * ML_HEADS, cmax, g)
    col_ref[0] = g
    row_ref[0] = g.T[:GATE_ROWS, :]

    src = lax.broadcasted_iota(jnp.int32, (3 * LANES, ATT_SLOTS), 0)
    dst = lax.broadcasted_iota(jnp.int32, (3 * LANES, ATT_SLOTS), 1)
    place = (src % LANES < ATT_HEADS) & (dst == (src % LANES) * LANES + ATT_HEAD_DIM + src // LANES)
    decay = cs_full * -LOG2E
    decay_cols = _dot(jnp.concatenate(_split3(decay), axis=-1),
                      jnp.where(place, 1.0, 0.0).astype(BF16))
    kf = k_ref[0].astype(F32)
    ka_ref[0] = (kf + decay_cols).astype(BF16)

    lane_row = lax.broadcasted_iota(jnp.int32, (1, LANES), 1)
    knorm = jnp.zeros((1, LANES), F32)
    for h in range(ATT_HEADS):
        kh = kf[:, h * LANES:(h + 1) * LANES]
        sq = jnp.max(jnp.sum(kh * kh, axis=-1, keepdims=True), axis=0, keepdims=True)
        knorm = jnp.where(lane_row == h, jnp.sqrt(sq), knorm)
    kmax_ref[0] = jnp.maximum(kmax_ref[0], jnp.broadcast_to(knorm, (SUBLANES, LANES)))

    rows = []
    running = bcarry_ref[...]
    for c in range(ts // tk):
        running = jnp.maximum(running, jnp.max(decay[c * tk:(c + 1) * tk, :], axis=0, keepdims=True))
        rows.append(running)
    bcarry_ref[...] = running
    rows += [running] * (SUBLANES - len(rows))
    bpre_ref[0, 0] = jnp.concatenate(rows, axis=0)


def _gates(small, bias, k, l, ts, tk):
    b, s, _ = small.shape
    assert ts % tk == 0 and ts // tk <= SUBLANES
    row = lambda w: pl.BlockSpec((1, ts, w), lambda bi, si: (bi, si, 0))
    return pl.pallas_call(
        functools.partial(_gates_kernel, ts=ts, tk=tk),
        out_shape=(jax.ShapeDtypeStruct((b, s, LANES), F32),
                   jax.ShapeDtypeStruct((b, GATE_ROWS, s), F32),
                   jax.ShapeDtypeStruct((b, s, ATT_SLOTS), BF16),
                   jax.ShapeDtypeStruct((b, SUBLANES, LANES), F32),
                   jax.ShapeDtypeStruct((b, s // ts, SUBLANES, LANES), F32)),
        grid=(b, s // ts),
        in_specs=[row(LANES), _resident((1, LANES), l), row(ATT_SLOTS)],
        out_specs=(row(LANES),
                   pl.BlockSpec((1, GATE_ROWS, ts), lambda bi, si: (bi, 0, si)),
                   row(ATT_SLOTS),
                   pl.BlockSpec((1, SUBLANES, LANES), lambda bi, si: (bi, 0, 0)),
                   pl.BlockSpec((1, 1, SUBLANES, LANES), lambda bi, si: (bi, si, 0, 0))),
        scratch_shapes=[pltpu.VMEM((1, LANES), F32), pltpu.VMEM((1, LANES), F32)],
        compiler_params=_cparams(("parallel", "arbitrary")),
        name="gates",
    )(small, bias, k)


def _att_kernel(qt_ref, k_ref, vt_ref, kmax_ref, bpre_ref, o_ref, s0_ref, s1_ref, acc_ref,
                *, tq, tk, hb):
    i = pl.program_id(2)
    nd = tq // tk
    s_refs = (s0_ref, s1_ref)

    q_pad = jnp.zeros((LANES - ATT_V_ROWS, tq), BF16)
    q_slots = [jnp.concatenate([qt_ref[0, hh, 0], q_pad], axis=0) for hh in range(hb)]

    def scores(hh, chunk):
        kj = k_ref[0, pl.ds(pl.multiple_of(chunk * tk, tk), tk), hh * LANES:(hh + 1) * LANES]
        return _dot(kj, q_slots[hh])

    def visit(par, carry, cur_chunk, next_chunk, next_mask=None):
        oth = 1 - par
        out = []
        for hh in range(hb):
            m, cmax = carry[hh]
            s_next = scores(hh, next_chunk)
            m_new = jnp.maximum(m, cmax)
            alpha = jnp.exp2(m - m_new)
            p = jnp.exp2(s_refs[par][hh] - m_new).astype(BF16)
            acc_ref[hh] = acc_ref[hh] * alpha + _dot(vt_ref[0, hh, cur_chunk], p)
            if next_mask is not None:
                s_next = jnp.where(next_mask, s_next, NEG_BIG)
            s_refs[oth][hh] = s_next
            out.append((m_new, jnp.max(s_next, axis=0, keepdims=True)))
        return tuple(out)

    key = lax.broadcasted_iota(jnp.int32, (tk, tq), 0)
    qry = lax.broadcasted_iota(jnp.int32, (tk, tq), 1)
    init = []
    for hh in range(hb):
        s_first = jnp.where(key <= qry, scores(hh, nd * i), NEG_BIG)
        s0_ref[hh] = s_first
        acc_ref[hh] = jnp.zeros((ATT_V_ROWS, tq), F32)
        init.append((jnp.full((1, tq), NEG_BIG, F32), jnp.max(s_first, axis=0, keepdims=True)))
    carry = tuple(init)
    below = nd * i
    for d in range(nd):
        if d + 1 < nd:
            carry = visit(d % 2, carry, below + d, below + d + 1, key + (d + 1) * tk <= qry)
        else:
            carry = visit(d % 2, carry, below + d, jnp.maximum(below - 1, 0))

    lane = lax.broadcasted_iota(jnp.int32, (1, LANES), 1)
    bound = jnp.full((1, LANES), NEG_BIG, F32)
    for hh in range(hb):
        h = pl.program_id(1) * hb + hh
        qf = qt_ref[0, hh, 0][:ATT_HEAD_DIM, :].astype(F32)
        qmax = jnp.sqrt(jnp.max(jnp.sum(qf * qf, axis=0, keepdims=True), axis=-1, keepdims=True))
        mmin = jnp.min(carry[hh][0], axis=-1, keepdims=True)
        bound = jnp.where(lane == h, qmax * kmax_ref[0, 0:1, :] - mmin, bound)
    chunk_id = lax.broadcasted_iota(jnp.int32, bpre_ref.shape[1:], 0)
    live = (bpre_ref[0] + bound >= -ATT_SKIP_EXPONENT) & (chunk_id < below)
    n_live = jnp.max(jnp.sum(jnp.where(live, 1.0, 0.0), axis=0, keepdims=True)).astype(jnp.int32)
    n_pairs = (n_live + 1) // 2

    def pair(w, carry):
        top = below - 2 * w
        carry = visit(0, carry, top - 1, top - 2)
        return visit(1, carry, top - 2, jnp.maximum(top - 3, 0))

    def two_pairs(w2, carry):
        return pair(2 * w2 + 1, pair(2 * w2, carry))

    carry = lax.fori_loop(0, n_pairs // 2, two_pairs, carry)
    carry = lax.fori_loop(2 * (n_pairs // 2), n_pairs, pair, carry)
    for pr in range(hb // 2):
        halves = []
        for hh in (2 * pr, 2 * pr + 1):
            acc = acc_ref[hh]
            halves.append(acc[:ATT_HEAD_DIM, :] / acc[ATT_HEAD_DIM:ATT_HEAD_DIM + 1, :])
        both = jnp.concatenate(halves, axis=0).T
        o_ref[0, :, pr * LANES:(pr + 1) * LANES] = both.astype(o_ref.dtype)


def _attention(qt, k, vt, kmax, bpre, tq, tk, hb):
    b, s, _ = k.shape
    assert tq % (2 * tk) == 0 and bpre.shape == (b, s // tk, LANES)
    return pl.pallas_call(
        functools.partial(_att_kernel, tq=tq, tk=tk, hb=hb),
        out_shape=jax.ShapeDtypeStruct((b, s, ATT_WIDTH), BF16),
        grid=(b, ATT_HEADS // hb, s // tq),
        in_specs=[
            pl.BlockSpec((1, hb, 1, ATT_V_ROWS, tq), lambda bi, hg, i: (bi, hg, i, 0, 0)),
            pl.BlockSpec((1, s, hb * LANES), lambda bi, hg, i: (bi, 0, hg),
                         pipeline_mode=pl.Buffered(1)),
            pl.BlockSpec((1, hb, s // tk, ATT_V_ROWS, tk), lambda bi, hg, i: (bi, hg, 0, 0, 0),
                         pipeline_mode=pl.Buffered(1)),
            pl.BlockSpec((1, SUBLANES, LANES), lambda bi, hg, i: (bi, 0, 0)),
            pl.BlockSpec((1, s // tk, LANES), lambda bi, hg, i: (bi, 0, 0)),
        ],
        out_specs=pl.BlockSpec((1, tq, hb * ATT_HEAD_DIM), lambda bi, hg, i: (bi, i, hg)),
        scratch_shapes=[pltpu.VMEM((hb, tk, tq), F32), pltpu.VMEM((hb, tk, tq), F32),
                        pltpu.VMEM((hb, ATT_V_ROWS, tq), F32)],
        compiler_params=_cparams(("parallel", "parallel", "arbitrary")),
        name="fox_attention",
    )(qt, k, vt, kmax, bpre)


def _mlstm_kernel(q_ref, k_ref, v_ref, gcol_ref, grow_ref, spread_ref, nw_ref, o_ref,
                  st_ref, m_ref, *, nsub):
    @pl.when(pl.program_id(1) == 0)
    def _():
        st_ref[...] = jnp.zeros_like(st_ref)
        m_ref[...] = jnp.zeros_like(m_ref)

    L = CHUNK
    D = ML_HEAD_DIM
    H = range(ML_HEADS)
    r = lax.broadcasted_iota(jnp.int32, (L, L), 0)
    c = lax.broadcasted_iota(jnp.int32, (L, L), 1)
    causal = c <= r
    ones = jnp.ones((L, D), BF16)
    mean_mat = jnp.full((D, D), 1.0 / D, BF16)
    two = lambda x: jnp.concatenate([x, x], axis=-1)

    st = [st_ref[h] for h in H]
    m = [m_ref[h] for h in H]
    pre = []
    for sc in range(nsub):
        rows = slice(sc * L, (sc + 1) * L)
        head = lambda ref, h, rows=rows: ref[0, rows, h * D:(h + 1) * D]
        rep = _dot(jnp.concatenate(_split3(gcol_ref[0, rows, :]), axis=-1), spread_ref[...])
        qk = [_dot_nt(head(q_ref, h), head(k_ref, h)) for h in H]
        kt = [head(k_ref, h).astype(F32).T.astype(BF16) for h in H]
        pre.append((rows, head, rep, qk, kt))

    for sc in range(nsub):
        rows, head, rep, qk, kt = pre[sc]
        col = lambda j, h: rep[:, (j * ML_HEADS + h) * LANES:(j * ML_HEADS + h + 1) * LANES]
        inter = [_dot(head(q_ref, h), st[h].astype(BF16)) for h in H]
        m_t, w_intra, w_inter, decay, wv, m_next = [], [], [], [], [], []
        for h in H:
            b, gq, cm = col(0, h), col(1, h), col(2, h)
            gq_row = grow_ref[0, ML_HEADS + h:ML_HEADS + h + 1, rows]
            a = b + m[h]
            m_t.append(jnp.maximum(a, b + cm))
            w_intra.append(jnp.exp(jnp.where(causal, b + gq_row, NEG_BIG) - m_t[h]))
            w_inter.append(jnp.exp(a - m_t[h]))
            b_last = b[L - 1:L, :]
            m_new = jnp.maximum(b_last + m[h], b_last + cm[L - 1:L, :])
            w_s = jnp.exp(b_last + gq - m_new)
            decay.append(jnp.exp(b_last + m[h] - m_new))
            wv.append(jnp.concatenate([w_s * head(v_ref, h).astype(F32), w_s],
                                      axis=-1).astype(BF16))
            m_next.append(m_new)
        qkw = [(qk[h] * w_intra[h]).astype(BF16) for h in H]
        both = [_dot(qkw[h], jnp.concatenate([head(v_ref, h), ones], axis=-1))
                + two(w_inter[h]) * inter[h] for h in H]
        st = [two(decay[h]) * st[h] + _dot(kt[h], wv[h]) for h in H]
        m = m_next
        hh = [both[h][:, :D] / jnp.maximum(jnp.abs(both[h][:, D:]), jnp.exp(-m_t[h])) for h in H]
        ms = []
        for h in H:
            sq = hh[h] * hh[h]
            sq_hi = sq.astype(BF16)
            sq_lo = (sq - sq_hi.astype(F32)).astype(BF16)
            ms.append(_dot(sq_hi, mean_mat) + _dot(sq_lo, mean_mat))
        for h in H:
            o_ref[0, rows, h * D:(h + 1) * D] = (hh[h] * lax.rsqrt(ms[h] + EPS)
                                                 * nw_ref[:, h * D:(h + 1) * D])
    for h in H:
        st_ref[h] = st[h]
        m_ref[h] = m[h]


def _mlstm(mq, mk, mv, gcol, grow, norm_w, l, nsub):
    b, s, _ = mq.shape
    rows = nsub * CHUNK
    nc = s // rows
    row = lambda w: pl.BlockSpec((1, rows, w), lambda bi, ci: (bi, ci, 0))
    src = lax.broadcasted_iota(jnp.int32, (3 * LANES, ML_REP), 0) % LANES
    dst = lax.broadcasted_iota(jnp.int32, (3 * LANES, ML_REP), 1) // LANES
    spread = (src == dst).astype(BF16)
    return pl.pallas_call(
        functools.partial(_mlstm_kernel, nsub=nsub),
        out_shape=jax.ShapeDtypeStruct((b, s, ML_WIDTH), F32),
        grid=(b, nc),
        in_specs=[
            row(ML_WIDTH), row(ML_WIDTH), row(ML_WIDTH), row(LANES),
            pl.BlockSpec((1, GATE_ROWS, rows), lambda bi, ci: (bi, 0, ci)),
            _resident((3 * LANES, ML_REP)),
            _resident((1, ML_WIDTH), l),
        ],
        out_specs=row(ML_WIDTH),
        scratch_shapes=[
            pltpu.VMEM((ML_HEADS, ML_HEAD_DIM, 2 * ML_HEAD_DIM), F32),
            pltpu.VMEM((ML_HEADS, 1, LANES), F32),
        ],
        compiler_params=_cparams(("parallel", "arbitrary")),
        name="mlstm",
    )(mq, mk, mv, gcol, grow, spread, norm_w)


def _gelu_tanh(x):
    return 0.5 * x * (1.0 + jnp.tanh(0.7978845608028654 * (x + 0.044715 * (x * x * x))))


def _merge_kernel(x_ref, yatt_ref, hm_ref, g_ref, wo_ref, wuv_ref, wgt_ref, ng_ref,
                  wsp_ref, bsp_ref, wba_ref, wbm_ref, wbg_ref, wout_ref, o_ref, *, tm):
    x = x_ref[...]
    hn = _rms(x, g_ref[...]).astype(BF16)

    y_ml = jax.nn.sigmoid(_dot(hn, wo_ref[...])) * hm_ref[...]

    uv = _gelu_tanh(_dot(hn, wuv_ref[...]))
    u = uv[:, :GM_WIDTH]
    vn = _rms(uv[:, GM_WIDTH:], ng_ref[...]).astype(BF16)
    r = lax.broadcasted_iota(jnp.int32, (CHUNK, CHUNK), 0)
    c = lax.broadcasted_iota(jnp.int32, (CHUNK, CHUNK), 1)
    mixed = []
    for gi in range(GM_GROUPS):
        w = jnp.where(c <= r, wsp_ref[gi], 0.0).astype(BF16)
        lanes = slice(gi * GM_GROUP_DIM, (gi + 1) * GM_GROUP_DIM)
        mixed.append(jnp.concatenate(
            [_dot(w, vn[ci * CHUNK:(ci + 1) * CHUNK, lanes]) for ci in range(tm // CHUNK)], axis=0))
    y_gm = u * (jnp.concatenate(mixed, axis=-1) + bsp_ref[...])

    gates = jax.nn.sigmoid(_dot(hn, wgt_ref[...]))
    merged = (gates[:, :D_MODEL] * _dot(yatt_ref[...], wba_ref[...])
              + gates[:, D_MODEL:2 * D_MODEL] * _dot(y_ml.astype(BF16), wbm_ref[...])
              + gates[:, 2 * D_MODEL:] * _dot(y_gm.astype(BF16), wbg_ref[...]))
    o_ref[...] = x + _dot(merged.astype(BF16), wout_ref[...])


def _merge(x2, yatt, hm, g, wo, wuv, wgt, ng, wsp, bsp, wba, wbm, wbg, wout, l, tm):
    n = x2.shape[0]
    row = lambda w: pl.BlockSpec((tm, w), lambda i: (i, 0))
    return pl.pallas_call(
        functools.partial(_merge_kernel, tm=tm),
        out_shape=jax.ShapeDtypeStruct((n, D_MODEL), F32),
        grid=(n // tm,),
        in_specs=[
            row(D_MODEL), row(ATT_WIDTH), row(ML_WIDTH),
            _resident((1, D_MODEL), l),
            _resident((D_MODEL, ML_WIDTH), l),
            _resident((D_MODEL, 2 * GM_WIDTH), l),
            _resident((D_MODEL, N_BRANCH * D_MODEL), l),
            _resident((1, GM_WIDTH), l),
            _resident((GM_GROUPS, CHUNK, CHUNK), l),
            _resident((tm, GM_WIDTH), l),
            _resident((ATT_WIDTH, D_MODEL), l),
            _resident((ML_WIDTH, D_MODEL), l),
            _resident((GM_WIDTH, D_MODEL), l),
            _resident((D_MODEL, D_MODEL), l),
        ],
        out_specs=row(D_MODEL),
        compiler_params=_cparams(("parallel",)),
        name="merge",
    )(x2, yatt, hm, g, wo, wuv, wgt, ng, wsp, bsp, wba, wbm, wbg, wout)


def _tile(n, pref):
    t = min(n, pref)
    assert n % t == 0, (n, t)
    return t


def _prepare_layer_params(p, tm_merge):
    depth = p["w_in"].shape[0]
    o_att = 3 * ATT_WIDTH
    o_mqk = o_att + ATT_HEADS
    o_mv = o_mqk + 2 * ML_WIDTH
    o_mi = o_mv + ML_WIDTH
    o_mf = o_mi + ML_HEADS
    o_mo = o_mf + ML_HEADS
    o_uv = o_mo + ML_WIDTH
    o_gt = o_uv + 2 * GM_WIDTH
    w_in = p["w_in"]
    w_small = jnp.concatenate(
        [w_in[:, :, o_att:o_mqk], w_in[:, :, o_mi:o_mf], w_in[:, :, o_mf:o_mo],
         jnp.zeros((depth, D_MODEL, LANES - GATE_ROWS), F32)], axis=-1)
    b_small = jnp.concatenate(
        [p["b_f_att"], p["b_i_ml"], p["b_f_ml"], jnp.zeros((depth, LANES - GATE_ROWS), F32)],
        axis=-1)[:, None, :]
    reps = tm_merge // CHUNK
    bf = lambda a: a.astype(BF16)
    row = lambda a: a[:, None, :]

    def slots(w, width):
        w = w.reshape(depth, D_MODEL, ATT_HEADS, ATT_HEAD_DIM)
        w = jnp.pad(w, ((0, 0), (0, 0), (0, 0), (0, width - ATT_HEAD_DIM)))
        return w.reshape(depth, D_MODEL, ATT_HEADS * width)

    w_q = slots(w_in[:, :, :ATT_WIDTH], ATT_V_ROWS) * (ATT_HEAD_DIM ** -0.5 * LOG2E)
    w_k = slots(w_in[:, :, ATT_WIDTH:2 * ATT_WIDTH], LANES)
    w_v = slots(w_in[:, :, 2 * ATT_WIDTH:o_att], ATT_V_ROWS)
    return dict(
        norm_ffn1=row(p["norm_ffn1"]), ffn1_gate=bf(p["ffn1_gate"]), ffn1_up=bf(p["ffn1_up"]),
        ffn1_down=bf(p["ffn1_down"]),
        norm_mix=row(p["norm_mix"]),
        w_qt=bf(jnp.swapaxes(w_q, 1, 2)), w_k=bf(w_k), w_vt=bf(jnp.swapaxes(w_v, 1, 2)),
        w_small=bf(w_small), b_small=b_small,
        w_mqk=bf(w_in[:, :, o_mqk:o_mv]), w_mv=bf(w_in[:, :, o_mv:o_mi]),
        w_mo=bf(w_in[:, :, o_mo:o_uv]), w_uv=bf(w_in[:, :, o_uv:o_gt]), w_gt=bf(w_in[:, :, o_gt:]),
        conv_ml=p["conv_ml"], norm_ml_head=row(p["norm_ml_head"]), norm_gmlp=row(p["norm_gmlp"]),
        w_spatial=p["w_spatial"],
        b_spatial=jnp.tile(jnp.repeat(jnp.swapaxes(p["b_spatial"], 1, 2), GM_GROUP_DIM, axis=2),
                           (1, reps, 1)),
        w_br_att=bf(p["w_br_att"]), w_br_ml=bf(p["w_br_ml"]), w_br_gmlp=bf(p["w_br_gmlp"]),
        w_out=bf(p["w_out"]),
        norm_ffn2=row(p["norm_ffn2"]), ffn2_gate=bf(p["ffn2_gate"]), ffn2_up=bf(p["ffn2_up"]),
        ffn2_down=bf(p["ffn2_down"]),
    )


def _layer(x2, lp, l, b, s, tiles, final_g):
    n = b * s
    x2 = _ffn(x2, lp["norm_ffn1"], lp["ffn1_gate"], lp["ffn1_up"], lp["ffn1_down"], l,
              tiles["ffn"])
    qt, k, vt, small, mq, mk, mv = _inproj(
        x2.reshape(b, s, D_MODEL), lp["norm_mix"], lp["w_qt"], lp["w_k"], lp["w_vt"],
        lp["w_small"], lp["w_mqk"], lp["w_mv"], lp["conv_ml"], l, tiles["inproj"], tiles["att_q"],
        tiles["att_k"])
    gcol, grow, k, kmax, bpre = _gates(small, lp["b_small"], k, l, tiles["gates"], tiles["att_k"])
    bpre = bpre[:, :, :tiles["gates"] // tiles["att_k"], :].reshape(b, s // tiles["att_k"], LANES)
    yatt = _attention(qt, k, vt, kmax, bpre, tiles["att_q"], tiles["att_k"], tiles["att_heads"])
    hm = _mlstm(mq, mk, mv, gcol, grow, lp["norm_ml_head"], l, tiles["mlstm_chunks"])
    x2 = _merge(x2, yatt.reshape(n, ATT_WIDTH), hm.reshape(n, ML_WIDTH), lp["norm_mix"],
                lp["w_mo"], lp["w_uv"], lp["w_gt"], lp["norm_gmlp"], lp["w_spatial"],
                lp["b_spatial"], lp["w_br_att"], lp["w_br_ml"], lp["w_br_gmlp"], lp["w_out"],
                l, tiles["merge"])
    return _ffn(x2, lp["norm_ffn2"], lp["ffn2_gate"], lp["ffn2_up"], lp["ffn2_down"], l,
                tiles["ffn"], final_g)


def _tiles_for(n, s):
    att_q = _tile(s, 512)
    return dict(ffn=_tile(n, 512), inproj=_tile(s, 1024), gates=_tile(s, 512), att_q=att_q,
                att_k=min(att_q // 2, 256), att_heads=2, merge=_tile(s, 512),
                mlstm_chunks=_tile(s, 4 * CHUNK) // CHUNK)


def _trunk(x, params, norm_final):
    b, s, _ = x.shape
    n = b * s
    tiles = _tiles_for(n, s)
    stacked = _prepare_layer_params(params, tiles["merge"])
    depth = params["w_in"].shape[0]
    x2 = x.reshape(n, D_MODEL)
    for l in range(depth):
        x2 = _layer(x2, stacked, l, b, s, tiles, norm_final[None, :] if l == depth - 1 else None)
    return x2.reshape(b, s, D_MODEL)


def kernel(x, norm_ffn1, ffn1_gate, ffn1_up, ffn1_down, norm_mix, w_in, b_f_att, b_i_ml, b_f_ml, conv_ml, norm_ml_head, norm_gmlp, w_spatial, b_spatial, w_br_att, w_br_ml, w_br_gmlp, w_out, norm_ffn2, ffn2_gate, ffn2_up, ffn2_down, norm_final):
    params = dict(norm_ffn1=norm_ffn1, ffn1_gate=ffn1_gate, ffn1_up=ffn1_up, ffn1_down=ffn1_down,
                  norm_mix=norm_mix, w_in=w_in, b_f_att=b_f_att, b_i_ml=b_i_ml, b_f_ml=b_f_ml,
                  conv_ml=conv_ml, norm_ml_head=norm_ml_head, norm_gmlp=norm_gmlp,
                  w_spatial=w_spatial, b_spatial=b_spatial, w_br_att=w_br_att, w_br_ml=w_br_ml,
                  w_br_gmlp=w_br_gmlp, w_out=w_out, norm_ffn2=norm_ffn2, ffn2_gate=ffn2_gate,
                  ffn2_up=ffn2_up, ffn2_down=ffn2_down)
    return _trunk(x, params, norm_final)
```

```python
import functools

import jax
import jax.numpy as jnp
from jax import lax
from jax.experimental import pallas as pl
from jax.experimental.pallas import tpu as pltpu

D_MODEL = 1024
ATT_HEADS = 8
ATT_HEAD_DIM = 64
ATT_WIDTH = ATT_HEADS * ATT_HEAD_DIM
ML_HEADS = 4
ML_HEAD_DIM = 128
ML_WIDTH = ML_HEADS * ML_HEAD_DIM
CONV_WIDTH = 4
GM_GROUPS = 4
GM_GROUP_DIM = 128
GM_WIDTH = GM_GROUPS * GM_GROUP_DIM
CHUNK = 128
D_FF = 2816
FFN_RES = 0.5
N_BRANCH = 3
EPS = 1e-6

LANES = 128
SUBLANES = 8
ATT_SLOTS = ATT_HEADS * LANES
ATT_V_ROWS = 80
LOG2E = 1.4426950408889634
ATT_SKIP_EXPONENT = 160.0
VMEM_LIMIT = 56 * 1024 * 1024
NEG_BIG = -1e30

GATE_ATT_F = 0
GATE_ML_I = ATT_HEADS
GATE_ML_F = ATT_HEADS + ML_HEADS
GATE_ROWS = 16
ML_REP = 3 * ML_HEADS * 128

F32 = jnp.float32
BF16 = jnp.bfloat16


def _cparams(sem):
    return pltpu.CompilerParams(dimension_semantics=sem, vmem_limit_bytes=VMEM_LIMIT)


def _resident(shape, layer=None):
    nd = len(shape)
    if layer is None:
        return pl.BlockSpec(shape, lambda *_: (0,) * nd, pipeline_mode=pl.Buffered(1))
    return pl.BlockSpec((None,) + tuple(shape), lambda *_: (layer,) + (0,) * nd,
                        pipeline_mode=pl.Buffered(1))


def _rms(x, g):
    ms = jnp.mean(x * x, axis=-1, keepdims=True)
    return x * lax.rsqrt(ms + EPS) * g


def _log_sigmoid(z):
    return jnp.minimum(z, 0.0) - jnp.log1p(jnp.exp(-jnp.abs(z)))


def _dot(a, b):
    return jnp.dot(a, b, preferred_element_type=F32)


def _dot_nt(a, b):
    return lax.dot_general(a, b, (((1,), (1,)), ((), ())), preferred_element_type=F32)


def _ffn_kernel(x_ref, g_ref, wg_ref, wu_ref, wd_ref, *rest, final):
    o_ref = rest[-1]
    x = x_ref[...]
    hn = _rms(x, g_ref[...]).astype(BF16)
    gate = _dot(hn, wg_ref[...])
    up = _dot(hn, wu_ref[...])
    act = (gate * jax.nn.sigmoid(gate) * up).astype(BF16)
    y = x + FFN_RES * _dot(act, wd_ref[...])
    o_ref[...] = _rms(y, rest[0][...]) if final else y


def _ffn(x2, g, wg, wu, wd, l, tm, final_g=None):
    n = x2.shape[0]
    final = final_g is not None
    return pl.pallas_call(
        functools.partial(_ffn_kernel, final=final),
        out_shape=jax.ShapeDtypeStruct((n, D_MODEL), F32),
        grid=(n // tm,),
        in_specs=[
            pl.BlockSpec((tm, D_MODEL), lambda i: (i, 0)),
            _resident((1, D_MODEL), l),
            _resident((D_MODEL, D_FF), l),
            _resident((D_MODEL, D_FF), l),
            _resident((D_FF, D_MODEL), l),
        ] + ([_resident((1, D_MODEL))] if final else []),
        out_specs=pl.BlockSpec((tm, D_MODEL), lambda i: (i, 0)),
        compiler_params=_cparams(("parallel",)),
        name="ffn_final" if final else "ffn",
    )(x2, g, wg, wu, wd, *([final_g] if final else []))


def _inproj_kernel(x_ref, xp_ref, g_ref, wqt_ref, wk_ref, wvt_ref, wsm_ref, wqk_ref, wv_ref,
                   cw_ref, qt_ref, k_ref, vt_ref, sm_ref, mq_ref, mk_ref, mv_ref, *, tm, tq, tk):
    hn = _rms(x_ref[0], g_ref[...])
    hb = hn.astype(BF16)
    hnt = hn.T.astype(BF16)
    slot_row = lax.broadcasted_iota(jnp.int32, (ATT_HEADS * ATT_V_ROWS, tm), 0) % ATT_V_ROWS
    qt = _dot(wqt_ref[...], hnt)
    qt = jnp.where((slot_row >= ATT_HEAD_DIM) & (slot_row < ATT_HEAD_DIM + 3), 1.0, qt)
    qt = qt.astype(BF16)
    vt = _dot(wvt_ref[...], hnt)
    vt = jnp.where(slot_row == ATT_HEAD_DIM, 1.0, vt).astype(BF16)
    for c in range(tm // tq):
        qt_ref[0, :, c] = qt[:, c * tq:(c + 1) * tq].reshape(ATT_HEADS, ATT_V_ROWS, tq)
    for c in range(tm // tk):
        vt_ref[0, :, c] = vt[:, c * tk:(c + 1) * tk].reshape(ATT_HEADS, ATT_V_ROWS, tk)
    k_ref[0] = _dot(hb, wk_ref[...]).astype(BF16)
    sm_ref[0] = _dot(hb, wsm_ref[...])
    mv_ref[0] = _dot(hb, wv_ref[...]).astype(BF16)

    hn_before = _rms(xp_ref[0], g_ref[...])
    hn_before = jnp.where(pl.program_id(1) > 0, hn_before, 0.0)
    xcat = _dot(jnp.concatenate([hn_before, hn], axis=0).astype(BF16), wqk_ref[...])
    proj = xcat[SUBLANES:, :]
    cw = cw_ref[...]
    conv = proj * cw[CONV_WIDTH - 1:CONV_WIDTH, :]
    for j in range(CONV_WIDTH - 1):
        off = SUBLANES - (CONV_WIDTH - 1) + j
        conv = conv + xcat[off:off + tm, :] * cw[j:j + 1, :]
    act = conv * jax.nn.sigmoid(conv)
    mq_ref[0] = (act[:, :ML_WIDTH] * (ML_HEAD_DIM ** -0.5)).astype(BF16)
    mk_ref[0] = act[:, ML_WIDTH:].astype(BF16)


def _inproj(x3, g, wqt, wk, wvt, wsm, wqk, wv, cw, l, tm, tq, tk):
    b, s, _ = x3.shape
    row = lambda w: pl.BlockSpec((1, tm, w), lambda bi, si: (bi, si, 0))
    rows_before = pl.BlockSpec((1, SUBLANES, D_MODEL),
                               lambda bi, si: (bi, jnp.maximum(si * (tm // SUBLANES) - 1, 0), 0))
    tr = lambda rows, t: pl.BlockSpec((1, ATT_HEADS, tm // t, rows, t),
                                      lambda bi, si: (bi, 0, si, 0, 0))
    tr_shape = lambda rows, t: jax.ShapeDtypeStruct((b, ATT_HEADS, s // t, rows, t), BF16)
    return pl.pallas_call(
        functools.partial(_inproj_kernel, tm=tm, tq=tq, tk=tk),
        out_shape=(
            tr_shape(ATT_V_ROWS, tq),
            jax.ShapeDtypeStruct((b, s, ATT_SLOTS), BF16),
            tr_shape(ATT_V_ROWS, tk),
            jax.ShapeDtypeStruct((b, s, LANES), F32),
            jax.ShapeDtypeStruct((b, s, ML_WIDTH), BF16),
            jax.ShapeDtypeStruct((b, s, ML_WIDTH), BF16),
            jax.ShapeDtypeStruct((b, s, ML_WIDTH), BF16),
        ),
        grid=(b, s // tm),
        in_specs=[
            row(D_MODEL),
            rows_before,
            _resident((1, D_MODEL), l),
            _resident((ATT_HEADS * ATT_V_ROWS, D_MODEL), l),
            _resident((D_MODEL, ATT_SLOTS), l),
            _resident((ATT_HEADS * ATT_V_ROWS, D_MODEL), l),
            _resident((D_MODEL, LANES), l),
            _resident((D_MODEL, 2 * ML_WIDTH), l),
            _resident((D_MODEL, ML_WIDTH), l),
            _resident((CONV_WIDTH, 2 * ML_WIDTH), l),
        ],
        out_specs=(tr(ATT_V_ROWS, tq), row(ATT_SLOTS), tr(ATT_V_ROWS, tk), row(LANES),
                   row(ML_WIDTH), row(ML_WIDTH), row(ML_WIDTH)),
        compiler_params=_cparams(("parallel", "parallel")),
        name="inproj",
    )(x3, x3, g, wqt, wk, wvt, wsm, wqk, wv, cw)


def _split3(x):
    hi = x.astype(BF16)
    r1 = x - hi.astype(F32)
    mid = r1.astype(BF16)
    lo = (r1 - mid.astype(F32)).astype(BF16)
    return hi, mid, lo


def _place(x, moves):
    r = lax.broadcasted_iota(jnp.int32, (LANES, LANES), 0)
    c = lax.broadcasted_iota(jnp.int32, (LANES, LANES), 1)
    mat = jnp.zeros((LANES, LANES), F32)
    for src, dst, sign in moves:
        mat = jnp.where((r == src) & (c == dst), sign, mat)
    mat3 = jnp.concatenate([mat.astype(BF16)] * 3, axis=0)
    return _dot(jnp.concatenate(_split3(x), axis=-1), mat3)


def _gates_kernel(sm_ref, bias_ref, k_ref, col_ref, row_ref, ka_ref, kmax_ref, bpre_ref,
                  carry_ref, bcarry_ref, *, ts, tk):
    @pl.when(pl.program_id(1) == 0)
    def _():
        carry_ref[...] = jnp.zeros_like(carry_ref)
        bcarry_ref[...] = jnp.full_like(bcarry_ref, NEG_BIG)
        kmax_ref[...] = jnp.zeros_like(kmax_ref)

    z = sm_ref[0] + bias_ref[...]
    lane = lax.broadcasted_iota(jnp.int32, z.shape, 1)
    is_att = lane < GATE_ML_I
    is_mlf = (lane >= GATE_ML_F) & (lane < GATE_ML_F + ML_HEADS)
    vals = jnp.where(is_att | is_mlf, _log_sigmoid(z), z)

    r = lax.broadcasted_iota(jnp.int32, (ts, ts), 0)
    c = lax.broadcasted_iota(jnp.int32, (ts, ts), 1)
    tri = c <= r
    tri_full = jnp.where(tri, 1.0, 0.0).astype(BF16)
    tri_chunk = jnp.where(tri & ((r // CHUNK) == (c // CHUNK)), 1.0, 0.0).astype(BF16)
    pieces = jnp.concatenate(_split3(vals), axis=-1)
    add3 = lambda y: y[:, :LANES] + y[:, LANES:2 * LANES] + y[:, 2 * LANES:]
    cs_full = add3(_dot(tri_full, pieces)) + carry_ref[...]
    cs_chunk = add3(_dot(tri_chunk, pieces))
    carry_ref[...] = cs_full[ts - 1:ts, :]

    li_moves, b_moves = [], []
    for h in range(ML_HEADS):
        li_moves += [(GATE_ML_I + h, ML_HEADS + h, 1.0), (GATE_ML_I + h, 2 * ML_HEADS + h, 1.0)]
        b_moves += [(GATE_ML_F + h, h, 1.0), (GATE_ML_F + h, ML_HEADS + h, -1.0),
                    (GATE_ML_F + h, 2 * ML_HEADS + h, -1.0)]
    g = _place(vals, li_moves) + _place(cs_chunk, b_moves)
    pos = lax.broadcasted_iota(jnp.int32, (ts, LANES), 0) % CHUNK
    cmax = g
    d = 1
    while d < CHUNK:
        cmax = jnp.where(pos >= d, jnp.maximum(cmax, pltpu.roll(cmax, d, axis=0)), cmax)
        d *= 2
    g = jnp.where(lane >= 2 * ML_HEADS, cmax, g)
    col_ref[0] = g
    row_ref[0] = g.T[:GATE_ROWS, :]

    src = lax.broadcasted_iota(jnp.int32, (3 * LANES, ATT_SLOTS), 0)
    dst = lax.broadcasted_iota(jnp.int32, (3 * LANES, ATT_SLOTS), 1)
    place = (src % LANES < ATT_HEADS) & (dst == (src % LANES) * LANES + ATT_HEAD_DIM + src // LANES)
    decay = cs_full * -LOG2E
    decay_cols = _dot(jnp.concatenate(_split3(decay), axis=-1),
                      jnp.where(place, 1.0, 0.0).astype(BF16))
    kf = k_ref[0].astype(F32)
    ka_ref[0] = (kf + decay_cols).astype(BF16)

    lane_row = lax.broadcasted_iota(jnp.int32, (1, LANES), 1)
    knorm = jnp.zeros((1, LANES), F32)
    for h in range(ATT_HEADS):
        kh = kf[:, h * LANES:(h + 1) * LANES]
        sq = jnp.max(jnp.sum(kh * kh, axis=-1, keepdims=True), axis=0, keepdims=True)
        knorm = jnp.where(lane_row == h, jnp.sqrt(sq), knorm)
    kmax_ref[0] = jnp.maximum(kmax_ref[0], jnp.broadcast_to(knorm, (SUBLANES, LANES)))

    rows = []
    running = bcarry_ref[...]
    for c in range(ts // tk):
        running = jnp.maximum(running, jnp.max(decay[c * tk:(c + 1) * tk, :], axis=0, keepdims=True))
        rows.append(running)
    bcarry_ref[...] = running
    rows += [running] * (SUBLANES - len(rows))
    bpre_ref[0, 0] = jnp.concatenate(rows, axis=0)


def _gates(small, bias, k, l, ts, tk):
    b, s, _ = small.shape
    assert ts % tk == 0 and ts // tk <= SUBLANES
    row = lambda w: pl.BlockSpec((1, ts, w), lambda bi, si: (bi, si, 0))
    return pl.pallas_call(
        functools.partial(_gates_kernel, ts=ts, tk=tk),
        out_shape=(jax.ShapeDtypeStruct((b, s, LANES), F32),
                   jax.ShapeDtypeStruct((b, GATE_ROWS, s), F32),
                   jax.ShapeDtypeStruct((b, s, ATT_SLOTS), BF16),
                   jax.ShapeDtypeStruct((b, SUBLANES, LANES), F32),
                   jax.ShapeDtypeStruct((b, s // ts, SUBLANES, LANES), F32)),
        grid=(b, s // ts),
        in_specs=[row(LANES), _resident((1, LANES), l), row(ATT_SLOTS)],
        out_specs=(row(LANES),
                   pl.BlockSpec((1, GATE_ROWS, ts), lambda bi, si: (bi, 0, si)),
                   row(ATT_SLOTS),
                   pl.BlockSpec((1, SUBLANES, LANES), lambda bi, si: (bi, 0, 0)),
                   pl.BlockSpec((1, 1, SUBLANES, LANES), lambda bi, si: (bi, si, 0, 0))),
        scratch_shapes=[pltpu.VMEM((1, LANES), F32), pltpu.VMEM((1, LANES), F32)],
        compiler_params=_cparams(("parallel", "arbitrary")),
        name="gates",
    )(small, bias, k)


def _att_kernel(qt_ref, k_ref, vt_ref, kmax_ref, bpre_ref, o_ref, s0_ref, s1_ref, acc_ref,
                *, tq, tk, hb):
    i = pl.program_id(2)
    nd = tq // tk
    s_refs = (s0_ref, s1_ref)

    q_pad = jnp.zeros((LANES - ATT_V_ROWS, tq), BF16)
    q_slots = [jnp.concatenate([qt_ref[0, hh, 0], q_pad], axis=0) for hh in range(hb)]

    def scores(hh, chunk):
        kj = k_ref[0, pl.ds(pl.multiple_of(chunk * tk, tk), tk), hh * LANES:(hh + 1) * LANES]
        return _dot(kj, q_slots[hh])

    def visit(par, carry, cur_chunk, next_chunk, next_mask=None):
        oth = 1 - par
        out = []
        for hh in range(hb):
            m, cmax = carry[hh]
            s_next = scores(hh, next_chunk)
            m_new = jnp.maximum(m, cmax)
            alpha = jnp.exp2(m - m_new)
            p = jnp.exp2(s_refs[par][hh] - m_new).astype(BF16)
            acc_ref[hh] = acc_ref[hh] * alpha + _dot(vt_ref[0, hh, cur_chunk], p)
            if next_mask is not None:
                s_next = jnp.where(next_mask, s_next, NEG_BIG)
            s_refs[oth][hh] = s_next
            out.append((m_new, jnp.max(s_next, axis=0, keepdims=True)))
        return tuple(out)

    key = lax.broadcasted_iota(jnp.int32, (tk, tq), 0)
    qry = lax.broadcasted_iota(jnp.int32, (tk, tq), 1)
    init = []
    for hh in range(hb):
        s_first = jnp.where(key <= qry, scores(hh, nd * i), NEG_BIG)
        s0_ref[hh] = s_first
        acc_ref[hh] = jnp.zeros((ATT_V_ROWS, tq), F32)
        init.append((jnp.full((1, tq), NEG_BIG, F32), jnp.max(s_first, axis=0, keepdims=True)))
    carry = tuple(init)
    below = nd * i
    for d in range(nd):
        if d + 1 < nd:
            carry = visit(d % 2, carry, below + d, below + d + 1, key + (d + 1) * tk <= qry)
        else:
            carry = visit(d % 2, carry, below + d, jnp.maximum(below - 1, 0))

    lane = lax.broadcasted_iota(jnp.int32, (1, LANES), 1)
    bound = jnp.full((1, LANES), NEG_BIG, F32)
    for hh in range(hb):
        h = pl.program_id(1) * hb + hh
        qf = qt_ref[0, hh, 0][:ATT_HEAD_DIM, :].astype(F32)
        qmax = jnp.sqrt(jnp.max(jnp.sum(qf * qf, axis=0, keepdims=True), axis=-1, keepdims=True))
        mmin = jnp.min(carry[hh][0], axis=-1, keepdims=True)
        bound = jnp.where(lane == h, qmax * kmax_ref[0, 0:1, :] - mmin, bound)
    chunk_id = lax.broadcasted_iota(jnp.int32, bpre_ref.shape[1:], 0)
    live = (bpre_ref[0] + bound >= -ATT_SKIP_EXPONENT) & (chunk_id < below)
    n_live = jnp.max(jnp.sum(jnp.where(live, 1.0, 0.0), axis=0, keepdims=True)).astype(jnp.int32)
    n_pairs = (n_live + 1) // 2

    def pair(w, carry):
        top = below - 2 * w
        carry = visit(0, carry, top - 1, top - 2)
        return visit(1, carry, top - 2, jnp.maximum(top - 3, 0))

    def two_pairs(w2, carry):
        return pair(2 * w2 + 1, pair(2 * w2, carry))

    carry = lax.fori_loop(0, n_pairs // 2, two_pairs, carry)
    carry = lax.fori_loop(2 * (n_pairs // 2), n_pairs, pair, carry)
    for pr in range(hb // 2):
        halves = []
        for hh in (2 * pr, 2 * pr + 1):
            acc = acc_ref[hh]
            halves.append(acc[:ATT_HEAD_DIM, :] / acc[ATT_HEAD_DIM:ATT_HEAD_DIM + 1, :])
        both = jnp.concatenate(halves, axis=0).T
        o_ref[0, :, pr * LANES:(pr + 1) * LANES] = both.astype(o_ref.dtype)


def _attention(qt, k, vt, kmax, bpre, tq, tk, hb):
    b, s, _ = k.shape
    assert tq % (2 * tk) == 0 and bpre.shape == (b, s // tk, LANES)
    return pl.pallas_call(
        functools.partial(_att_kernel, tq=tq, tk=tk, hb=hb),
        out_shape=jax.ShapeDtypeStruct((b, s, ATT_WIDTH), BF16),
        grid=(b, ATT_HEADS // hb, s // tq),
        in_specs=[
            pl.BlockSpec((1, hb, 1, ATT_V_ROWS, tq), lambda bi, hg, i: (bi, hg, i, 0, 0)),
            pl.BlockSpec((1, s, hb * LANES), lambda bi, hg, i: (bi, 0, hg),
                         pipeline_mode=pl.Buffered(1)),
            pl.BlockSpec((1, hb, s // tk, ATT_V_ROWS, tk), lambda bi, hg, i: (bi, hg, 0, 0, 0),
                         pipeline_mode=pl.Buffered(1)),
            pl.BlockSpec((1, SUBLANES, LANES), lambda bi, hg, i: (bi, 0, 0)),
            pl.BlockSpec((1, s // tk, LANES), lambda bi, hg, i: (bi, 0, 0)),
        ],
        out_specs=pl.BlockSpec((1, tq, hb * ATT_HEAD_DIM), lambda bi, hg, i: (bi, i, hg)),
        scratch_shapes=[pltpu.VMEM((hb, tk, tq), F32), pltpu.VMEM((hb, tk, tq), F32),
                        pltpu.VMEM((hb, ATT_V_ROWS, tq), F32)],
        compiler_params=_cparams(("parallel", "parallel", "arbitrary")),
        name="fox_attention",
    )(qt, k, vt, kmax, bpre)


def _mlstm_kernel(q_ref, k_ref, v_ref, gcol_ref, grow_ref, spread_ref, nw_ref, o_ref,
                  st_ref, m_ref, *, nsub):
    @pl.when(pl.program_id(1) == 0)
    def _():
        st_ref[...] = jnp.zeros_like(st_ref)
        m_ref[...] = jnp.zeros_like(m_ref)

    L = CHUNK
    D = ML_HEAD_DIM
    H = range(ML_HEADS)
    r = lax.broadcasted_iota(jnp.int32, (L, L), 0)
    c = lax.broadcasted_iota(jnp.int32, (L, L), 1)
    causal = c <= r
    ones = jnp.ones((L, D), BF16)
    mean_mat = jnp.full((D, D), 1.0 / D, BF16)
    two = lambda x: jnp.concatenate([x, x], axis=-1)

    st = [st_ref[h] for h in H]
    m = [m_ref[h] for h in H]
    pre = []
    for sc in range(nsub):
        rows = slice(sc * L, (sc + 1) * L)
        head = lambda ref, h, rows=rows: ref[0, rows, h * D:(h + 1) * D]
        rep = _dot(jnp.concatenate(_split3(gcol_ref[0, rows, :]), axis=-1), spread_ref[...])
        qk = [_dot_nt(head(q_ref, h), head(k_ref, h)) for h in H]
        kt = [head(k_ref, h).astype(F32).T.astype(BF16) for h in H]
        pre.append((rows, head, rep, qk, kt))

    for sc in range(nsub):
        rows, head, rep, qk, kt = pre[sc]
        col = lambda j, h: rep[:, (j * ML_HEADS + h) * LANES:(j * ML_HEADS + h + 1) * LANES]
        inter = [_dot(head(q_ref, h), st[h].astype(BF16)) for h in H]
        m_t, w_intra, w_inter, decay, wv, m_next = [], [], [], [], [], []
        for h in H:
            b, gq, cm = col(0, h), col(1, h), col(2, h)
            gq_row = grow_ref[0, ML_HEADS + h:ML_HEADS + h + 1, rows]
            a = b + m[h]
            m_t.append(jnp.maximum(a, b + cm))
            w_intra.append(jnp.exp(jnp.where(causal, b + gq_row, NEG_BIG) - m_t[h]))
            w_inter.append(jnp.exp(a - m_t[h]))
            b_last = b[L - 1:L, :]
            m_new = jnp.maximum(b_last + m[h], b_last + cm[L - 1:L, :])
            w_s = jnp.exp(b_last + gq - m_new)
            decay.append(jnp.exp(b_last + m[h] - m_new))
            wv.append(jnp.concatenate([w_s * head(v_ref, h).astype(F32), w_s],
                                      axis=-1).astype(BF16))
            m_next.append(m_new)
        qkw = [(qk[h] * w_intra[h]).astype(BF16) for h in H]
        both = [_dot(qkw[h], jnp.concatenate([head(v_ref, h), ones], axis=-1))
                + two(w_inter[h]) * inter[h] for h in H]
        st = [two(decay[h]) * st[h] + _dot(kt[h], wv[h]) for h in H]
        m = m_next
        hh = [both[h][:, :D] / jnp.maximum(jnp.abs(both[h][:, D:]), jnp.exp(-m_t[h])) for h in H]
        ms = []
        for h in H:
            sq = hh[h] * hh[h]
            sq_hi = sq.astype(BF16)
            sq_lo = (sq - sq_hi.astype(F32)).astype(BF16)
            ms.append(_dot(sq_hi, mean_mat) + _dot(sq_lo, mean_mat))
        for h in H:
            o_ref[0, rows, h * D:(h + 1) * D] = (hh[h] * lax.rsqrt(ms[h] + EPS)
                                                 * nw_ref[:, h * D:(h + 1) * D])
    for h in H:
        st_ref[h] = st[h]
        m_ref[h] = m[h]


def _mlstm(mq, mk, mv, gcol, grow, norm_w, l, nsub):
    b, s, _ = mq.shape
    rows = nsub * CHUNK
    nc = s // rows
    row = lambda w: pl.BlockSpec((1, rows, w), lambda bi, ci: (bi, ci, 0))
    src = lax.broadcasted_iota(jnp.int32, (3 * LANES, ML_REP), 0) % LANES
    dst = lax.broadcasted_iota(jnp.int32, (3 * LANES, ML_REP), 1) // LANES
    spread = (src == dst).astype(BF16)
    return pl.pallas_call(
        functools.partial(_mlstm_kernel, nsub=nsub),
        out_shape=jax.ShapeDtypeStruct((b, s, ML_WIDTH), F32),
        grid=(b, nc),
        in_specs=[
            row(ML_WIDTH), row(ML_WIDTH), row(ML_WIDTH), row(LANES),
            pl.BlockSpec((1, GATE_ROWS, rows), lambda bi, ci: (bi, 0, ci)),
            _resident((3 * LANES, ML_REP)),
            _resident((1, ML_WIDTH), l),
        ],
        out_specs=row(ML_WIDTH),
        scratch_shapes=[
            pltpu.VMEM((ML_HEADS, ML_HEAD_DIM, 2 * ML_HEAD_DIM), F32),
            pltpu.VMEM((ML_HEADS, 1, LANES), F32),
        ],
        compiler_params=_cparams(("parallel", "arbitrary")),
        name="mlstm",
    )(mq, mk, mv, gcol, grow, spread, norm_w)


def _gelu_tanh(x):
    return 0.5 * x * (1.0 + jnp.tanh(0.7978845608028654 * (x + 0.044715 * (x * x * x))))


def _merge_kernel(x_ref, yatt_ref, hm_ref, g_ref, wo_ref, wuv_ref, wgt_ref, ng_ref,
                  wsp_ref, bsp_ref, wba_ref, wbm_ref, wbg_ref, wout_ref, o_ref, *, tm):
    x = x_ref[...]
    hn = _rms(x, g_ref[...]).astype(BF16)
    branch_gate = lambda j: jax.nn.sigmoid(_dot(hn, wgt_ref[:, j * D_MODEL:(j + 1) * D_MODEL]))

    uv_pre = _dot(hn, wuv_ref[...])
    o_pre = _dot(hn, wo_ref[...])
    merged = branch_gate(0) * _dot(yatt_ref[...], wba_ref[...])

    y_ml = jax.nn.sigmoid(o_pre) * hm_ref[...]
    merged = merged + branch_gate(1) * _dot(y_ml.astype(BF16), wbm_ref[...])

    uv = _gelu_tanh(uv_pre)
    u = uv[:, :GM_WIDTH]
    vn = _rms(uv[:, GM_WIDTH:], ng_ref[...]).astype(BF16)
    r = lax.broadcasted_iota(jnp.int32, (CHUNK, CHUNK), 0)
    c = lax.broadcasted_iota(jnp.int32, (CHUNK, CHUNK), 1)
    mixed = []
    for gi in range(GM_GROUPS):
        w = jnp.where(c <= r, wsp_ref[gi], 0.0).astype(BF16)
        lanes = slice(gi * GM_GROUP_DIM, (gi + 1) * GM_GROUP_DIM)
        mixed.append(jnp.concatenate(
            [_dot(w, vn[ci * CHUNK:(ci + 1) * CHUNK, lanes]) for ci in range(tm // CHUNK)], axis=0))
    y_gm = u * (jnp.concatenate(mixed, axis=-1) + bsp_ref[...])

    merged = merged + branch_gate(2) * _dot(y_gm.astype(BF16), wbg_ref[...])
    o_ref[...] = x + _dot(merged.astype(BF16), wout_ref[...])


def _merge(x2, yatt, hm, g, wo, wuv, wgt, ng, wsp, bsp, wba, wbm, wbg, wout, l, tm):
    n = x2.shape[0]
    row = lambda w: pl.BlockSpec((tm, w), lambda i: (i, 0))
    return pl.pallas_call(
        functools.partial(_merge_kernel, tm=tm),
        out_shape=jax.ShapeDtypeStruct((n, D_MODEL), F32),
        grid=(n // tm,),
        in_specs=[
            row(D_MODEL), row(ATT_WIDTH), row(ML_WIDTH),
            _resident((1, D_MODEL), l),
            _resident((D_MODEL, ML_WIDTH), l),
            _resident((D_MODEL, 2 * GM_WIDTH), l),
            _resident((D_MODEL, N_BRANCH * D_MODEL), l),
            _resident((1, GM_WIDTH), l),
            _resident((GM_GROUPS, CHUNK, CHUNK), l),
            _resident((tm, GM_WIDTH), l),
            _resident((ATT_WIDTH, D_MODEL), l),
            _resident((ML_WIDTH, D_MODEL), l),
            _resident((GM_WIDTH, D_MODEL), l),
            _resident((D_MODEL, D_MODEL), l),
        ],
        out_specs=row(D_MODEL),
        compiler_params=_cparams(("parallel",)),
        name="merge",
    )(x2, yatt, hm, g, wo, wuv, wgt, ng, wsp, bsp, wba, wbm, wbg, wout)


def _tile(n, pref):
    t = min(n, pref)
    assert n % t == 0, (n, t)
    return t


def _prepare_layer_params(p, tm_merge):
    depth = p["w_in"].shape[0]
    o_att = 3 * ATT_WIDTH
    o_mqk = o_att + ATT_HEADS
    o_mv = o_mqk + 2 * ML_WIDTH
    o_mi = o_mv + ML_WIDTH
    o_mf = o_mi + ML_HEADS
    o_mo = o_mf + ML_HEADS
    o_uv = o_mo + ML_WIDTH
    o_gt = o_uv + 2 * GM_WIDTH
    w_in = p["w_in"]
    w_small = jnp.concatenate(
        [w_in[:, :, o_att:o_mqk], w_in[:, :, o_mi:o_mf], w_in[:, :, o_mf:o_mo],
         jnp.zeros((depth, D_MODEL, LANES - GATE_ROWS), F32)], axis=-1)
    b_small = jnp.concatenate(
        [p["b_f_att"], p["b_i_ml"], p["b_f_ml"], jnp.zeros((depth, LANES - GATE_ROWS), F32)],
        axis=-1)[:, None, :]
    reps = tm_merge // CHUNK
    bf = lambda a: a.astype(BF16)
    row = lambda a: a[:, None, :]

    def slots(w, width):
        w = w.reshape(depth, D_MODEL, ATT_HEADS, ATT_HEAD_DIM)
        w = jnp.pad(w, ((0, 0), (0, 0), (0, 0), (0, width - ATT_HEAD_DIM)))
        return w.reshape(depth, D_MODEL, ATT_HEADS * width)

    w_q = slots(w_in[:, :, :ATT_WIDTH], ATT_V_ROWS) * (ATT_HEAD_DIM ** -0.5 * LOG2E)
    w_k = slots(w_in[:, :, ATT_WIDTH:2 * ATT_WIDTH], LANES)
    w_v = slots(w_in[:, :, 2 * ATT_WIDTH:o_att], ATT_V_ROWS)
    return dict(
        norm_ffn1=row(p["norm_ffn1"]), ffn1_gate=bf(p["ffn1_gate"]), ffn1_up=bf(p["ffn1_up"]),
        ffn1_down=bf(p["ffn1_down"]),
        norm_mix=row(p["norm_mix"]),
        w_qt=bf(jnp.swapaxes(w_q, 1, 2)), w_k=bf(w_k), w_vt=bf(jnp.swapaxes(w_v, 1, 2)),
        w_small=bf(w_small), b_small=b_small,
        w_mqk=bf(w_in[:, :, o_mqk:o_mv]), w_mv=bf(w_in[:, :, o_mv:o_mi]),
        w_mo=bf(w_in[:, :, o_mo:o_uv]), w_uv=bf(w_in[:, :, o_uv:o_gt]), w_gt=bf(w_in[:, :, o_gt:]),
        conv_ml=p["conv_ml"], norm_ml_head=row(p["norm_ml_head"]), norm_gmlp=row(p["norm_gmlp"]),
        w_spatial=p["w_spatial"],
        b_spatial=jnp.tile(jnp.repeat(jnp.swapaxes(p["b_spatial"], 1, 2), GM_GROUP_DIM, axis=2),
                           (1, reps, 1)),
        w_br_att=bf(p["w_br_att"]), w_br_ml=bf(p["w_br_ml"]), w_br_gmlp=bf(p["w_br_gmlp"]),
        w_out=bf(p["w_out"]),
        norm_ffn2=row(p["norm_ffn2"]), ffn2_gate=bf(p["ffn2_gate"]), ffn2_up=bf(p["ffn2_up"]),
        ffn2_down=bf(p["ffn2_down"]),
    )


def _layer(x2, lp, l, b, s, tiles, final_g):
    n = b * s
    x2 = _ffn(x2, lp["norm_ffn1"], lp["ffn1_gate"], lp["ffn1_up"], lp["ffn1_down"], l,
              tiles["ffn"])
    qt, k, vt, small, mq, mk, mv = _inproj(
        x2.reshape(b, s, D_MODEL), lp["norm_mix"], lp["w_qt"], lp["w_k"], lp["w_vt"],
        lp["w_small"], lp["w_mqk"], lp["w_mv"], lp["conv_ml"], l, tiles["inproj"], tiles["att_q"],
        tiles["att_k"])
    gcol, grow, k, kmax, bpre = _gates(small, lp["b_small"], k, l, tiles["gates"], tiles["att_k"])
    bpre = bpre[:, :, :tiles["gates"] // tiles["att_k"], :].reshape(b, s // tiles["att_k"], LANES)
    yatt = _attention(qt, k, vt, kmax, bpre, tiles["att_q"], tiles["att_k"], tiles["att_heads"])
    hm = _mlstm(mq, mk, mv, gcol, grow, lp["norm_ml_head"], l, tiles["mlstm_chunks"])
    x2 = _merge(x2, yatt.reshape(n, ATT_WIDTH), hm.reshape(n, ML_WIDTH), lp["norm_mix"],
                lp["w_mo"], lp["w_uv"], lp["w_gt"], lp["norm_gmlp"], lp["w_spatial"],
                lp["b_spatial"], lp["w_br_att"], lp["w_br_ml"], lp["w_br_gmlp"], lp["w_out"],
                l, tiles["merge"])
    return _ffn(x2, lp["norm_ffn2"], lp["ffn2_gate"], lp["ffn2_up"], lp["ffn2_down"], l,
                tiles["ffn"], final_g)


def _tiles_for(n, s):
    att_q = _tile(s, 512)
    return dict(ffn=_tile(n, 512), inproj=_tile(s, 1024), gates=_tile(s, 512), att_q=att_q,
                att_k=min(att_q // 2, 256), att_heads=2, merge=_tile(s, 512),
                mlstm_chunks=_tile(s, 4 * CHUNK) // CHUNK)


def _trunk(x, params, norm_final):
    b, s, _ = x.shape
    n = b * s
    tiles = _tiles_for(n, s)
    stacked = _prepare_layer_params(params, tiles["merge"])
    depth = params["w_in"].shape[0]
    x2 = x.reshape(n, D_MODEL)
    for l in range(depth):
        x2 = _layer(x2, stacked, l, b, s, tiles, norm_final[None, :] if l == depth - 1 else None)
    return x2.reshape(b, s, D_MODEL)


def kernel(x, norm_ffn1, ffn1_gate, ffn1_up, ffn1_down, norm_mix, w_in, b_f_att, b_i_ml, b_f_ml, conv_ml, norm_ml_head, norm_gmlp, w_spatial, b_spatial, w_br_att, w_br_ml, w_br_gmlp, w_out, norm_ffn2, ffn2_gate, ffn2_up, ffn2_down, norm_final):
    params = dict(norm_ffn1=norm_ffn1, ffn1_gate=ffn1_gate, ffn1_up=ffn1_up, ffn1_down=ffn1_down,
                  norm_mix=norm_mix, w_in=w_in, b_f_att=b_f_att, b_i_ml=b_i_ml, b_f_ml=b_f_ml,
                  conv_ml=conv_ml, norm_ml_head=norm_ml_head, norm_gmlp=norm_gmlp,
                  w_spatial=w_spatial, b_spatial=b_spatial, w_br_att=w_br_att, w_br_ml=w_br_ml,
                  w_br_gmlp=w_br_gmlp, w_out=w_out, norm_ffn2=norm_ffn2, ffn2_gate=ffn2_gate,
                  ffn2_up=ffn2_up, ffn2_down=ffn2_down)
    return _trunk(x, params, norm_final)
```

```python
import functools

import jax
import jax.numpy as jnp
from jax import lax
from jax.experimental import pallas as pl
from jax.experimental.pallas import tpu as pltpu

D_MODEL = 1024
ATT_HEADS = 8
ATT_HEAD_DIM = 64
ATT_WIDTH = ATT_HEADS * ATT_HEAD_DIM
ML_HEADS = 4
ML_HEAD_DIM = 128
ML_WIDTH = ML_HEADS * ML_HEAD_DIM
CONV_WIDTH = 4
GM_GROUPS = 4
GM_GROUP_DIM = 128
GM_WIDTH = GM_GROUPS * GM_GROUP_DIM
CHUNK = 128
D_FF = 2816
FFN_RES = 0.5
N_BRANCH = 3
EPS = 1e-6

LANES = 128
SUBLANES = 8
ATT_SLOTS = ATT_HEADS * LANES
ATT_V_ROWS = 80
LOG2E = 1.4426950408889634
ATT_SKIP_EXPONENT = 160.0
VMEM_LIMIT = 56 * 1024 * 1024
NEG_BIG = -1e30

GATE_ATT_F = 0
GATE_ML_I = ATT_HEADS
GATE_ML_F = ATT_HEADS + ML_HEADS
GATE_ROWS = 16
ML_REP = 3 * ML_HEADS * 128

F32 = jnp.float32
BF16 = jnp.bfloat16


def _cparams(sem):
    return pltpu.CompilerParams(dimension_semantics=sem, vmem_limit_bytes=VMEM_LIMIT)


def _resident(shape, layer=None):
    nd = len(shape)
    if layer is None:
        return pl.BlockSpec(shape, lambda *_: (0,) * nd, pipeline_mode=pl.Buffered(1))
    return pl.BlockSpec((None,) + tuple(shape), lambda *_: (layer,) + (0,) * nd,
                        pipeline_mode=pl.Buffered(1))


def _rms(x, g):
    ms = jnp.mean(x * x, axis=-1, keepdims=True)
    return x * lax.rsqrt(ms + EPS) * g


def _log_sigmoid(z):
    return jnp.minimum(z, 0.0) - jnp.log1p(jnp.exp(-jnp.abs(z)))


def _dot(a, b):
    return jnp.dot(a, b, preferred_element_type=F32)


def _dot_nt(a, b):
    return lax.dot_general(a, b, (((1,), (1,)), ((), ())), preferred_element_type=F32)


def _ffn_kernel(x_ref, g_ref, wg_ref, wu_ref, wd_ref, *rest, final):
    o_ref = rest[-1]
    x = x_ref[...]
    hn = _rms(x, g_ref[...]).astype(BF16)
    gate = _dot(hn, wg_ref[...])
    up = _dot(hn, wu_ref[...])
    act = (gate * jax.nn.sigmoid(gate) * up).astype(BF16)
    y = x + FFN_RES * _dot(act, wd_ref[...])
    o_ref[...] = _rms(y, rest[0][...]) if final else y


def _ffn(x2, g, wg, wu, wd, l, tm, final_g=None):
    n = x2.shape[0]
    final = final_g is not None
    return pl.pallas_call(
        functools.partial(_ffn_kernel, final=final),
        out_shape=jax.ShapeDtypeStruct((n, D_MODEL), F32),
        grid=(n // tm,),
        in_specs=[
            pl.BlockSpec((tm, D_MODEL), lambda i: (i, 0)),
            _resident((1, D_MODEL), l),
            _resident((D_MODEL, D_FF), l),
            _resident((D_MODEL, D_FF), l),
            _resident((D_FF, D_MODEL), l),
        ] + ([_resident((1, D_MODEL))] if final else []),
        out_specs=pl.BlockSpec((tm, D_MODEL), lambda i: (i, 0)),
        compiler_params=_cparams(("parallel",)),
        name="ffn_final" if final else "ffn",
    )(x2, g, wg, wu, wd, *([final_g] if final else []))


def _inproj_kernel(x_ref, xp_ref, g_ref, wqt_ref, wk_ref, wvt_ref, wsm_ref, wqk_ref, wv_ref,
                   cw_ref, qt_ref, k_ref, vt_ref, sm_ref, mq_ref, mk_ref, mv_ref, *, tm, tq, tk):
    hn = _rms(x_ref[0], g_ref[...])
    hb = hn.astype(BF16)
    hnt = hn.T.astype(BF16)
    slot_row = lax.broadcasted_iota(jnp.int32, (ATT_HEADS * ATT_V_ROWS, tm), 0) % ATT_V_ROWS
    qt = _dot(wqt_ref[...], hnt)
    qt = jnp.where((slot_row >= ATT_HEAD_DIM) & (slot_row < ATT_HEAD_DIM + 3), 1.0, qt)
    qt = qt.astype(BF16)
    vt = _dot(wvt_ref[...], hnt)
    vt = jnp.where(slot_row == ATT_HEAD_DIM, 1.0, vt).astype(BF16)
    for c in range(tm // tq):
        qt_ref[0, :, c] = qt[:, c * tq:(c + 1) * tq].reshape(ATT_HEADS, ATT_V_ROWS, tq)
    for c in range(tm // tk):
        vt_ref[0, :, c] = vt[:, c * tk:(c + 1) * tk].reshape(ATT_HEADS, ATT_V_ROWS, tk)
    k_ref[0] = _dot(hb, wk_ref[...]).astype(BF16)
    sm_ref[0] = _dot(hb, wsm_ref[...])
    mv_ref[0] = _dot(hb, wv_ref[...]).astype(BF16)

    hn_before = _rms(xp_ref[0], g_ref[...])
    hn_before = jnp.where(pl.program_id(1) > 0, hn_before, 0.0)
    xcat = _dot(jnp.concatenate([hn_before, hn], axis=0).astype(BF16), wqk_ref[...])
    proj = xcat[SUBLANES:, :]
    cw = cw_ref[...]
    conv = proj * cw[CONV_WIDTH - 1:CONV_WIDTH, :]
    for j in range(CONV_WIDTH - 1):
        off = SUBLANES - (CONV_WIDTH - 1) + j
        conv = conv + xcat[off:off + tm, :] * cw[j:j + 1, :]
    act = conv * jax.nn.sigmoid(conv)
    mq_ref[0] = (act[:, :ML_WIDTH] * (ML_HEAD_DIM ** -0.5)).astype(BF16)
    mk_ref[0] = act[:, ML_WIDTH:].astype(BF16)


def _inproj(x3, g, wqt, wk, wvt, wsm, wqk, wv, cw, l, tm, tq, tk):
    b, s, _ = x3.shape
    row = lambda w: pl.BlockSpec((1, tm, w), lambda bi, si: (bi, si, 0))
    rows_before = pl.BlockSpec((1, SUBLANES, D_MODEL),
                               lambda bi, si: (bi, jnp.maximum(si * (tm // SUBLANES) - 1, 0), 0))
    tr = lambda rows, t: pl.BlockSpec((1, ATT_HEADS, tm // t, rows, t),
                                      lambda bi, si: (bi, 0, si, 0, 0))
    tr_shape = lambda rows, t: jax.ShapeDtypeStruct((b, ATT_HEADS, s // t, rows, t), BF16)
    return pl.pallas_call(
        functools.partial(_inproj_kernel, tm=tm, tq=tq, tk=tk),
        out_shape=(
            tr_shape(ATT_V_ROWS, tq),
            jax.ShapeDtypeStruct((b, s, ATT_SLOTS), BF16),
            tr_shape(ATT_V_ROWS, tk),
            jax.ShapeDtypeStruct((b, s, LANES), F32),
            jax.ShapeDtypeStruct((b, s, ML_WIDTH), BF16),
            jax.ShapeDtypeStruct((b, s, ML_WIDTH), BF16),
            jax.ShapeDtypeStruct((b, s, ML_WIDTH), BF16),
        ),
        grid=(b, s // tm),
        in_specs=[
            row(D_MODEL),
            rows_before,
            _resident((1, D_MODEL), l),
            _resident((ATT_HEADS * ATT_V_ROWS, D_MODEL), l),
            _resident((D_MODEL, ATT_SLOTS), l),
            _resident((ATT_HEADS * ATT_V_ROWS, D_MODEL), l),
            _resident((D_MODEL, LANES), l),
            _resident((D_MODEL, 2 * ML_WIDTH), l),
            _resident((D_MODEL, ML_WIDTH), l),
            _resident((CONV_WIDTH, 2 * ML_WIDTH), l),
        ],
        out_specs=(tr(ATT_V_ROWS, tq), row(ATT_SLOTS), tr(ATT_V_ROWS, tk), row(LANES),
                   row(ML_WIDTH), row(ML_WIDTH), row(ML_WIDTH)),
        compiler_params=_cparams(("parallel", "parallel")),
        name="inproj",
    )(x3, x3, g, wqt, wk, wvt, wsm, wqk, wv, cw)


def _split3(x):
    hi = x.astype(BF16)
    r1 = x - hi.astype(F32)
    mid = r1.astype(BF16)
    lo = (r1 - mid.astype(F32)).astype(BF16)
    return hi, mid, lo


def _place(x, moves):
    r = lax.broadcasted_iota(jnp.int32, (LANES, LANES), 0)
    c = lax.broadcasted_iota(jnp.int32, (LANES, LANES), 1)
    mat = jnp.zeros((LANES, LANES), F32)
    for src, dst, sign in moves:
        mat = jnp.where((r == src) & (c == dst), sign, mat)
    mat3 = jnp.concatenate([mat.astype(BF16)] * 3, axis=0)
    return _dot(jnp.concatenate(_split3(x), axis=-1), mat3)


def _gates_kernel(sm_ref, bias_ref, k_ref, col_ref, row_ref, ka_ref, kmax_ref, bpre_ref,
                  carry_ref, bcarry_ref, *, ts, tk):
    @pl.when(pl.program_id(1) == 0)
    def _():
        carry_ref[...] = jnp.zeros_like(carry_ref)
        bcarry_ref[...] = jnp.full_like(bcarry_ref, NEG_BIG)
        kmax_ref[...] = jnp.zeros_like(kmax_ref)

    z = sm_ref[0] + bias_ref[...]
    lane = lax.broadcasted_iota(jnp.int32, z.shape, 1)
    is_att = lane < GATE_ML_I
    is_mlf = (lane >= GATE_ML_F) & (lane < GATE_ML_F + ML_HEADS)
    vals = jnp.where(is_att | is_mlf, _log_sigmoid(z), z)

    r = lax.broadcasted_iota(jnp.int32, (ts, ts), 0)
    c = lax.broadcasted_iota(jnp.int32, (ts, ts), 1)
    tri = c <= r
    tri_full = jnp.where(tri, 1.0, 0.0).astype(BF16)
    tri_chunk = jnp.where(tri & ((r // CHUNK) == (c // CHUNK)), 1.0, 0.0).astype(BF16)
    pieces = jnp.concatenate(_split3(vals), axis=-1)
    add3 = lambda y: y[:, :LANES] + y[:, LANES:2 * LANES] + y[:, 2 * LANES:]
    cs_full = add3(_dot(tri_full, pieces)) + carry_ref[...]
    cs_chunk = add3(_dot(tri_chunk, pieces))
    carry_ref[...] = cs_full[ts - 1:ts, :]

    li_moves, b_moves = [], []
    for h in range(ML_HEADS):
        li_moves += [(GATE_ML_I + h, ML_HEADS + h, 1.0), (GATE_ML_I + h, 2 * ML_HEADS + h, 1.0)]
        b_moves += [(GATE_ML_F + h, h, 1.0), (GATE_ML_F + h, ML_HEADS + h, -1.0),
                    (GATE_ML_F + h, 2 * ML_HEADS + h, -1.0)]
    g = _place(vals, li_moves) + _place(cs_chunk, b_moves)
    pos = lax.broadcasted_iota(jnp.int32, (ts, LANES), 0) % CHUNK
    cmax = g
    d = 1
    while d < CHUNK:
        cmax = jnp.where(pos >= d, jnp.maximum(cmax, pltpu.roll(cmax, d, axis=0)), cmax)
        d *= 2
    g = jnp.where(lane >= 2 * ML_HEADS, cmax, g)
    col_ref[0] = g
    row_ref[0] = g.T[:GATE_ROWS, :]

    src = lax.broadcasted_iota(jnp.int32, (3 * LANES, ATT_SLOTS), 0)
    dst = lax.broadcasted_iota(jnp.int32, (3 * LANES, ATT_SLOTS), 1)
    place = (src % LANES < ATT_HEADS) & (dst == (src % LANES) * LANES + ATT_HEAD_DIM + src // LANES)
    decay = cs_full * -LOG2E
    decay_cols = _dot(jnp.concatenate(_split3(decay), axis=-1),
                      jnp.where(place, 1.0, 0.0).astype(BF16))
    kf = k_ref[0].astype(F32)
    ka_ref[0] = (kf + decay_cols).astype(BF16)

    lane_row = lax.broadcasted_iota(jnp.int32, (1, LANES), 1)
    knorm = jnp.zeros((1, LANES), F32)
    for h in range(ATT_HEADS):
        kh = kf[:, h * LANES:(h + 1) * LANES]
        sq = jnp.max(jnp.sum(kh * kh, axis=-1, keepdims=True), axis=0, keepdims=True)
        knorm = jnp.where(lane_row == h, jnp.sqrt(sq), knorm)
    kmax_ref[0] = jnp.maximum(kmax_ref[0], jnp.broadcast_to(knorm, (SUBLANES, LANES)))

    rows = []
    running = bcarry_ref[...]
    for c in range(ts // tk):
        running = jnp.maximum(running, jnp.max(decay[c * tk:(c + 1) * tk, :], axis=0, keepdims=True))
        rows.append(running)
    bcarry_ref[...] = running
    rows += [running] * (SUBLANES - len(rows))
    bpre_ref[0, 0] = jnp.concatenate(rows, axis=0)


def _gates(small, bias, k, l, ts, tk):
    b, s, _ = small.shape
    assert ts % tk == 0 and ts // tk <= SUBLANES
    row = lambda w: pl.BlockSpec((1, ts, w), lambda bi, si: (bi, si, 0))
    return pl.pallas_call(
        functools.partial(_gates_kernel, ts=ts, tk=tk),
        out_shape=(jax.ShapeDtypeStruct((b, s, LANES), F32),
                   jax.ShapeDtypeStruct((b, GATE_ROWS, s), F32),
                   jax.ShapeDtypeStruct((b, s, ATT_SLOTS), BF16),
                   jax.ShapeDtypeStruct((b, SUBLANES, LANES), F32),
                   jax.ShapeDtypeStruct((b, s // ts, SUBLANES, LANES), F32)),
        grid=(b, s // ts),
        in_specs=[row(LANES), _resident((1, LANES), l), row(ATT_SLOTS)],
        out_specs=(row(LANES),
                   pl.BlockSpec((1, GATE_ROWS, ts), lambda bi, si: (bi, 0, si)),
                   row(ATT_SLOTS),
                   pl.BlockSpec((1, SUBLANES, LANES), lambda bi, si: (bi, 0, 0)),
                   pl.BlockSpec((1, 1, SUBLANES, LANES), lambda bi, si: (bi, si, 0, 0))),
        scratch_shapes=[pltpu.VMEM((1, LANES), F32), pltpu.VMEM((1, LANES), F32)],
        compiler_params=_cparams(("parallel", "arbitrary")),
        name="gates",
    )(small, bias, k)


def _att_kernel(qt_ref, k_ref, vt_ref, kmax_ref, bpre_ref, o_ref, s0_ref, s1_ref, acc_ref,
                *, tq, tk, hb):
    i = pl.program_id(2)
    nd = tq // tk
    s_refs = (s0_ref, s1_ref)

    q_pad = jnp.zeros((LANES - ATT_V_ROWS, tq), BF16)
    q_slots = [jnp.concatenate([qt_ref[0, hh, 0], q_pad], axis=0) for hh in range(hb)]

    def scores(hh, chunk):
        kj = k_ref[0, pl.ds(pl.multiple_of(chunk * tk, tk), tk), hh * LANES:(hh + 1) * LANES]
        return _dot(kj, q_slots[hh])

    def visit(par, carry, cur_chunk, next_chunk, next_mask=None):
        oth = 1 - par
        out = []
        for hh in range(hb):
            m, cmax = carry[hh]
            s_next = scores(hh, next_chunk)
            m_new = jnp.maximum(m, cmax)
            alpha = jnp.exp2(m - m_new)
            p = jnp.exp2(s_refs[par][hh] - m_new).astype(BF16)
            acc_ref[hh] = acc_ref[hh] * alpha + _dot(vt_ref[0, hh, cur_chunk], p)
            if next_mask is not None:
                s_next = jnp.where(next_mask, s_next, NEG_BIG)
            s_refs[oth][hh] = s_next
            out.append((m_new, jnp.max(s_next, axis=0, keepdims=True)))
        return tuple(out)

    key = lax.broadcasted_iota(jnp.int32, (tk, tq), 0)
    qry = lax.broadcasted_iota(jnp.int32, (tk, tq), 1)
    init = []
    for hh in range(hb):
        s_first = jnp.where(key <= qry, scores(hh, nd * i), NEG_BIG)
        s0_ref[hh] = s_first
        acc_ref[hh] = jnp.zeros((ATT_V_ROWS, tq), F32)
        init.append((jnp.full((1, tq), NEG_BIG, F32), jnp.max(s_first, axis=0, keepdims=True)))
    carry = tuple(init)
    below = nd * i
    for d in range(nd):
        if d + 1 < nd:
            carry = visit(d % 2, carry, below + d, below + d + 1, key + (d + 1) * tk <= qry)
        else:
            carry = visit(d % 2, carry, below + d, jnp.maximum(below - 1, 0))

    lane = lax.broadcasted_iota(jnp.int32, (1, LANES), 1)
    bound = jnp.full((1, LANES), NEG_BIG, F32)
    for hh in range(hb):
        h = pl.program_id(1) * hb + hh
        qf = qt_ref[0, hh, 0][:ATT_HEAD_DIM, :].astype(F32)
        qmax = jnp.sqrt(jnp.max(jnp.sum(qf * qf, axis=0, keepdims=True), axis=-1, keepdims=True))
        mmin = jnp.min(carry[hh][0], axis=-1, keepdims=True)
        bound = jnp.where(lane == h, qmax * kmax_ref[0, 0:1, :] - mmin, bound)
    chunk_id = lax.broadcasted_iota(jnp.int32, bpre_ref.shape[1:], 0)
    live = (bpre_ref[0] + bound >= -ATT_SKIP_EXPONENT) & (chunk_id < below)
    n_live = jnp.max(jnp.sum(jnp.where(live, 1.0, 0.0), axis=0, keepdims=True)).astype(jnp.int32)
    n_pairs = (n_live + 1) // 2

    def pair(w, carry):
        top = below - 2 * w
        carry = visit(0, carry, top - 1, top - 2)
        return visit(1, carry, top - 2, jnp.maximum(top - 3, 0))

    def two_pairs(w2, carry):
        return pair(2 * w2 + 1, pair(2 * w2, carry))

    carry = lax.fori_loop(0, n_pairs // 2, two_pairs, carry)
    carry = lax.fori_loop(2 * (n_pairs // 2), n_pairs, pair, carry)
    for pr in range(hb // 2):
        halves = []
        for hh in (2 * pr, 2 * pr + 1):
            acc = acc_ref[hh]
            halves.append(acc[:ATT_HEAD_DIM, :] / acc[ATT_HEAD_DIM:ATT_HEAD_DIM + 1, :])
        both = jnp.concatenate(halves, axis=0).T
        o_ref[0, :, pr * LANES:(pr + 1) * LANES] = both.astype(o_ref.dtype)


def _attention(qt, k, vt, kmax, bpre, tq, tk, hb):
    b, s, _ = k.shape
    assert tq % (2 * tk) == 0 and bpre.shape == (b, s // tk, LANES)
    return pl.pallas_call(
        functools.partial(_att_kernel, tq=tq, tk=tk, hb=hb),
        out_shape=jax.ShapeDtypeStruct((b, s, ATT_WIDTH), BF16),
        grid=(b, ATT_HEADS // hb, s // tq),
        in_specs=[
            pl.BlockSpec((1, hb, 1, ATT_V_ROWS, tq), lambda bi, hg, i: (bi, hg, i, 0, 0)),
            pl.BlockSpec((1, s, hb * LANES), lambda bi, hg, i: (bi, 0, hg),
                         pipeline_mode=pl.Buffered(1)),
            pl.BlockSpec((1, hb, s // tk, ATT_V_ROWS, tk), lambda bi, hg, i: (bi, hg, 0, 0, 0),
                         pipeline_mode=pl.Buffered(1)),
            pl.BlockSpec((1, SUBLANES, LANES), lambda bi, hg, i: (bi, 0, 0)),
            pl.BlockSpec((1, s // tk, LANES), lambda bi, hg, i: (bi, 0, 0)),
        ],
        out_specs=pl.BlockSpec((1, tq, hb * ATT_HEAD_DIM), lambda bi, hg, i: (bi, i, hg)),
        scratch_shapes=[pltpu.VMEM((hb, tk, tq), F32), pltpu.VMEM((hb, tk, tq), F32),
                        pltpu.VMEM((hb, ATT_V_ROWS, tq), F32)],
        compiler_params=_cparams(("parallel", "parallel", "arbitrary")),
        name="fox_attention",
    )(qt, k, vt, kmax, bpre)


def _mlstm_kernel(q_ref, k_ref, v_ref, gcol_ref, grow_ref, spread_ref, nw_ref, o_ref,
                  st_ref, m_ref, *, nsub):
    @pl.when(pl.program_id(1) == 0)
    def _():
        st_ref[...] = jnp.zeros_like(st_ref)
        m_ref[...] = jnp.zeros_like(m_ref)

    L = CHUNK
    D = ML_HEAD_DIM
    H = range(ML_HEADS)
    r = lax.broadcasted_iota(jnp.int32, (L, L), 0)
    c = lax.broadcasted_iota(jnp.int32, (L, L), 1)
    causal = c <= r
    ones = jnp.ones((L, D), BF16)
    mean_mat = jnp.full((D, D), 1.0 / D, BF16)
    two = lambda x: jnp.concatenate([x, x], axis=-1)

    st = [st_ref[h] for h in H]
    m = [m_ref[h] for h in H]
    pre = []
    for sc in range(nsub):
        rows = slice(sc * L, (sc + 1) * L)
        head = lambda ref, h, rows=rows: ref[0, rows, h * D:(h + 1) * D]
        rep = _dot(jnp.concatenate(_split3(gcol_ref[0, rows, :]), axis=-1), spread_ref[...])
        qk = [_dot_nt(head(q_ref, h), head(k_ref, h)) for h in H]
        kt = [head(k_ref, h).astype(F32).T.astype(BF16) for h in H]
        pre.append((rows, head, rep, qk, kt))

    for sc in range(nsub):
        rows, head, rep, qk, kt = pre[sc]
        col = lambda j, h: rep[:, (j * ML_HEADS + h) * LANES:(j * ML_HEADS + h + 1) * LANES]
        inter = [_dot(head(q_ref, h), st[h].astype(BF16)) for h in H]
        m_t, w_intra, w_inter, decay, wv, m_next = [], [], [], [], [], []
        for h in H:
            b, gq, cm = col(0, h), col(1, h), col(2, h)
            gq_row = grow_ref[0, ML_HEADS + h:ML_HEADS + h + 1, rows]
            a = b + m[h]
            m_t.append(jnp.maximum(a, b + cm))
            w_intra.append(jnp.exp(jnp.where(causal, b + gq_row, NEG_BIG) - m_t[h]))
            w_inter.append(jnp.exp(a - m_t[h]))
            b_last = b[L - 1:L, :]
            m_new = jnp.maximum(b_last + m[h], b_last + cm[L - 1:L, :])
            w_s = jnp.exp(b_last + gq - m_new)
            decay.append(jnp.exp(b_last + m[h] - m_new))
            wv.append(jnp.concatenate([w_s * head(v_ref, h).astype(F32), w_s],
                                      axis=-1).astype(BF16))
            m_next.append(m_new)
        qkw = [(qk[h] * w_intra[h]).astype(BF16) for h in H]
        both = [_dot(qkw[h], jnp.concatenate([head(v_ref, h), ones], axis=-1))
                + two(w_inter[h]) * inter[h] for h in H]
        st = [two(decay[h]) * st[h] + _dot(kt[h], wv[h]) for h in H]
        m = m_next
        hh = [both[h][:, :D] / jnp.maximum(jnp.abs(both[h][:, D:]), jnp.exp(-m_t[h])) for h in H]
        ms = []
        for h in H:
            sq = hh[h] * hh[h]
            sq_hi = sq.astype(BF16)
            sq_lo = (sq - sq_hi.astype(F32)).astype(BF16)
            ms.append(_dot(sq_hi, mean_mat) + _dot(sq_lo, mean_mat))
        for h in H:
            o_ref[0, rows, h * D:(h + 1) * D] = (hh[h] * lax.rsqrt(ms[h] + EPS)
                                                 * nw_ref[:, h * D:(h + 1) * D])
    for h in H:
        st_ref[h] = st[h]
        m_ref[h] = m[h]


def _mlstm(mq, mk, mv, gcol, grow, norm_w, l, nsub):
    b, s, _ = mq.shape
    rows = nsub * CHUNK
    nc = s // rows
    row = lambda w: pl.BlockSpec((1, rows, w), lambda bi, ci: (bi, ci, 0))
    src = lax.broadcasted_iota(jnp.int32, (3 * LANES, ML_REP), 0) % LANES
    dst = lax.broadcasted_iota(jnp.int32, (3 * LANES, ML_REP), 1) // LANES
    spread = (src == dst).astype(BF16)
    return pl.pallas_call(
        functools.partial(_mlstm_kernel, nsub=nsub),
        out_shape=jax.ShapeDtypeStruct((b, s, ML_WIDTH), F32),
        grid=(b, nc),
        in_specs=[
            row(ML_WIDTH), row(ML_WIDTH), row(ML_WIDTH), row(LANES),
            pl.BlockSpec((1, GATE_ROWS, rows), lambda bi, ci: (bi, 0, ci)),
            _resident((3 * LANES, ML_REP)),
            _resident((1, ML_WIDTH), l),
        ],
        out_specs=row(ML_WIDTH),
        scratch_shapes=[
            pltpu.VMEM((ML_HEADS, ML_HEAD_DIM, 2 * ML_HEAD_DIM), F32),
            pltpu.VMEM((ML_HEADS, 1, LANES), F32),
        ],
        compiler_params=_cparams(("parallel", "arbitrary")),
        name="mlstm",
    )(mq, mk, mv, gcol, grow, spread, norm_w)


def _gelu_tanh(x):
    return 0.5 * x * (1.0 + jnp.tanh(0.7978845608028654 * (x + 0.044715 * (x * x * x))))


def _merge_kernel(x_ref, yatt_ref, hm_ref, g_ref, wo_ref, wuv_ref, wgt_ref, ng_ref,
                  wsp_ref, bsp_ref, wba_ref, wbm_ref, wbg_ref, wout_ref, o_ref, *, tm):
    x = x_ref[...]
    hn = _rms(x, g_ref[...]).astype(BF16)
    branch_gate = lambda j: jax.nn.sigmoid(_dot(hn, wgt_ref[:, j * D_MODEL:(j + 1) * D_MODEL]))

    uv_pre = _dot(hn, wuv_ref[...])
    o_pre = _dot(hn, wo_ref[...])
    merged = branch_gate(0) * _dot(yatt_ref[...], wba_ref[...])

    y_ml = jax.nn.sigmoid(o_pre) * hm_ref[...]
    merged = merged + branch_gate(1) * _dot(y_ml.astype(BF16), wbm_ref[...])

    uv = _gelu_tanh(uv_pre)
    u = uv[:, :GM_WIDTH]
    vn = _rms(uv[:, GM_WIDTH:], ng_ref[...]).astype(BF16)
    r = lax.broadcasted_iota(jnp.int32, (CHUNK, CHUNK), 0)
    c = lax.broadcasted_iota(jnp.int32, (CHUNK, CHUNK), 1)
    mixed = []
    for gi in range(GM_GROUPS):
        w = jnp.where(c <= r, wsp_ref[gi], 0.0).astype(BF16)
        lanes = slice(gi * GM_GROUP_DIM, (gi + 1) * GM_GROUP_DIM)
        mixed.append(jnp.concatenate(
            [_dot(w, vn[ci * CHUNK:(ci + 1) * CHUNK, lanes]) for ci in range(tm // CHUNK)], axis=0))
    y_gm = u * (jnp.concatenate(mixed, axis=-1) + bsp_ref[...])

    merged = merged + branch_gate(2) * _dot(y_gm.astype(BF16), wbg_ref[...])
    o_ref[...] = x + _dot(merged.astype(BF16), wout_ref[...])


def _merge(x2, yatt, hm, g, wo, wuv, wgt, ng, wsp, bsp, wba, wbm, wbg, wout, l, tm):
    n = x2.shape[0]
    row = lambda w: pl.BlockSpec((tm, w), lambda i: (i, 0))
    return pl.pallas_call(
        functools.partial(_merge_kernel, tm=tm),
        out_shape=jax.ShapeDtypeStruct((n, D_MODEL), F32),
        grid=(n // tm,),
        in_specs=[
            row(D_MODEL), row(ATT_WIDTH), row(ML_WIDTH),
            _resident((1, D_MODEL), l),
            _resident((D_MODEL, ML_WIDTH), l),
            _resident((D_MODEL, 2 * GM_WIDTH), l),
            _resident((D_MODEL, N_BRANCH * D_MODEL), l),
            _resident((1, GM_WIDTH), l),
            _resident((GM_GROUPS, CHUNK, CHUNK), l),
            _resident((tm, GM_WIDTH), l),
            _resident((ATT_WIDTH, D_MODEL), l),
            _resident((ML_WIDTH, D_MODEL), l),
            _resident((GM_WIDTH, D_MODEL), l),
            _resident((D_MODEL, D_MODEL), l),
        ],
        out_specs=row(D_MODEL),
        compiler_params=_cparams(("parallel",)),
        name="merge",
    )(x2, yatt, hm, g, wo, wuv, wgt, ng, wsp, bsp, wba, wbm, wbg, wout)


def _tile(n, pref):
    t = min(n, pref)
    assert n % t == 0, (n, t)
    return t


def _prepare_layer_params(p, tm_merge):
    depth = p["w_in"].shape[0]
    o_att = 3 * ATT_WIDTH
    o_mqk = o_att + ATT_HEADS
    o_mv = o_mqk + 2 * ML_WIDTH
    o_mi = o_mv + ML_WIDTH
    o_mf = o_mi + ML_HEADS
    o_mo = o_mf + ML_HEADS
    o_uv = o_mo + ML_WIDTH
    o_gt = o_uv + 2 * GM_WIDTH
    w_in = p["w_in"]
    w_small = jnp.concatenate(
        [w_in[:, :, o_att:o_mqk], w_in[:, :, o_mi:o_mf], w_in[:, :, o_mf:o_mo],
         jnp.zeros((depth, D_MODEL, LANES - GATE_ROWS), F32)], axis=-1)
    b_small = jnp.concatenate(
        [p["b_f_att"], p["b_i_ml"], p["b_f_ml"], jnp.zeros((depth, LANES - GATE_ROWS), F32)],
        axis=-1)[:, None, :]
    reps = tm_merge // CHUNK
    bf = lambda a: a.astype(BF16)
    row = lambda a: a[:, None, :]

    def slots(w, width):
        w = w.reshape(depth, D_MODEL, ATT_HEADS, ATT_HEAD_DIM)
        w = jnp.pad(w, ((0, 0), (0, 0), (0, 0), (0, width - ATT_HEAD_DIM)))
        return w.reshape(depth, D_MODEL, ATT_HEADS * width)

    w_q = slots(w_in[:, :, :ATT_WIDTH], ATT_V_ROWS) * (ATT_HEAD_DIM ** -0.5 * LOG2E)
    w_k = slots(w_in[:, :, ATT_WIDTH:2 * ATT_WIDTH], LANES)
    w_v = slots(w_in[:, :, 2 * ATT_WIDTH:o_att], ATT_V_ROWS)
    return dict(
        norm_ffn1=row(p["norm_ffn1"]), ffn1_gate=bf(p["ffn1_gate"]), ffn1_up=bf(p["ffn1_up"]),
        ffn1_down=bf(p["ffn1_down"]),
        norm_mix=row(p["norm_mix"]),
        w_qt=bf(jnp.swapaxes(w_q, 1, 2)), w_k=bf(w_k), w_vt=bf(jnp.swapaxes(w_v, 1, 2)),
        w_small=bf(w_small), b_small=b_small,
        w_mqk=bf(w_in[:, :, o_mqk:o_mv]), w_mv=bf(w_in[:, :, o_mv:o_mi]),
        w_mo=bf(w_in[:, :, o_mo:o_uv]), w_uv=bf(w_in[:, :, o_uv:o_gt]), w_gt=bf(w_in[:, :, o_gt:]),
        conv_ml=p["conv_ml"], norm_ml_head=row(p["norm_ml_head"]), norm_gmlp=row(p["norm_gmlp"]),
        w_spatial=p["w_spatial"],
        b_spatial=jnp.tile(jnp.repeat(jnp.swapaxes(p["b_spatial"], 1, 2), GM_GROUP_DIM, axis=2),
                           (1, reps, 1)),
        w_br_att=bf(p["w_br_att"]), w_br_ml=bf(p["w_br_ml"]), w_br_gmlp=bf(p["w_br_gmlp"]),
        w_out=bf(p["w_out"]),
        norm_ffn2=row(p["norm_ffn2"]), ffn2_gate=bf(p["ffn2_gate"]), ffn2_up=bf(p["ffn2_up"]),
        ffn2_down=bf(p["ffn2_down"]),
    )


def _layer(x2, lp, l, b, s, tiles, final_g):
    n = b * s
    x2 = _ffn(x2, lp["norm_ffn1"], lp["ffn1_gate"], lp["ffn1_up"], lp["ffn1_down"], l,
              tiles["ffn"])
    qt, k, vt, small, mq, mk, mv = _inproj(
        x2.reshape(b, s, D_MODEL), lp["norm_mix"], lp["w_qt"], lp["w_k"], lp["w_vt"],
        lp["w_small"], lp["w_mqk"], lp["w_mv"], lp["conv_ml"], l, tiles["inproj"], tiles["att_q"],
        tiles["att_k"])
    gcol, grow, k, kmax, bpre = _gates(small, lp["b_small"], k, l, tiles["gates"], tiles["att_k"])
    bpre = bpre[:, :, :tiles["gates"] // tiles["att_k"], :].reshape(b, s // tiles["att_k"], LANES)
    yatt = _attention(qt, k, vt, kmax, bpre, tiles["att_q"], tiles["att_k"], tiles["att_heads"])
    hm = _mlstm(mq, mk, mv, gcol, grow, lp["norm_ml_head"], l, tiles["mlstm_chunks"])
    x2 = _merge(x2, yatt.reshape(n, ATT_WIDTH), hm.reshape(n, ML_WIDTH), lp["norm_mix"],
                lp["w_mo"], lp["w_uv"], lp["w_gt"], lp["norm_gmlp"], lp["w_spatial"],
                lp["b_spatial"], lp["w_br_att"], lp["w_br_ml"], lp["w_br_gmlp"], lp["w_out"],
                l, tiles["merge"])
    return _ffn(x2, lp["norm_ffn2"], lp["ffn2_gate"], lp["ffn2_up"], lp["ffn2_down"], l,
                tiles["ffn"], final_g)


def _tiles_for(n, s):
    att_q = _tile(s, 512)
    return dict(ffn=_tile(n, 512), inproj=_tile(s, 1024), gates=_tile(s, 512), att_q=att_q,
                att_k=min(att_q // 2, 256), att_heads=2, merge=_tile(s, 512),
                mlstm_chunks=_tile(s, 8 * CHUNK) // CHUNK)


def _trunk(x, params, norm_final):
    b, s, _ = x.shape
    n = b * s
    tiles = _tiles_for(n, s)
    stacked = _prepare_layer_params(params, tiles["merge"])
    depth = params["w_in"].shape[0]
    x2 = x.reshape(n, D_MODEL)
    for l in range(depth):
        x2 = _layer(x2, stacked, l, b, s, tiles, norm_final[None, :] if l == depth - 1 else None)
    return x2.reshape(b, s, D_MODEL)


def kernel(x, norm_ffn1, ffn1_gate, ffn1_up, ffn1_down, norm_mix, w_in, b_f_att, b_i_ml, b_f_ml, conv_ml, norm_ml_head, norm_gmlp, w_spatial, b_spatial, w_br_att, w_br_ml, w_br_gmlp, w_out, norm_ffn2, ffn2_gate, ffn2_up, ffn2_down, norm_final):
    params = dict(norm_ffn1=norm_ffn1, ffn1_gate=ffn1_gate, ffn1_up=ffn1_up, ffn1_down=ffn1_down,
                  norm_mix=norm_mix, w_in=w_in, b_f_att=b_f_att, b_i_ml=b_i_ml, b_f_ml=b_f_ml,
                  conv_ml=conv_ml, norm_ml_head=norm_ml_head, norm_gmlp=norm_gmlp,
                  w_spatial=w_spatial, b_spatial=b_spatial, w_br_att=w_br_att, w_br_ml=w_br_ml,
                  w_br_gmlp=w_br_gmlp, w_out=w_out, norm_ffn2=norm_ffn2, ffn2_gate=ffn2_gate,
                  ffn2_up=ffn2_up, ffn2_down=ffn2_down)
    return _trunk(x, params, norm_final)
```

```python
import functools

import jax
import jax.numpy as jnp
from jax import lax
from jax.experimental import pallas as pl
from jax.experimental.pallas import tpu as pltpu

D_MODEL = 1024
ATT_HEADS = 8
ATT_HEAD_DIM = 64
ATT_WIDTH = ATT_HEADS * ATT_HEAD_DIM
ML_HEADS = 4
ML_HEAD_DIM = 128
ML_WIDTH = ML_HEADS * ML_HEAD_DIM
CONV_WIDTH = 4
GM_GROUPS = 4
GM_GROUP_DIM = 128
GM_WIDTH = GM_GROUPS * GM_GROUP_DIM
CHUNK = 128
D_FF = 2816
FFN_RES = 0.5
N_BRANCH = 3
EPS = 1e-6

LANES = 128
SUBLANES = 8
ATT_SLOTS = ATT_HEADS * LANES
ATT_V_ROWS = 80
LOG2E = 1.4426950408889634
ATT_SKIP_EXPONENT = 160.0
VMEM_LIMIT = 56 * 1024 * 1024
NEG_BIG = -1e30

GATE_ATT_F = 0
GATE_ML_I = ATT_HEADS
GATE_ML_F = ATT_HEADS + ML_HEADS
GATE_ROWS = 16
ML_REP = 3 * ML_HEADS * 128

F32 = jnp.float32
BF16 = jnp.bfloat16


def _cparams(sem):
    return pltpu.CompilerParams(dimension_semantics=sem, vmem_limit_bytes=VMEM_LIMIT)


def _resident(shape, layer=None):
    nd = len(shape)
    if layer is None:
        return pl.BlockSpec(shape, lambda *_: (0,) * nd, pipeline_mode=pl.Buffered(1))
    return pl.BlockSpec((None,) + tuple(shape), lambda *_: (layer,) + (0,) * nd,
                        pipeline_mode=pl.Buffered(1))


def _rms(x, g):
    ms = jnp.mean(x * x, axis=-1, keepdims=True)
    return x * lax.rsqrt(ms + EPS) * g


def _log_sigmoid(z):
    return jnp.minimum(z, 0.0) - jnp.log1p(jnp.exp(-jnp.abs(z)))


def _dot(a, b):
    return jnp.dot(a, b, preferred_element_type=F32)


def _dot_nt(a, b):
    return lax.dot_general(a, b, (((1,), (1,)), ((), ())), preferred_element_type=F32)


def _ffn_kernel(x_ref, g_ref, wg_ref, wu_ref, wd_ref, *rest, final):
    o_ref = rest[-1]
    x = x_ref[...]
    hn = _rms(x, g_ref[...]).astype(BF16)
    gate = _dot(hn, wg_ref[...])
    up = _dot(hn, wu_ref[...])
    act = (gate * jax.nn.sigmoid(gate) * up).astype(BF16)
    y = x + FFN_RES * _dot(act, wd_ref[...])
    o_ref[...] = _rms(y, rest[0][...]) if final else y


def _ffn(x2, g, wg, wu, wd, l, tm, final_g=None):
    n = x2.shape[0]
    final = final_g is not None
    return pl.pallas_call(
        functools.partial(_ffn_kernel, final=final),
        out_shape=jax.ShapeDtypeStruct((n, D_MODEL), F32),
        grid=(n // tm,),
        in_specs=[
            pl.BlockSpec((tm, D_MODEL), lambda i: (i, 0)),
            _resident((1, D_MODEL), l),
            _resident((D_MODEL, D_FF), l),
            _resident((D_MODEL, D_FF), l),
            _resident((D_FF, D_MODEL), l),
        ] + ([_resident((1, D_MODEL))] if final else []),
        out_specs=pl.BlockSpec((tm, D_MODEL), lambda i: (i, 0)),
        compiler_params=_cparams(("parallel",)),
        name="ffn_final" if final else "ffn",
    )(x2, g, wg, wu, wd, *([final_g] if final else []))


def _inproj_kernel(x_ref, xp_ref, g_ref, wqt_ref, wk_ref, wvt_ref, wsm_ref, wqk_ref, wv_ref,
                   cw_ref, qt_ref, k_ref, vt_ref, sm_ref, mq_ref, mk_ref, mv_ref, *, tm, tq, tk):
    hn_before = _rms(xp_ref[0], g_ref[...])
    hn_before = jnp.where(pl.program_id(1) > 0, hn_before, 0.0)
    cw = cw_ref[...]
    slot_row = lax.broadcasted_iota(jnp.int32, (ATT_HEADS * ATT_V_ROWS, tq), 0) % ATT_V_ROWS
    for part in range(tm // tq):
        rows = slice(part * tq, (part + 1) * tq)
        hn = _rms(x_ref[0, rows, :], g_ref[...])
        hb = hn.astype(BF16)
        hnt = hn.T.astype(BF16)
        qt = _dot(wqt_ref[...], hnt)
        qt = jnp.where((slot_row >= ATT_HEAD_DIM) & (slot_row < ATT_HEAD_DIM + 3), 1.0, qt)
        qt_ref[0, :, part] = qt.astype(BF16).reshape(ATT_HEADS, ATT_V_ROWS, tq)
        vt = _dot(wvt_ref[...], hnt)
        vt = jnp.where(slot_row == ATT_HEAD_DIM, 1.0, vt).astype(BF16)
        for c in range(tq // tk):
            vt_ref[0, :, part * (tq // tk) + c] = vt[:, c * tk:(c + 1) * tk].reshape(
                ATT_HEADS, ATT_V_ROWS, tk)
        k_ref[0, rows, :] = _dot(hb, wk_ref[...]).astype(BF16)
        sm_ref[0, rows, :] = _dot(hb, wsm_ref[...])
        mv_ref[0, rows, :] = _dot(hb, wv_ref[...]).astype(BF16)

        xcat = _dot(jnp.concatenate([hn_before, hn], axis=0).astype(BF16), wqk_ref[...])
        proj = xcat[SUBLANES:, :]
        conv = proj * cw[CONV_WIDTH - 1:CONV_WIDTH, :]
        for j in range(CONV_WIDTH - 1):
            off = SUBLANES - (CONV_WIDTH - 1) + j
            conv = conv + xcat[off:off + tq, :] * cw[j:j + 1, :]
        act = conv * jax.nn.sigmoid(conv)
        mq_ref[0, rows, :] = (act[:, :ML_WIDTH] * (ML_HEAD_DIM ** -0.5)).astype(BF16)
        mk_ref[0, rows, :] = act[:, ML_WIDTH:].astype(BF16)
        hn_before = hn[tq - SUBLANES:, :]


def _inproj(x3, g, wqt, wk, wvt, wsm, wqk, wv, cw, l, tm, tq, tk):
    b, s, _ = x3.shape
    row = lambda w: pl.BlockSpec((1, tm, w), lambda bi, si: (bi, si, 0))
    rows_before = pl.BlockSpec((1, SUBLANES, D_MODEL),
                               lambda bi, si: (bi, jnp.maximum(si * (tm // SUBLANES) - 1, 0), 0))
    tr = lambda rows, t: pl.BlockSpec((1, ATT_HEADS, tm // t, rows, t),
                                      lambda bi, si: (bi, 0, si, 0, 0))
    tr_shape = lambda rows, t: jax.ShapeDtypeStruct((b, ATT_HEADS, s // t, rows, t), BF16)
    return pl.pallas_call(
        functools.partial(_inproj_kernel, tm=tm, tq=tq, tk=tk),
        out_shape=(
            tr_shape(ATT_V_ROWS, tq),
            jax.ShapeDtypeStruct((b, s, ATT_SLOTS), BF16),
            tr_shape(ATT_V_ROWS, tk),
            jax.ShapeDtypeStruct((b, s, LANES), F32),
            jax.ShapeDtypeStruct((b, s, ML_WIDTH), BF16),
            jax.ShapeDtypeStruct((b, s, ML_WIDTH), BF16),
            jax.ShapeDtypeStruct((b, s, ML_WIDTH), BF16),
        ),
        grid=(b, s // tm),
        in_specs=[
            row(D_MODEL),
            rows_before,
            _resident((1, D_MODEL), l),
            _resident((ATT_HEADS * ATT_V_ROWS, D_MODEL), l),
            _resident((D_MODEL, ATT_SLOTS), l),
            _resident((ATT_HEADS * ATT_V_ROWS, D_MODEL), l),
            _resident((D_MODEL, LANES), l),
            _resident((D_MODEL, 2 * ML_WIDTH), l),
            _resident((D_MODEL, ML_WIDTH), l),
            _resident((CONV_WIDTH, 2 * ML_WIDTH), l),
        ],
        out_specs=(tr(ATT_V_ROWS, tq), row(ATT_SLOTS), tr(ATT_V_ROWS, tk), row(LANES),
                   row(ML_WIDTH), row(ML_WIDTH), row(ML_WIDTH)),
        compiler_params=_cparams(("parallel", "parallel")),
        name="inproj",
    )(x3, x3, g, wqt, wk, wvt, wsm, wqk, wv, cw)


def _split3(x):
    hi = x.astype(BF16)
    r1 = x - hi.astype(F32)
    mid = r1.astype(BF16)
    lo = (r1 - mid.astype(F32)).astype(BF16)
    return hi, mid, lo


def _place(x, moves):
    r = lax.broadcasted_iota(jnp.int32, (LANES, LANES), 0)
    c = lax.broadcasted_iota(jnp.int32, (LANES, LANES), 1)
    mat = jnp.zeros((LANES, LANES), F32)
    for src, dst, sign in moves:
        mat = jnp.where((r == src) & (c == dst), sign, mat)
    mat3 = jnp.concatenate([mat.astype(BF16)] * 3, axis=0)
    return _dot(jnp.concatenate(_split3(x), axis=-1), mat3)


def _gates_kernel(sm_ref, bias_ref, k_ref, col_ref, row_ref, ka_ref, kmax_ref, bpre_ref,
                  carry_ref, bcarry_ref, *, ts, tk):
    @pl.when(pl.program_id(1) == 0)
    def _():
        carry_ref[...] = jnp.zeros_like(carry_ref)
        bcarry_ref[...] = jnp.full_like(bcarry_ref, NEG_BIG)
        kmax_ref[...] = jnp.zeros_like(kmax_ref)

    z = sm_ref[0] + bias_ref[...]
    lane = lax.broadcasted_iota(jnp.int32, z.shape, 1)
    is_att = lane < GATE_ML_I
    is_mlf = (lane >= GATE_ML_F) & (lane < GATE_ML_F + ML_HEADS)
    vals = jnp.where(is_att | is_mlf, _log_sigmoid(z), z)

    r = lax.broadcasted_iota(jnp.int32, (ts, ts), 0)
    c = lax.broadcasted_iota(jnp.int32, (ts, ts), 1)
    tri = c <= r
    tri_full = jnp.where(tri, 1.0, 0.0).astype(BF16)
    tri_chunk = jnp.where(tri & ((r // CHUNK) == (c // CHUNK)), 1.0, 0.0).astype(BF16)
    pieces = jnp.concatenate(_split3(vals), axis=-1)
    add3 = lambda y: y[:, :LANES] + y[:, LANES:2 * LANES] + y[:, 2 * LANES:]
    cs_full = add3(_dot(tri_full, pieces)) + carry_ref[...]
    cs_chunk = add3(_dot(tri_chunk, pieces))
    carry_ref[...] = cs_full[ts - 1:ts, :]

    li_moves, b_moves = [], []
    for h in range(ML_HEADS):
        li_moves += [(GATE_ML_I + h, ML_HEADS + h, 1.0), (GATE_ML_I + h, 2 * ML_HEADS + h, 1.0)]
        b_moves += [(GATE_ML_F + h, h, 1.0), (GATE_ML_F + h, ML_HEADS + h, -1.0),
                    (GATE_ML_F + h, 2 * ML_HEADS + h, -1.0)]
    g = _place(vals, li_moves) + _place(cs_chunk, b_moves)
    pos = lax.broadcasted_iota(jnp.int32, (ts, LANES), 0) % CHUNK
    cmax = g
    d = 1
    while d < CHUNK:
        cmax = jnp.where(pos >= d, jnp.maximum(cmax, pltpu.roll(cmax, d, axis=0)), cmax)
        d *= 2
    g = jnp.where(lane >= 2 * ML_HEADS, cmax, g)
    col_ref[0] = g
    row_ref[0] = g.T[:GATE_ROWS, :]

    src = lax.broadcasted_iota(jnp.int32, (3 * LANES, ATT_SLOTS), 0)
    dst = lax.broadcasted_iota(jnp.int32, (3 * LANES, ATT_SLOTS), 1)
    place = (src % LANES < ATT_HEADS) & (dst == (src % LANES) * LANES + ATT_HEAD_DIM + src // LANES)
    decay = cs_full * -LOG2E
    decay_cols = _dot(jnp.concatenate(_split3(decay), axis=-1),
                      jnp.where(place, 1.0, 0.0).astype(BF16))
    kf = k_ref[0].astype(F32)
    ka_ref[0] = (kf + decay_cols).astype(BF16)

    lane_row = lax.broadcasted_iota(jnp.int32, (1, LANES), 1)
    knorm = jnp.zeros((1, LANES), F32)
    for h in range(ATT_HEADS):
        kh = kf[:, h * LANES:(h + 1) * LANES]
        sq = jnp.max(jnp.sum(kh * kh, axis=-1, keepdims=True), axis=0, keepdims=True)
        knorm = jnp.where(lane_row == h, jnp.sqrt(sq), knorm)
    kmax_ref[0] = jnp.maximum(kmax_ref[0], jnp.broadcast_to(knorm, (SUBLANES, LANES)))

    rows = []
    running = bcarry_ref[...]
    for c in range(ts // tk):
        running = jnp.maximum(running, jnp.max(decay[c * tk:(c + 1) * tk, :], axis=0, keepdims=True))
        rows.append(running)
    bcarry_ref[...] = running
    rows += [running] * (SUBLANES - len(rows))
    bpre_ref[0, 0] = jnp.concatenate(rows, axis=0)


def _gates(small, bias, k, l, ts, tk):
    b, s, _ = small.shape
    assert ts % tk == 0 and ts // tk <= SUBLANES
    row = lambda w: pl.BlockSpec((1, ts, w), lambda bi, si: (bi, si, 0))
    return pl.pallas_call(
        functools.partial(_gates_kernel, ts=ts, tk=tk),
        out_shape=(jax.ShapeDtypeStruct((b, s, LANES), F32),
                   jax.ShapeDtypeStruct((b, GATE_ROWS, s), F32),
                   jax.ShapeDtypeStruct((b, s, ATT_SLOTS), BF16),
                   jax.ShapeDtypeStruct((b, SUBLANES, LANES), F32),
                   jax.ShapeDtypeStruct((b, s // ts, SUBLANES, LANES), F32)),
        grid=(b, s // ts),
        in_specs=[row(LANES), _resident((1, LANES), l), row(ATT_SLOTS)],
        out_specs=(row(LANES),
                   pl.BlockSpec((1, GATE_ROWS, ts), lambda bi, si: (bi, 0, si)),
                   row(ATT_SLOTS),
                   pl.BlockSpec((1, SUBLANES, LANES), lambda bi, si: (bi, 0, 0)),
                   pl.BlockSpec((1, 1, SUBLANES, LANES), lambda bi, si: (bi, si, 0, 0))),
        scratch_shapes=[pltpu.VMEM((1, LANES), F32), pltpu.VMEM((1, LANES), F32)],
        compiler_params=_cparams(("parallel", "arbitrary")),
        name="gates",
    )(small, bias, k)


def _att_kernel(qt_ref, k_ref, vt_ref, kmax_ref, bpre_ref, o_ref, s0_ref, s1_ref, acc_ref,
                *, tq, tk, hb):
    i = pl.program_id(2)
    nd = tq // tk
    s_refs = (s0_ref, s1_ref)

    q_pad = jnp.zeros((LANES - ATT_V_ROWS, tq), BF16)
    q_slots = [jnp.concatenate([qt_ref[0, hh, 0], q_pad], axis=0) for hh in range(hb)]

    def scores(hh, chunk):
        kj = k_ref[0, pl.ds(pl.multiple_of(chunk * tk, tk), tk), hh * LANES:(hh + 1) * LANES]
        return _dot(kj, q_slots[hh])

    def visit(par, carry, cur_chunk, next_chunk, next_mask=None):
        oth = 1 - par
        out = []
        for hh in range(hb):
            m, cmax = carry[hh]
            s_next = scores(hh, next_chunk)
            m_new = jnp.maximum(m, cmax)
            alpha = jnp.exp2(m - m_new)
            p = jnp.exp2(s_refs[par][hh] - m_new).astype(BF16)
            acc_ref[hh] = acc_ref[hh] * alpha + _dot(vt_ref[0, hh, cur_chunk], p)
            if next_mask is not None:
                s_next = jnp.where(next_mask, s_next, NEG_BIG)
            s_refs[oth][hh] = s_next
            out.append((m_new, jnp.max(s_next, axis=0, keepdims=True)))
        return tuple(out)

    key = lax.broadcasted_iota(jnp.int32, (tk, tq), 0)
    qry = lax.broadcasted_iota(jnp.int32, (tk, tq), 1)
    init = []
    for hh in range(hb):
        s_first = jnp.where(key <= qry, scores(hh, nd * i), NEG_BIG)
        s0_ref[hh] = s_first
        acc_ref[hh] = jnp.zeros((ATT_V_ROWS, tq), F32)
        init.append((jnp.full((1, tq), NEG_BIG, F32), jnp.max(s_first, axis=0, keepdims=True)))
    carry = tuple(init)
    below = nd * i
    for d in range(nd):
        if d + 1 < nd:
            carry = visit(d % 2, carry, below + d, below + d + 1, key + (d + 1) * tk <= qry)
        else:
            carry = visit(d % 2, carry, below + d, jnp.maximum(below - 1, 0))

    lane = lax.broadcasted_iota(jnp.int32, (1, LANES), 1)
    bound = jnp.full((1, LANES), NEG_BIG, F32)
    for hh in range(hb):
        h = pl.program_id(1) * hb + hh
        qf = qt_ref[0, hh, 0][:ATT_HEAD_DIM, :].astype(F32)
        qmax = jnp.sqrt(jnp.max(jnp.sum(qf * qf, axis=0, keepdims=True), axis=-1, keepdims=True))
        mmin = jnp.min(carry[hh][0], axis=-1, keepdims=True)
        bound = jnp.where(lane == h, qmax * kmax_ref[0, 0:1, :] - mmin, bound)
    chunk_id = lax.broadcasted_iota(jnp.int32, bpre_ref.shape[1:], 0)
    live = (bpre_ref[0] + bound >= -ATT_SKIP_EXPONENT) & (chunk_id < below)
    n_live = jnp.max(jnp.sum(jnp.where(live, 1.0, 0.0), axis=0, keepdims=True)).astype(jnp.int32)
    n_pairs = (n_live + 1) // 2

    def pair(w, carry):
        top = below - 2 * w
        carry = visit(0, carry, top - 1, top - 2)
        return visit(1, carry, top - 2, jnp.maximum(top - 3, 0))

    def two_pairs(w2, carry):
        return pair(2 * w2 + 1, pair(2 * w2, carry))

    carry = lax.fori_loop(0, n_pairs // 2, two_pairs, carry)
    carry = lax.fori_loop(2 * (n_pairs // 2), n_pairs, pair, carry)
    for pr in range(hb // 2):
        halves = []
        for hh in (2 * pr, 2 * pr + 1):
            acc = acc_ref[hh]
            halves.append(acc[:ATT_HEAD_DIM, :] / acc[ATT_HEAD_DIM:ATT_HEAD_DIM + 1, :])
        both = jnp.concatenate(halves, axis=0).T
        o_ref[0, :, pr * LANES:(pr + 1) * LANES] = both.astype(o_ref.dtype)


def _attention(qt, k, vt, kmax, bpre, tq, tk, hb):
    b, s, _ = k.shape
    assert tq % (2 * tk) == 0 and bpre.shape == (b, s // tk, LANES)
    return pl.pallas_call(
        functools.partial(_att_kernel, tq=tq, tk=tk, hb=hb),
        out_shape=jax.ShapeDtypeStruct((b, s, ATT_WIDTH), BF16),
        grid=(b, ATT_HEADS // hb, s // tq),
        in_specs=[
            pl.BlockSpec((1, hb, 1, ATT_V_ROWS, tq), lambda bi, hg, i: (bi, hg, i, 0, 0)),
            pl.BlockSpec((1, s, hb * LANES), lambda bi, hg, i: (bi, 0, hg),
                         pipeline_mode=pl.Buffered(1)),
            pl.BlockSpec((1, hb, s // tk, ATT_V_ROWS, tk), lambda bi, hg, i: (bi, hg, 0, 0, 0),
                         pipeline_mode=pl.Buffered(1)),
            pl.BlockSpec((1, SUBLANES, LANES), lambda bi, hg, i: (bi, 0, 0)),
            pl.BlockSpec((1, s // tk, LANES), lambda bi, hg, i: (bi, 0, 0)),
        ],
        out_specs=pl.BlockSpec((1, tq, hb * ATT_HEAD_DIM), lambda bi, hg, i: (bi, i, hg)),
        scratch_shapes=[pltpu.VMEM((hb, tk, tq), F32), pltpu.VMEM((hb, tk, tq), F32),
                        pltpu.VMEM((hb, ATT_V_ROWS, tq), F32)],
        compiler_params=_cparams(("parallel", "parallel", "arbitrary")),
        name="fox_attention",
    )(qt, k, vt, kmax, bpre)


def _mlstm_kernel(q_ref, k_ref, v_ref, gcol_ref, grow_ref, spread_ref, nw_ref, o_ref,
                  st_ref, m_ref, *, nsub):
    @pl.when(pl.program_id(1) == 0)
    def _():
        st_ref[...] = jnp.zeros_like(st_ref)
        m_ref[...] = jnp.zeros_like(m_ref)

    L = CHUNK
    D = ML_HEAD_DIM
    H = range(ML_HEADS)
    r = lax.broadcasted_iota(jnp.int32, (L, L), 0)
    c = lax.broadcasted_iota(jnp.int32, (L, L), 1)
    causal = c <= r
    ones = jnp.ones((L, D), BF16)
    mean_mat = jnp.full((D, D), 1.0 / D, BF16)
    two = lambda x: jnp.concatenate([x, x], axis=-1)

    st = [st_ref[h] for h in H]
    m = [m_ref[h] for h in H]
    pre = []
    for sc in range(nsub):
        rows = slice(sc * L, (sc + 1) * L)
        head = lambda ref, h, rows=rows: ref[0, rows, h * D:(h + 1) * D]
        rep = _dot(jnp.concatenate(_split3(gcol_ref[0, rows, :]), axis=-1), spread_ref[...])
        qk = [_dot_nt(head(q_ref, h), head(k_ref, h)) for h in H]
        kt = [head(k_ref, h).astype(F32).T.astype(BF16) for h in H]
        pre.append((rows, head, rep, qk, kt))

    for sc in range(nsub):
        rows, head, rep, qk, kt = pre[sc]
        col = lambda j, h: rep[:, (j * ML_HEADS + h) * LANES:(j * ML_HEADS + h + 1) * LANES]
        inter = [_dot(head(q_ref, h), st[h].astype(BF16)) for h in H]
        m_t, w_intra, w_inter, decay, wv, m_next = [], [], [], [], [], []
        for h in H:
            b, gq, cm = col(0, h), col(1, h), col(2, h)
            gq_row = grow_ref[0, ML_HEADS + h:ML_HEADS + h + 1, rows]
            a = b + m[h]
            m_t.append(jnp.maximum(a, b + cm))
            w_intra.append(jnp.exp(jnp.where(causal, b + gq_row, NEG_BIG) - m_t[h]))
            w_inter.append(jnp.exp(a - m_t[h]))
            b_last = b[L - 1:L, :]
            m_new = jnp.maximum(b_last + m[h], b_last + cm[L - 1:L, :])
            w_s = jnp.exp(b_last + gq - m_new)
            decay.append(jnp.exp(b_last + m[h] - m_new))
            wv.append(jnp.concatenate([w_s * head(v_ref, h).astype(F32), w_s],
                                      axis=-1).astype(BF16))
            m_next.append(m_new)
        qkw = [(qk[h] * w_intra[h]).astype(BF16) for h in H]
        both = [_dot(qkw[h], jnp.concatenate([head(v_ref, h), ones], axis=-1))
                + two(w_inter[h]) * inter[h] for h in H]
        st = [two(decay[h]) * st[h] + _dot(kt[h], wv[h]) for h in H]
        m = m_next
        hh = [both[h][:, :D] / jnp.maximum(jnp.abs(both[h][:, D:]), jnp.exp(-m_t[h])) for h in H]
        ms = []
        for h in H:
            sq = hh[h] * hh[h]
            sq_hi = sq.astype(BF16)
            sq_lo = (sq - sq_hi.astype(F32)).astype(BF16)
            ms.append(_dot(sq_hi, mean_mat) + _dot(sq_lo, mean_mat))
        for h in H:
            o_ref[0, rows, h * D:(h + 1) * D] = (hh[h] * lax.rsqrt(ms[h] + EPS)
                                                 * nw_ref[:, h * D:(h + 1) * D])
    for h in H:
        st_ref[h] = st[h]
        m_ref[h] = m[h]


def _mlstm(mq, mk, mv, gcol, grow, norm_w, l, nsub):
    b, s, _ = mq.shape
    rows = nsub * CHUNK
    nc = s // rows
    row = lambda w: pl.BlockSpec((1, rows, w), lambda bi, ci: (bi, ci, 0))
    src = lax.broadcasted_iota(jnp.int32, (3 * LANES, ML_REP), 0) % LANES
    dst = lax.broadcasted_iota(jnp.int32, (3 * LANES, ML_REP), 1) // LANES
    spread = (src == dst).astype(BF16)
    return pl.pallas_call(
        functools.partial(_mlstm_kernel, nsub=nsub),
        out_shape=jax.ShapeDtypeStruct((b, s, ML_WIDTH), F32),
        grid=(b, nc),
        in_specs=[
            row(ML_WIDTH), row(ML_WIDTH), row(ML_WIDTH), row(LANES),
            pl.BlockSpec((1, GATE_ROWS, rows), lambda bi, ci: (bi, 0, ci)),
            _resident((3 * LANES, ML_REP)),
            _resident((1, ML_WIDTH), l),
        ],
        out_specs=row(ML_WIDTH),
        scratch_shapes=[
            pltpu.VMEM((ML_HEADS, ML_HEAD_DIM, 2 * ML_HEAD_DIM), F32),
            pltpu.VMEM((ML_HEADS, 1, LANES), F32),
        ],
        compiler_params=_cparams(("parallel", "arbitrary")),
        name="mlstm",
    )(mq, mk, mv, gcol, grow, spread, norm_w)


def _gelu_tanh(x):
    return 0.5 * x * (1.0 + jnp.tanh(0.7978845608028654 * (x + 0.044715 * (x * x * x))))


def _merge_kernel(x_ref, yatt_ref, hm_ref, g_ref, wo_ref, wuv_ref, wgt_ref, ng_ref,
                  wsp_ref, bsp_ref, wba_ref, wbm_ref, wbg_ref, wout_ref, o_ref, *, tm):
    x = x_ref[...]
    hn = _rms(x, g_ref[...]).astype(BF16)
    branch_gate = lambda j: jax.nn.sigmoid(_dot(hn, wgt_ref[:, j * D_MODEL:(j + 1) * D_MODEL]))

    uv_pre = _dot(hn, wuv_ref[...])
    o_pre = _dot(hn, wo_ref[...])
    merged = branch_gate(0) * _dot(yatt_ref[...], wba_ref[...])

    y_ml = jax.nn.sigmoid(o_pre) * hm_ref[...]
    merged = merged + branch_gate(1) * _dot(y_ml.astype(BF16), wbm_ref[...])

    uv = _gelu_tanh(uv_pre)
    u = uv[:, :GM_WIDTH]
    vn = _rms(uv[:, GM_WIDTH:], ng_ref[...]).astype(BF16)
    r = lax.broadcasted_iota(jnp.int32, (CHUNK, CHUNK), 0)
    c = lax.broadcasted_iota(jnp.int32, (CHUNK, CHUNK), 1)
    mixed = []
    for gi in range(GM_GROUPS):
        w = jnp.where(c <= r, wsp_ref[gi], 0.0).astype(BF16)
        lanes = slice(gi * GM_GROUP_DIM, (gi + 1) * GM_GROUP_DIM)
        mixed.append(jnp.concatenate(
            [_dot(w, vn[ci * CHUNK:(ci + 1) * CHUNK, lanes]) for ci in range(tm // CHUNK)], axis=0))
    y_gm = u * (jnp.concatenate(mixed, axis=-1) + bsp_ref[...])

    merged = merged + branch_gate(2) * _dot(y_gm.astype(BF16), wbg_ref[...])
    o_ref[...] = x + _dot(merged.astype(BF16), wout_ref[...])


def _merge(x2, yatt, hm, g, wo, wuv, wgt, ng, wsp, bsp, wba, wbm, wbg, wout, l, tm):
    n = x2.shape[0]
    row = lambda w: pl.BlockSpec((tm, w), lambda i: (i, 0))
    return pl.pallas_call(
        functools.partial(_merge_kernel, tm=tm),
        out_shape=jax.ShapeDtypeStruct((n, D_MODEL), F32),
        grid=(n // tm,),
        in_specs=[
            row(D_MODEL), row(ATT_WIDTH), row(ML_WIDTH),
            _resident((1, D_MODEL), l),
            _resident((D_MODEL, ML_WIDTH), l),
            _resident((D_MODEL, 2 * GM_WIDTH), l),
            _resident((D_MODEL, N_BRANCH * D_MODEL), l),
            _resident((1, GM_WIDTH), l),
            _resident((GM_GROUPS, CHUNK, CHUNK), l),
            _resident((tm, GM_WIDTH), l),
            _resident((ATT_WIDTH, D_MODEL), l),
            _resident((ML_WIDTH, D_MODEL), l),
            _resident((GM_WIDTH, D_MODEL), l),
            _resident((D_MODEL, D_MODEL), l),
        ],
        out_specs=row(D_MODEL),
        compiler_params=_cparams(("parallel",)),
        name="merge",
    )(x2, yatt, hm, g, wo, wuv, wgt, ng, wsp, bsp, wba, wbm, wbg, wout)


def _tile(n, pref):
    t = min(n, pref)
    assert n % t == 0, (n, t)
    return t


def _prepare_layer_params(p, tm_merge):
    depth = p["w_in"].shape[0]
    o_att = 3 * ATT_WIDTH
    o_mqk = o_att + ATT_HEADS
    o_mv = o_mqk + 2 * ML_WIDTH
    o_mi = o_mv + ML_WIDTH
    o_mf = o_mi + ML_HEADS
    o_mo = o_mf + ML_HEADS
    o_uv = o_mo + ML_WIDTH
    o_gt = o_uv + 2 * GM_WIDTH
    w_in = p["w_in"]
    w_small = jnp.concatenate(
        [w_in[:, :, o_att:o_mqk], w_in[:, :, o_mi:o_mf], w_in[:, :, o_mf:o_mo],
         jnp.zeros((depth, D_MODEL, LANES - GATE_ROWS), F32)], axis=-1)
    b_small = jnp.concatenate(
        [p["b_f_att"], p["b_i_ml"], p["b_f_ml"], jnp.zeros((depth, LANES - GATE_ROWS), F32)],
        axis=-1)[:, None, :]
    reps = tm_merge // CHUNK
    bf = lambda a: a.astype(BF16)
    row = lambda a: a[:, None, :]

    def slots(w, width):
        w = w.reshape(depth, D_MODEL, ATT_HEADS, ATT_HEAD_DIM)
        w = jnp.pad(w, ((0, 0), (0, 0), (0, 0), (0, width - ATT_HEAD_DIM)))
        return w.reshape(depth, D_MODEL, ATT_HEADS * width)

    w_q = slots(w_in[:, :, :ATT_WIDTH], ATT_V_ROWS) * (ATT_HEAD_DIM ** -0.5 * LOG2E)
    w_k = slots(w_in[:, :, ATT_WIDTH:2 * ATT_WIDTH], LANES)
    w_v = slots(w_in[:, :, 2 * ATT_WIDTH:o_att], ATT_V_ROWS)
    return dict(
        norm_ffn1=row(p["norm_ffn1"]), ffn1_gate=bf(p["ffn1_gate"]), ffn1_up=bf(p["ffn1_up"]),
        ffn1_down=bf(p["ffn1_down"]),
        norm_mix=row(p["norm_mix"]),
        w_qt=bf(jnp.swapaxes(w_q, 1, 2)), w_k=bf(w_k), w_vt=bf(jnp.swapaxes(w_v, 1, 2)),
        w_small=bf(w_small), b_small=b_small,
        w_mqk=bf(w_in[:, :, o_mqk:o_mv]), w_mv=bf(w_in[:, :, o_mv:o_mi]),
        w_mo=bf(w_in[:, :, o_mo:o_uv]), w_uv=bf(w_in[:, :, o_uv:o_gt]), w_gt=bf(w_in[:, :, o_gt:]),
        conv_ml=p["conv_ml"], norm_ml_head=row(p["norm_ml_head"]), norm_gmlp=row(p["norm_gmlp"]),
        w_spatial=p["w_spatial"],
        b_spatial=jnp.tile(jnp.repeat(jnp.swapaxes(p["b_spatial"], 1, 2), GM_GROUP_DIM, axis=2),
                           (1, reps, 1)),
        w_br_att=bf(p["w_br_att"]), w_br_ml=bf(p["w_br_ml"]), w_br_gmlp=bf(p["w_br_gmlp"]),
        w_out=bf(p["w_out"]),
        norm_ffn2=row(p["norm_ffn2"]), ffn2_gate=bf(p["ffn2_gate"]), ffn2_up=bf(p["ffn2_up"]),
        ffn2_down=bf(p["ffn2_down"]),
    )


def _layer(x2, lp, l, b, s, tiles, final_g):
    n = b * s
    x2 = _ffn(x2, lp["norm_ffn1"], lp["ffn1_gate"], lp["ffn1_up"], lp["ffn1_down"], l,
              tiles["ffn"])
    qt, k, vt, small, mq, mk, mv = _inproj(
        x2.reshape(b, s, D_MODEL), lp["norm_mix"], lp["w_qt"], lp["w_k"], lp["w_vt"],
        lp["w_small"], lp["w_mqk"], lp["w_mv"], lp["conv_ml"], l, tiles["inproj"], tiles["att_q"],
        tiles["att_k"])
    gcol, grow, k, kmax, bpre = _gates(small, lp["b_small"], k, l, tiles["gates"], tiles["att_k"])
    bpre = bpre[:, :, :tiles["gates"] // tiles["att_k"], :].reshape(b, s // tiles["att_k"], LANES)
    yatt = _attention(qt, k, vt, kmax, bpre, tiles["att_q"], tiles["att_k"], tiles["att_heads"])
    hm = _mlstm(mq, mk, mv, gcol, grow, lp["norm_ml_head"], l, tiles["mlstm_chunks"])
    x2 = _merge(x2, yatt.reshape(n, ATT_WIDTH), hm.reshape(n, ML_WIDTH), lp["norm_mix"],
                lp["w_mo"], lp["w_uv"], lp["w_gt"], lp["norm_gmlp"], lp["w_spatial"],
                lp["b_spatial"], lp["w_br_att"], lp["w_br_ml"], lp["w_br_gmlp"], lp["w_out"],
                l, tiles["merge"])
    return _ffn(x2, lp["norm_ffn2"], lp["ffn2_gate"], lp["ffn2_up"], lp["ffn2_down"], l,
                tiles["ffn"], final_g)


def _tiles_for(n, s):
    att_q = _tile(s, 512)
    return dict(ffn=_tile(n, 512), inproj=_tile(s, 1024), gates=_tile(s, 512), att_q=att_q,
                att_k=min(att_q // 2, 256), att_heads=2, merge=_tile(s, 512),
                mlstm_chunks=_tile(s, 8 * CHUNK) // CHUNK)


def _trunk(x, params, norm_final):
    b, s, _ = x.shape
    n = b * s
    tiles = _tiles_for(n, s)
    stacked = _prepare_layer_params(params, tiles["merge"])
    depth = params["w_in"].shape[0]
    x2 = x.reshape(n, D_MODEL)
    for l in range(depth):
        x2 = _layer(x2, stacked, l, b, s, tiles, norm_final[None, :] if l == depth - 1 else None)
    return x2.reshape(b, s, D_MODEL)


def kernel(x, norm_ffn1, ffn1_gate, ffn1_up, ffn1_down, norm_mix, w_in, b_f_att, b_i_ml, b_f_ml, conv_ml, norm_ml_head, norm_gmlp, w_spatial, b_spatial, w_br_att, w_br_ml, w_br_gmlp, w_out, norm_ffn2, ffn2_gate, ffn2_up, ffn2_down, norm_final):
    params = dict(norm_ffn1=norm_ffn1, ffn1_gate=ffn1_gate, ffn1_up=ffn1_up, ffn1_down=ffn1_down,
                  norm_mix=norm_mix, w_in=w_in, b_f_att=b_f_att, b_i_ml=b_i_ml, b_f_ml=b_f_ml,
                  conv_ml=conv_ml, norm_ml_head=norm_ml_head, norm_gmlp=norm_gmlp,
                  w_spatial=w_spatial, b_spatial=b_spatial, w_br_att=w_br_att, w_br_ml=w_br_ml,
                  w_br_gmlp=w_br_gmlp, w_out=w_out, norm_ffn2=norm_ffn2, ffn2_gate=ffn2_gate,
                  ffn2_up=ffn2_up, ffn2_down=ffn2_down)
    return _trunk(x, params, norm_final)
```
